```python
import math
import jax, jax.numpy as jnp
from jax import lax
import numpy as np

D_MODEL = 1024
BATCH = 4
SEQ = 4096
DEPTH = 1
DEC_BATCH = 128
DEC_SEQ = 8
PAST_LEN = 2048
PAGE_SIZE = 128

HEAD_DIM = 64
D_MIX = D_MODEL
H_NSA = (D_MIX // 2) // HEAD_DIM
H_FOX = (D_MIX // 2) // HEAD_DIM
G_NSA = 2
R_NSA = H_NSA // G_NSA
CMP_LEN = 32
CMP_STRIDE = 16
CMP_HID = 2 * HEAD_DIM
SEL_BLOCK = 64
SEL_TOP = 16
WINDOW = 512
Q_BLOCK = 128
ROPE_THETA = 10000.0
EPS = 1e-6
NEG = -1e30
FORCED = 1e4

SIZES = [H_NSA * HEAD_DIM] + [G_NSA * HEAD_DIM] * 6 + [3 * H_NSA, H_NSA * HEAD_DIM] + [H_FOX * HEAD_DIM] * 3 + [H_FOX, H_FOX * HEAD_DIM]
SPLIT_POINTS = tuple(int(v) for v in np.cumsum(SIZES)[:-1])
N_IN = int(sum(SIZES))

kernel_name = 'nsa_fox_hymba_decode_step'


def rms_norm(x, g):
    x32 = x.astype(jnp.float32)
    y = x32 * lax.rsqrt(jnp.mean(x32 * x32, axis=-1, keepdims=True) + EPS)
    return (y * g.astype(jnp.float32)).astype(x.dtype)


def rope(x, pos):
    half = HEAD_DIM // 2
    inv = jnp.power(jnp.float32(ROPE_THETA), -jnp.arange(half, dtype=jnp.float32) / half)
    ang = pos.astype(jnp.float32)[:, None] * inv[None, :]
    cos = jnp.cos(ang)[None, :, None, :]
    sin = jnp.sin(ang)[None, :, None, :]
    x32 = x.astype(jnp.float32)
    x1, x2 = x32[..., :half], x32[..., half:]
    return jnp.concatenate([x1 * cos - x2 * sin, x2 * cos + x1 * sin], axis=-1).astype(x.dtype)


def masked_softmax(s, mask):
    p = jax.nn.softmax(jnp.where(mask, s, NEG), axis=-1)
    return jnp.where(mask, p, 0.0)


def compress(x, pe, w1, w2):
    L = x.shape[1]
    n_cmp = (L - CMP_LEN) // CMP_STRIDE + 1
    idx = np.arange(n_cmp)[:, None] * CMP_STRIDE + np.arange(CMP_LEN)[None, :]
    blocks = x[:, idx] + pe[None, None, :, None, :]
    h = jax.nn.silu(jnp.einsum('bnlgd,ldh->bngh', blocks, w1))
    return jnp.einsum('bngh,hd->bngd', h, w2)


def cmp_to_sel(n_cmp, n_sel):
    cs = np.arange(n_cmp) * CMP_STRIDE
    ss = np.arange(n_sel) * SEL_BLOCK
    ov = np.clip(np.minimum(cs[:, None] + CMP_LEN, ss[None, :] + SEL_BLOCK) - np.maximum(cs[:, None], ss[None, :]), 0, None)
    return jnp.asarray(ov / CMP_LEN, dtype=jnp.float32)


def to_sel_blocks(t):
    B, L = t.shape[:2]
    n_sel = -(-L // SEL_BLOCK)
    t = jnp.pad(t, ((0, 0), (0, n_sel * SEL_BLOCK - L), (0, 0), (0, 0)))
    return t.reshape(B, n_sel, SEL_BLOCK, G_NSA, HEAD_DIM).transpose(0, 3, 1, 2, 4)


def nsa_core(q, gates, q_pos, kc, vc, ks_blk, vs_blk, kw, vw, kw_pos):
    B, Tq = q.shape[:2]
    scale = HEAD_DIM ** -0.5
    n_cmp = kc.shape[1]
    c_end = jnp.arange(n_cmp, dtype=jnp.int32) * CMP_STRIDE + CMP_LEN - 1
    c_mask = c_end[None, :] <= q_pos[:, None]
    s = jnp.einsum('bqgrd,bngd->bgrqn', q, kc).astype(jnp.float32) * scale
    p_c = masked_softmax(s, c_mask)
    o_c = jnp.einsum('bgrqn,bngd->bqgrd', p_c.astype(vc.dtype), vc)
    n_sel = ks_blk.shape[2]
    imp = jnp.einsum('bgrqn,nj->bgqj', p_c, cmp_to_sel(n_cmp, n_sel))
    blk = jnp.arange(n_sel, dtype=jnp.int32)
    forced = (blk[None, :] == 0) | (blk[None, :] == (q_pos // SEL_BLOCK)[:, None])
    causal = blk[None, :] * SEL_BLOCK <= q_pos[:, None]
    imp = jnp.where(forced, FORCED, jnp.where(causal, imp, -1.0))
    n_top = min(SEL_TOP, n_sel)
    _, idx = lax.top_k(imp, n_top)
    gather = jax.vmap(jax.vmap(lambda kb, ib: kb[ib]))
    ksel = gather(ks_blk, idx).reshape(B, G_NSA, Tq, n_top * SEL_BLOCK, HEAD_DIM)
    vsel = gather(vs_blk, idx).reshape(B, G_NSA, Tq, n_top * SEL_BLOCK, HEAD_DIM)
    kpos = (idx[..., None] * SEL_BLOCK + jnp.arange(SEL_BLOCK, dtype=jnp.int32)).reshape(B, G_NSA, Tq, n_top * SEL_BLOCK)
    s_mask = (kpos <= q_pos[None, None, :, None])[:, :, None]
    s = jnp.einsum('bqgrd,bgqkd->bgrqk', q, ksel).astype(jnp.float32) * scale
    p_s = masked_softmax(s, s_mask)
    o_s = jnp.einsum('bgrqk,bgqkd->bqgrd', p_s.astype(vsel.dtype), vsel)
    dist = q_pos[:, None] - kw_pos[None, :]
    w_mask = (dist >= 0) & (dist <= WINDOW) & (kw_pos[None, :] >= 0)
    s = jnp.einsum('bqgrd,bkgd->bgrqk', q, kw).astype(jnp.float32) * scale
    p_w = masked_softmax(s, w_mask)
    o_w = jnp.einsum('bgrqk,bkgd->bqgrd', p_w.astype(vw.dtype), vw)
    return gates[..., 0:1] * o_c + gates[..., 1:2] * o_s + gates[..., 2:3] * o_w


def fox_core(q, cq, q_pos, k, v, ck, k_pos):
    s = jnp.einsum('bqhd,bkhd->bhqk', q, k).astype(jnp.float32) * (HEAD_DIM ** -0.5)
    s = s + (jnp.transpose(cq, (0, 2, 1))[:, :, :, None] - jnp.transpose(ck, (0, 2, 1))[:, :, None, :])
    p = masked_softmax(s, k_pos[None, :] <= q_pos[:, None])
    return jnp.einsum('bhqk,bkhd->bqhd', p.astype(v.dtype), v)


def mixer_inputs(x, pos, g_norm, w_in, b_f, gq_a, gk_slc, gk_win, gq_b, gk_b):
    B, T, _ = x.shape
    h = rms_norm(x, g_norm)
    proj = jnp.einsum('btd,dn->btn', h, w_in)
    qa, kc, vc, ks, vs, kw, vw, ga, za, qb, kb, vb, fb, zb = jnp.split(proj, SPLIT_POINTS, axis=-1)
    kvh = lambda t: t.reshape(B, T, G_NSA, HEAD_DIM)
    fh = lambda t: t.reshape(B, T, H_FOX, HEAD_DIM)
    qa = rope(rms_norm(qa.reshape(B, T, H_NSA, HEAD_DIM), gq_a), pos).reshape(B, T, G_NSA, R_NSA, HEAD_DIM)
    kc = rope(kvh(kc), pos)
    ks = rope(rms_norm(kvh(ks), gk_slc), pos)
    kw = rope(rms_norm(kvh(kw), gk_win), pos)
    ga = jax.nn.sigmoid(ga).reshape(B, T, G_NSA, R_NSA, 3)
    qb = rms_norm(fh(qb), gq_b)
    kb = rms_norm(fh(kb), gk_b)
    logf = jax.nn.log_sigmoid(fb.astype(jnp.float32) + b_f.astype(jnp.float32))
    return qa, kc, kvh(vc), ks, kvh(vs), kw, kvh(vw), ga, za, qb, kb, fh(vb), logf, zb


def merge_out(x, oa, za, ob, zb, w_out):
    B, T, _ = x.shape
    u = jnp.concatenate([oa.reshape(B, T, -1) * jax.nn.silu(za), ob.reshape(B, T, -1) * jax.nn.silu(zb)], axis=-1)
    return x + jnp.einsum('btm,md->btd', u, w_out).astype(x.dtype)


def prompt_layer(x, lw):
    g_norm, w_in, b_f, gq_a, gk_cmp, gk_slc, gk_win, pe_k, pe_v, w1k, w2k, w1v, w2v, gq_b, gk_b, w_out = lw
    B, T, _ = x.shape
    pos = jnp.arange(T, dtype=jnp.int32)
    qa, kc, vc, ks, vs, kw, vw, ga, za, qb, kb, vb, logf, zb = mixer_inputs(x, pos, g_norm, w_in, b_f, gq_a, gk_slc, gk_win, gq_b, gk_b)
    kcmp = rms_norm(compress(kc, pe_k, w1k, w2k), gk_cmp)
    vcmp = compress(vc, pe_v, w1v, w2v)
    ks_blk, vs_blk = to_sel_blocks(ks), to_sel_blocks(vs)
    pad = ((0, 0), (WINDOW, 0), (0, 0), (0, 0))
    kw_pad, vw_pad = jnp.pad(kw, pad), jnp.pad(vw, pad)
    c = jnp.cumsum(logf, axis=1)
    n_blk = T // Q_BLOCK

    def nsa_block(i):
        s0 = i * Q_BLOCK
        q_pos = s0 + jnp.arange(Q_BLOCK, dtype=jnp.int32)
        kw_pos = s0 - WINDOW + jnp.arange(WINDOW + Q_BLOCK, dtype=jnp.int32)
        return nsa_core(lax.dynamic_slice_in_dim(qa, s0, Q_BLOCK, 1), lax.dynamic_slice_in_dim(ga, s0, Q_BLOCK, 1), q_pos,
                        kcmp, vcmp, ks_blk, vs_blk,
                        lax.dynamic_slice_in_dim(kw_pad, s0, WINDOW + Q_BLOCK, 1),
                        lax.dynamic_slice_in_dim(vw_pad, s0, WINDOW + Q_BLOCK, 1), kw_pos)

    def fox_block(i):
        s0 = i * Q_BLOCK
        q_pos = s0 + jnp.arange(Q_BLOCK, dtype=jnp.int32)
        return fox_core(lax.dynamic_slice_in_dim(qb, s0, Q_BLOCK, 1), lax.dynamic_slice_in_dim(c, s0, Q_BLOCK, 1), q_pos,
                        kb, vb, c, pos)

    blocks = jnp.arange(n_blk, dtype=jnp.int32)
    oa = jnp.moveaxis(lax.map(nsa_block, blocks), 0, 1).reshape(B, T, -1)
    ob = jnp.moveaxis(lax.map(fox_block, blocks), 0, 1).reshape(B, T, -1)
    y = merge_out(x, oa, za, ob, zb, w_out)
    wb = min(WINDOW, T)
    return (y, jnp.stack([kc, vc], axis=2), jnp.stack([ks, vs], axis=2), jnp.stack([kw[:, T - wb:], vw[:, T - wb:]], axis=2),
            jnp.stack([kb, vb], axis=2), logf)


def sample_layer(x, cmp_pool, slc_pool, win_buf, fox_pool, logf_pool, page_table, lw):
    g_norm, w_in, b_f, gq_a, gk_cmp, gk_slc, gk_win, pe_k, pe_v, w1k, w2k, w1v, w2v, gq_b, gk_b, w_out = lw
    DB, S, _ = x.shape
    past = page_table.shape[1] * PAGE_SIZE
    wb = win_buf.shape[1]
    pos = past + jnp.arange(S, dtype=jnp.int32)
    qa, kc, vc, ks, vs, kw, vw, ga, za, qb, kb, vb, logf, zb = mixer_inputs(x, pos, g_norm, w_in, b_f, gq_a, gk_slc, gk_win, gq_b, gk_b)

    def gather_pages(pool):
        return pool[page_table].reshape((DB, past) + pool.shape[2:])

    cmp_past = gather_pages(cmp_pool)
    slc_past = gather_pages(slc_pool)
    fox_past = gather_pages(fox_pool)
    logf_past = gather_pages(logf_pool)
    kc_all = jnp.concatenate([cmp_past[:, :, 0], kc], axis=1)
    vc_all = jnp.concatenate([cmp_past[:, :, 1], vc], axis=1)
    ks_all = jnp.concatenate([slc_past[:, :, 0], ks], axis=1)
    vs_all = jnp.concatenate([slc_past[:, :, 1], vs], axis=1)
    kw_all = jnp.concatenate([win_buf[:, :, 0], kw], axis=1)
    vw_all = jnp.concatenate([win_buf[:, :, 1], vw], axis=1)
    kcmp = rms_norm(compress(kc_all, pe_k, w1k, w2k), gk_cmp)
    vcmp = compress(vc_all, pe_v, w1v, w2v)
    kw_pos = past - wb + jnp.arange(wb + S, dtype=jnp.int32)
    oa = nsa_core(qa, ga, pos, kcmp, vcmp, to_sel_blocks(ks_all), to_sel_blocks(vs_all), kw_all, vw_all, kw_pos)
    c = jnp.cumsum(jnp.concatenate([logf_past.astype(jnp.float32), logf], axis=1), axis=1)
    kb_all = jnp.concatenate([fox_past[:, :, 0], kb], axis=1)
    vb_all = jnp.concatenate([fox_past[:, :, 1], vb], axis=1)
    ob = fox_core(qb, c[:, past:], pos, kb_all, vb_all, c, jnp.arange(past + S, dtype=jnp.int32))
    y = merge_out(x, oa, za, ob, zb, w_out)
    new_win = jnp.stack([kw_all[:, -wb:], vw_all[:, -wb:]], axis=2)
    return (y, jnp.stack([kc, vc], axis=2), jnp.stack([ks, vs], axis=2), new_win, jnp.stack([kb, vb], axis=2), logf)


def setup_inputs(seed: int = 0) -> dict:
    key = jax.random.key(seed)
    k = jax.random.split(key, 24)
    n_pages = PAST_LEN // PAGE_SIZE
    n_used = DEC_BATCH * n_pages
    n_pool = n_used + max(1, n_used // 4)
    wb = min(WINDOW, PAST_LEN)
    nrm = jax.random.normal
    f32 = jnp.float32
    gain = lambda kk, shape: 1.0 + 0.02 * nrm(kk, shape, f32)
    return {
        'x_prompt': nrm(k[0], (BATCH, SEQ, D_MODEL), f32),
        'x_sample': nrm(k[1], (DEC_BATCH, DEC_SEQ, D_MODEL), f32),
        'cache_nsa_cmp_kv': nrm(k[2], (DEPTH, n_pool, PAGE_SIZE, 2, G_NSA, HEAD_DIM), f32),
        'cache_nsa_slc_kv': nrm(k[3], (DEPTH, n_pool, PAGE_SIZE, 2, G_NSA, HEAD_DIM), f32),
        'cache_nsa_win_kv': nrm(k[4], (DEPTH, DEC_BATCH, wb, 2, G_NSA, HEAD_DIM), f32),
        'cache_fox_kv': nrm(k[5], (DEPTH, n_pool, PAGE_SIZE, 2, H_FOX, HEAD_DIM), f32),
        'cache_fox_logf': jax.nn.log_sigmoid(2.0 + 0.5 * nrm(k[6], (DEPTH, n_pool, PAGE_SIZE, H_FOX), f32)),
        'page_table': jax.random.permutation(k[7], n_pool)[:n_used].reshape(DEC_BATCH, n_pages).astype(jnp.int32),
        'g_norm': gain(k[8], (DEPTH, D_MODEL)),
        'w_in': nrm(k[9], (DEPTH, D_MODEL, N_IN), f32) * D_MODEL ** -0.5,
        'b_f': 2.0 + 0.5 * nrm(k[10], (DEPTH, H_FOX), f32),
        'gq_a': gain(k[11], (DEPTH, HEAD_DIM)),
        'gk_cmp': gain(k[12], (DEPTH, HEAD_DIM)),
        'gk_slc': gain(k[13], (DEPTH, HEAD_DIM)),
        'gk_win': gain(k[14], (DEPTH, HEAD_DIM)),
        'pe_cmp_k': 0.1 * nrm(k[15], (DEPTH, CMP_LEN, HEAD_DIM), f32),
        'pe_cmp_v': 0.1 * nrm(k[16], (DEPTH, CMP_LEN, HEAD_DIM), f32),
        'w_cmp1_k': nrm(k[17], (DEPTH, CMP_LEN, HEAD_DIM, CMP_HID), f32) * (CMP_LEN * HEAD_DIM) ** -0.5,
        'w_cmp2_k': nrm(k[18], (DEPTH, CMP_HID, HEAD_DIM), f32) * CMP_HID ** -0.5,
        'w_cmp1_v': nrm(k[19], (DEPTH, CMP_LEN, HEAD_DIM, CMP_HID), f32) * (CMP_LEN * HEAD_DIM) ** -0.5,
        'w_cmp2_v': nrm(k[20], (DEPTH, CMP_HID, HEAD_DIM), f32) * CMP_HID ** -0.5,
        'gq_b': gain(k[21], (DEPTH, HEAD_DIM)),
        'gk_b': gain(k[22], (DEPTH, HEAD_DIM)),
        'w_out': nrm(k[23], (DEPTH, D_MIX, D_MODEL), f32) * D_MIX ** -0.5,
    }


def reference(x_prompt, x_sample, cache_nsa_cmp_kv, cache_nsa_slc_kv, cache_nsa_win_kv, cache_fox_kv, cache_fox_logf,
              page_table, g_norm, w_in, b_f, gq_a, gk_cmp, gk_slc, gk_win, pe_cmp_k, pe_cmp_v,
              w_cmp1_k, w_cmp2_k, w_cmp1_v, w_cmp2_v, gq_b, gk_b, w_out):
    yp, ys = x_prompt, x_sample
    pst = [[] for _ in range(5)]
    sst = [[] for _ in range(5)]
    for l in range(DEPTH):
        lw = (g_norm[l], w_in[l], b_f[l], gq_a[l], gk_cmp[l], gk_slc[l], gk_win[l], pe_cmp_k[l], pe_cmp_v[l],
              w_cmp1_k[l], w_cmp2_k[l], w_cmp1_v[l], w_cmp2_v[l], gq_b[l], gk_b[l], w_out[l])
        yp, *p_new = prompt_layer(yp, lw)
        ys, *s_new = sample_layer(ys, cache_nsa_cmp_kv[l], cache_nsa_slc_kv[l], cache_nsa_win_kv[l],
                                  cache_fox_kv[l], cache_fox_logf[l], page_table, lw)
        for j in range(5):
            pst[j].append(p_new[j])
            sst[j].append(s_new[j])
    p_cmp, p_slc, p_win, p_fox, p_logf = [jnp.stack(a, axis=0) for a in pst]
    s_cmp, s_slc, s_win, s_fox, s_logf = [jnp.stack(a, axis=0) for a in sst]
    return (yp, ys, p_cmp, s_cmp, p_slc, s_slc, p_win, s_win, p_fox, s_fox, p_logf, s_logf)
```

```python
import functools

import numpy as np
import jax
import jax.numpy as jnp
from jax import lax
from jax.experimental import pallas as pl
from jax.experimental.pallas import tpu as pltpu

F32 = jnp.float32
BF16 = jnp.bfloat16

HEAD_DIM = 64
LANES = 128
N_HEADS = 8
G_NSA = 2
R_NSA = N_HEADS // G_NSA
CMP_LEN = 32
CMP_STRIDE = 16
CMP_HID = 2 * HEAD_DIM
SEL_BLOCK = 64
SEL_TOP = 16
WINDOW = 512
PAGE_SIZE = 128
ROPE_THETA = 10000.0
EPS = 1e-6
NEG = -1e30
FORCED = 1e4
QK_SCALE = HEAD_DIM ** -0.5

VMEM_LIMIT = 56 * 1024 * 1024

C_QA, C_CMP, C_SLC, C_WIN, C_Z, C_QB, C_FOX, C_SMALL, C_END = 0, 512, 768, 1024, 1280, 2304, 2816, 3840, 3968
GATE_LANES = 3 * N_HEADS
LOGF_LANE = GATE_LANES


def _dot(a, b):
    return jnp.dot(a, b, preferred_element_type=F32)


def _dot_nt(a, b):
    return lax.dot_general(a, b, (((1,), (1,)), ((), ())), preferred_element_type=F32)


def _split2(x):
    hi = x.astype(BF16)
    lo = (x - hi.astype(F32)).astype(BF16)
    return hi, lo


def _split3(x):
    hi = x.astype(BF16)
    r = x - hi.astype(F32)
    mid = r.astype(BF16)
    lo = (r - mid.astype(F32)).astype(BF16)
    return hi, mid, lo


def _dot_exact_lhs(x, m):
    a, b, c = _split3(x)
    return _dot(a, m) + _dot(b, m) + _dot(c, m)


def _dot_exact_rhs(m, x):
    a, b, c = _split3(x)
    return _dot(m, a) + _dot(m, b) + _dot(m, c)


def _sigmoid(x):
    return 1.0 / (1.0 + jnp.exp(-x))


def _params(sem):
    return pltpu.CompilerParams(dimension_semantics=sem, vmem_limit_bytes=VMEM_LIMIT)


def _proj_kernel(x_ref, g_ref, w_ref, cos_ref, sin_ref, gains_ref, bf_ref, ones_ref, tri_ref, hexp_ref,
                 cmp_ref, slc_ref, win_ref, fox_ref, small_ref, z_ref, qa_ref, qb_ref, cexp_ref,
                 slcb_ref, winb_ref, foxb_ref, carry_ref, *, tm, tiles_per_seq):
    i = pl.program_id(0)
    x = x_ref[...]
    ms = jnp.mean(x * x, axis=-1, keepdims=True)
    h = (x * lax.rsqrt(ms + EPS) * g_ref[...]).astype(BF16)
    cos = cos_ref[...]
    sin = sin_ref[...]
    lane = lax.broadcasted_iota(jnp.int32, (tm, LANES), 1)
    first_half = (lane & (HEAD_DIM // 2)) == 0
    ones = ones_ref[...]

    def proj(c0, c1):
        return _dot(h, w_ref[:, c0:c1])

    def head_norm(y, row):
        hi, lo = _split2(y * y)
        ss = _dot(hi, ones) + _dot(lo, ones)
        return y * lax.rsqrt(ss * (1.0 / HEAD_DIM) + EPS) * gains_ref[row:row + 1, :]

    def rope(y):
        partner = jnp.where(first_half, pltpu.roll(y, LANES - HEAD_DIM // 2, 1), pltpu.roll(y, HEAD_DIM // 2, 1))
        return y * cos + partner * sin

    def chunk(y, c):
        return y[:, c * LANES:(c + 1) * LANES]

    y = proj(C_QA, C_CMP)
    for c in range(4):
        qa_ref[:, c * LANES:(c + 1) * LANES] = rope(head_norm(chunk(y, c), 0)) * QK_SCALE

    y = proj(C_CMP, C_SLC)
    cmp_ref[:, 0:LANES] = rope(chunk(y, 0))
    cmp_ref[:, LANES:2 * LANES] = chunk(y, 1)

    for (c0, c1, row, o_ref, ob_ref) in ((C_SLC, C_WIN, 1, slc_ref, slcb_ref), (C_WIN, C_Z, 2, win_ref, winb_ref)):
        y = proj(c0, c1)
        k = rope(head_norm(chunk(y, 0), row))
        v = chunk(y, 1)
        o_ref[:, 0:LANES] = k
        o_ref[:, LANES:2 * LANES] = v
        ob_ref[:, 0:LANES] = k.astype(BF16)
        ob_ref[:, LANES:2 * LANES] = v.astype(BF16)

    z_ref[...] = proj(C_Z, C_QB)

    y = proj(C_QB, C_FOX)
    for c in range(4):
        qb_ref[:, c * LANES:(c + 1) * LANES] = head_norm(chunk(y, c), 3) * QK_SCALE

    y = proj(C_FOX, C_SMALL)
    for c in range(8):
        t = chunk(y, c)
        if c < 4:
            t = head_norm(t, 4)
        fox_ref[:, c * LANES:(c + 1) * LANES] = t
        foxb_ref[:, c * LANES:(c + 1) * LANES] = t.astype(BF16)

    raw = proj(C_SMALL, C_END)
    zf = raw + bf_ref[...]
    logf = jnp.minimum(zf, 0.0) - jnp.log1p(jnp.exp(-jnp.abs(zf)))
    small = jnp.where(lane < GATE_LANES, _sigmoid(raw), jnp.where(lane < GATE_LANES + N_HEADS, logf, 0.0))
    small_ref[...] = small

    @pl.when(i % tiles_per_seq == 0)
    def _():
        carry_ref[...] = jnp.zeros_like(carry_ref)

    c = carry_ref[0:1, :] + _dot_exact_rhs(tri_ref[...], small)
    carry_ref[0:1, :] = c[tm - 1:tm, :]
    cexp_ref[...] = _dot_exact_lhs(c, hexp_ref[...])


def _proj_call(x2d, tables, consts, tiles_per_seq, tm, name):
    n = x2d.shape[0]
    cos_t, sin_t, table_map = tables
    g_norm, w_bf, gains, bf_row, ones_bd, tri, hexp = consts
    row = lambda i: (i, 0)
    fixed = lambda i: (0, 0)
    full = lambda a: pl.BlockSpec(a.shape, fixed)
    out_widths = (256, 256, 256, 1024, 128, 1024, 512, 512, 512)
    out_shape = [jax.ShapeDtypeStruct((n, w), F32) for w in out_widths]
    out_shape += [jax.ShapeDtypeStruct((n, w), BF16) for w in (256, 256, 1024)]
    out_specs = [pl.BlockSpec((tm, w), row) for w in out_widths + (256, 256, 1024)]
    return pl.pallas_call(
        functools.partial(_proj_kernel, tm=tm, tiles_per_seq=tiles_per_seq),
        grid=(n // tm,),
        in_specs=[pl.BlockSpec((tm, x2d.shape[1]), row), full(g_norm), full(w_bf),
                  pl.BlockSpec((tm, LANES), table_map), pl.BlockSpec((tm, LANES), table_map),
                  full(gains), full(bf_row), full(ones_bd), full(tri), full(hexp)],
        out_specs=out_specs,
        out_shape=out_shape,
        scratch_shapes=[pltpu.VMEM((8, LANES), F32)],
        compiler_params=_params(("arbitrary",)),
        name=name,
    )(x2d, g_norm, w_bf, cos_t, sin_t, gains, bf_row, ones_bd, tri, hexp)


def _compress_body(y_refs, wk_ref, wv_ref, pek_ref, pev_ref, w2k_ref, w2v_ref, gain_ref, ones_ref, out_ref,
                   rows_per_seq):
    if len(y_refs) == 1:
        y = y_refs[0][...].astype(BF16)
    else:
        y = jnp.concatenate([r[...] for r in y_refs], axis=0).astype(BF16)
    m = y.shape[0]
    chunk_tokens = CMP_STRIDE
    width = 2 * G_NSA * HEAD_DIM
    hid = G_NSA * CMP_HID

    def branch(off, w1_ref, pe_ref, w2_ref):
        ysel = jnp.concatenate([y[:, l * width + off:l * width + off + LANES] for l in range(chunk_tokens)], axis=1)
        r = _dot(jnp.concatenate([ysel, pe_ref[...]], axis=0), w1_ref[...])
        hh = r[:m, :hid] + pltpu.roll(r[:m, hid:], m - 1, 0) + r[m:m + 1, :hid] + r[m + 1:m + 2, hid:]
        act = hh * _sigmoid(hh)
        return _dot(act.astype(BF16), w2_ref[...])

    kc = branch(0, wk_ref, pek_ref, w2k_ref)
    hi, lo = _split2(kc * kc)
    ss = _dot(hi, ones_ref[...]) + _dot(lo, ones_ref[...])
    kc = kc * lax.rsqrt(ss * (1.0 / HEAD_DIM) + EPS) * gain_ref[...]
    vc = branch(LANES, wv_ref, pev_ref, w2v_ref)
    rows = lax.broadcasted_iota(jnp.int32, (m, LANES), 0)
    valid = (rows % rows_per_seq) != rows_per_seq - 1
    out_ref[:, 0:LANES] = jnp.where(valid, kc, 0.0)
    out_ref[:, LANES:2 * LANES] = jnp.where(valid, vc, 0.0)


def _compress_prompt_kernel(y_ref, *rest, rows_per_seq):
    _compress_body([y_ref], *rest, rows_per_seq=rows_per_seq)


def _compress_sample_kernel(pt_ref, *refs, n_pages, rows_per_seq):
    del pt_ref
    _compress_body(list(refs[:n_pages]), *refs[n_pages:], rows_per_seq=rows_per_seq)


def _compress_prompt_call(y2d, cw, rows_per_seq):
    n = y2d.shape[0]
    fixed = lambda i: (0, 0)
    return pl.pallas_call(
        functools.partial(_compress_prompt_kernel, rows_per_seq=rows_per_seq),
        grid=(n // rows_per_seq,),
        in_specs=[pl.BlockSpec((rows_per_seq, y2d.shape[1]), lambda i: (i, 0))] + [pl.BlockSpec(a.shape, fixed) for a in cw],
        out_specs=pl.BlockSpec((rows_per_seq, 2 * LANES), lambda i: (i, 0)),
        out_shape=jax.ShapeDtypeStruct((n, 2 * LANES), F32),
        compiler_params=_params(("arbitrary",)),
        name="compress_prompt",
    )(y2d, *cw)


def _compress_sample_call(pt_flat, pool3d, cw, n_batch, pages_per_seq, batch_per_step):
    rows_per_page = pool3d.shape[1]
    rows_per_seq = pages_per_seq * rows_per_page
    n_pages = batch_per_step * pages_per_seq
    fixed = lambda i, pt: (0, 0)

    def page_spec(k):
        bb, j = divmod(k, pages_per_seq)
        return pl.BlockSpec((None, rows_per_page, pool3d.shape[2]),
                            lambda i, pt: (pt[(i * batch_per_step + bb) * pages_per_seq + j], 0, 0))

    m = batch_per_step * rows_per_seq
    grid_spec = pltpu.PrefetchScalarGridSpec(
        num_scalar_prefetch=1,
        grid=(n_batch // batch_per_step,),
        in_specs=[page_spec(k) for k in range(n_pages)] + [pl.BlockSpec(a.shape, fixed) for a in cw],
        out_specs=pl.BlockSpec((m, 2 * LANES), lambda i, pt: (i, 0)),
    )
    return pl.pallas_call(
        functools.partial(_compress_sample_kernel, n_pages=n_pages, rows_per_seq=rows_per_seq),
        grid_spec=grid_spec,
        out_shape=jax.ShapeDtypeStruct((n_batch * rows_per_seq, 2 * LANES), F32),
        compiler_params=_params(("arbitrary",)),
        name="compress_sample",
    )(pt_flat, *([pool3d] * n_pages), *cw)


def _group_queries(qa, g, lane):
    in_group = (lane < HEAD_DIM) if g == 0 else (lane >= HEAD_DIM)
    rows = []
    for r in range(R_NSA):
        head = R_NSA * g + r
        y = qa[:, (head // 2) * LANES:(head // 2 + 1) * LANES]
        if head % 2 != g:
            y = pltpu.roll(y, HEAD_DIM, 1)
        rows.append(jnp.where(in_group, y, 0.0))
    return jnp.concatenate(rows, axis=0)


def _assemble_heads(res, tq, lane):
    chunks = []
    for c in range(N_HEADS // 2):
        g = c // 2
        a = res[g][(2 * (c % 2)) * tq:(2 * (c % 2) + 1) * tq]
        b = res[g][(2 * (c % 2) + 1) * tq:(2 * (c % 2) + 2) * tq]
        if g == 0:
            b = pltpu.roll(b, HEAD_DIM, 1)
        else:
            a = pltpu.roll(a, HEAD_DIM, 1)
        chunks.append(jnp.where(lane < HEAD_DIM, a, b))
    return jnp.concatenate(chunks, axis=1)


def _block_ranks(impm, n_blocks, lane):
    cnt = jnp.zeros(impm.shape, F32)
    for j in range(n_blocks):
        col = impm[:, j:j + 1]
        ge = jnp.where(col >= impm, 1.0, 0.0)
        gt = jnp.where(col > impm, 1.0, 0.0)
        cnt = cnt + jnp.where(lane > j, ge, gt)
    return cnt


def _compressed_branch(qz, ck, cv, cmask, msel, n_heads, tq):
    q_hi, q_lo = _split2(qz)
    k_hi, k_lo = _split2(ck)
    s = _dot_nt(q_hi, k_hi) + _dot_nt(q_hi, k_lo) + _dot_nt(q_lo, k_hi)
    ps = []
    psum = None
    for r in range(n_heads):
        sr = jnp.where(cmask, s[r * tq:(r + 1) * tq], NEG)
        mx = jnp.max(sr, axis=-1, keepdims=True)
        e = jnp.where(cmask, jnp.exp(sr - mx), 0.0)
        den = jnp.sum(e, axis=-1, keepdims=True)
        p = e / jnp.where(den > 0.0, den, 1.0)
        ps.append(p)
        psum = p if psum is None else psum + p
    o = _dot(jnp.concatenate(ps, axis=0).astype(BF16), cv)
    return o, _dot_exact_lhs(psum, msel)


def _select_blocks(imp, tpos, lane, n_blocks):
    forced = (lane == 0) | (lane == tpos // SEL_BLOCK)
    causal = lane * SEL_BLOCK <= tpos
    impm = jnp.where(forced, FORCED, jnp.where(causal, imp, -1.0))
    impm = jnp.where(lane < n_blocks, impm, -3e38)
    cnt = _block_ranks(impm, n_blocks, lane)
    n_top = min(SEL_TOP, n_blocks)
    return jnp.where(cnt < n_top, jnp.where(lane < n_blocks, 1.0, 0.0), 0.0)


def _nsa_prompt_kernel(qa_ref, small_ref, cmp_ref, slc_ref, win_ref, msel_ref, eblk_ref, gexp_ref,
                       oa_ref, selexp_ref, m_ref, l_ref, acc_ref, *, tq, n_sel):
    tk = tq
    qi = pl.program_id(1)
    s0 = qi * tq
    lane = lax.broadcasted_iota(jnp.int32, (tq, LANES), 1)
    trow = lax.broadcasted_iota(jnp.int32, (tq, LANES), 0)
    tpos = s0 + trow
    row_t = lax.broadcasted_iota(jnp.int32, (tq, tk), 0)
    col_t = lax.broadcasted_iota(jnp.int32, (tq, tk), 1)
    diag_mask = col_t <= row_t
    far_mask = (2 * tk + row_t - col_t) <= WINDOW
    n_cmp = cmp_ref.shape[0]
    cmask = (lax.broadcasted_iota(jnp.int32, (tq, n_cmp), 1) * CMP_STRIDE + CMP_LEN - 1
             <= s0 + lax.broadcasted_iota(jnp.int32, (tq, n_cmp), 0))
    qa = qa_ref[...]
    ck = cmp_ref[:, 0:LANES]
    cv = cmp_ref[:, LANES:2 * LANES].astype(BF16)
    n_kt = selexp_ref.shape[1]

    def masked(s, masks):
        out = []
        for r in range(R_NSA):
            sr = s[r * tq:(r + 1) * tq]
            for mk in masks:
                sr = jnp.where(mk, sr, NEG)
            out.append(sr)
        return jnp.concatenate(out, axis=0)

    def flash_init():
        m_ref[...] = jnp.full(m_ref.shape, NEG, F32)
        l_ref[...] = jnp.zeros(l_ref.shape, F32)
        acc_ref[...] = jnp.zeros(acc_ref.shape, F32)

    def flash_step(qzb, kv_ref, kt, masks):
        off = pl.multiple_of(kt * tk, tk)
        k2 = kv_ref[pl.ds(off, tk), 0:LANES]
        v2 = kv_ref[pl.ds(off, tk), LANES:2 * LANES]
        s = masked(_dot_nt(qzb, k2), masks)
        m_prev = m_ref[...]
        m_new = jnp.maximum(m_prev, jnp.max(s, axis=-1, keepdims=True))
        alpha = jnp.exp(m_prev - m_new)
        p = jnp.exp(s - m_new)
        l_ref[...] = alpha * l_ref[...] + jnp.sum(p, axis=-1, keepdims=True)
        acc_ref[...] = alpha * acc_ref[...] + _dot(p.astype(BF16), v2)
        m_ref[...] = m_new

    def flash_result():
        return acc_ref[...] / l_ref[...]

    o_cmp, o_sel, o_win = [], [], []
    for g in range(G_NSA):
        qz = _group_queries(qa, g, lane)
        qzb = qz.astype(BF16)
        oc, imp = _compressed_branch(qz, ck, cv, cmask, msel_ref[...], R_NSA, tq)
        o_cmp.append(oc)
        sel = _select_blocks(imp, tpos, lane, n_sel).astype(BF16)
        for kt in range(n_kt):
            @pl.when(kt <= qi)
            def _(kt=kt):
                selexp_ref[g, kt] = _dot(sel, eblk_ref[:, kt * tk:(kt + 1) * tk]).astype(BF16)

        flash_init()

        def sel_body(kt, carry, qzb=qzb, g=g):
            flash_step(qzb, slc_ref, kt, [selexp_ref[g, kt] > 0.5])
            return carry

        lax.fori_loop(0, qi, sel_body, 0)
        flash_step(qzb, slc_ref, qi, [selexp_ref[g, qi] > 0.5, diag_mask])
        o_sel.append(flash_result())

        flash_init()

        @pl.when(qi >= 2)
        def _(qzb=qzb):
            flash_step(qzb, win_ref, qi - 2, [far_mask])

        @pl.when(qi >= 1)
        def _(qzb=qzb):
            flash_step(qzb, win_ref, qi - 1, [])

        flash_step(qzb, win_ref, qi, [diag_mask])
        o_win.append(flash_result())

    gates = _dot_exact_lhs(small_ref[...], gexp_ref[...])
    d_a = N_HEADS * HEAD_DIM
    oa_ref[...] = (gates[:, 0:d_a] * _assemble_heads(o_cmp, tq, lane)
                   + gates[:, d_a:2 * d_a] * _assemble_heads(o_sel, tq, lane)
                   + gates[:, 2 * d_a:3 * d_a] * _assemble_heads(o_win, tq, lane))


def _nsa_prompt_call(qa, small, cmpkv, slcb, winb, msel, eblk, gexp, n_batch, seq, tq):
    nq = seq // tq
    n_cmp_rows = cmpkv.shape[0] // n_batch
    n_sel = seq // SEL_BLOCK
    qrow = lambda b, q: (b * nq + q, 0)
    per_b = lambda b, q: (b, 0)
    fixed = lambda b, q: (0, 0)
    return pl.pallas_call(
        functools.partial(_nsa_prompt_kernel, tq=tq, n_sel=n_sel),
        grid=(n_batch, nq),
        in_specs=[pl.BlockSpec((tq, qa.shape[1]), qrow), pl.BlockSpec((tq, LANES), qrow),
                  pl.BlockSpec((n_cmp_rows, 2 * LANES), per_b),
                  pl.BlockSpec((seq, 2 * LANES), per_b), pl.BlockSpec((seq, 2 * LANES), per_b),
                  pl.BlockSpec(msel.shape, fixed), pl.BlockSpec(eblk.shape, fixed), pl.BlockSpec(gexp.shape, fixed)],
        out_specs=pl.BlockSpec((tq, qa.shape[1]), qrow),
        out_shape=jax.ShapeDtypeStruct(qa.shape, F32),
        scratch_shapes=[pltpu.VMEM((G_NSA, nq, tq, tq), BF16),
                        pltpu.VMEM((R_NSA * tq, 1), F32), pltpu.VMEM((R_NSA * tq, 1), F32),
                        pltpu.VMEM((R_NSA * tq, LANES), F32)],
        compiler_params=_params(("arbitrary", "arbitrary")),
        name="nsa_prompt",
    )(qa, small, cmpkv, slcb, winb, msel, eblk, gexp)


def _fox_prompt_kernel(q_ref, k_ref, v_ref, cq_ref, ck_ref, o_ref, m_ref, l_ref, acc_ref, *, tq):
    tk = tq
    qi = pl.program_id(2)
    lane = lax.broadcasted_iota(jnp.int32, (tq, LANES), 1)
    q = q_ref[...]
    qz = jnp.concatenate([jnp.where(lane < HEAD_DIM, q, 0.0), jnp.where(lane >= HEAD_DIM, q, 0.0)], axis=0).astype(BF16)
    cq = cq_ref[...]
    cq_cols = [cq[:, 0:1], cq[:, HEAD_DIM:HEAD_DIM + 1]]
    diag_mask = lax.broadcasted_iota(jnp.int32, (tq, tk), 1) <= lax.broadcasted_iota(jnp.int32, (tq, tk), 0)

    m_ref[...] = jnp.full(m_ref.shape, NEG, F32)
    l_ref[...] = jnp.zeros(l_ref.shape, F32)
    acc_ref[...] = jnp.zeros(acc_ref.shape, F32)

    def step(kt, mask):
        off = pl.multiple_of(kt * tk, tk)
        k2 = k_ref[pl.ds(off, tk), :]
        v2 = v_ref[pl.ds(off, tk), :]
        s = _dot_nt(qz, k2)
        parts = []
        for e in range(2):
            se = s[e * tq:(e + 1) * tq] + (cq_cols[e] - ck_ref[0, e, pl.ds(kt, 1), :])
            if mask is not None:
                se = jnp.where(mask, se, NEG)
            parts.append(se)
        s = jnp.concatenate(parts, axis=0)
        m_prev = m_ref[...]
        m_new = jnp.maximum(m_prev, jnp.max(s, axis=-1, keepdims=True))
        alpha = jnp.exp(m_prev - m_new)
        p = jnp.exp(s - m_new)
        l_ref[...] = alpha * l_ref[...] + jnp.sum(p, axis=-1, keepdims=True)
        acc_ref[...] = alpha * acc_ref[...] + _dot(p.astype(BF16), v2)
        m_ref[...] = m_new

    def body(kt, carry):
        step(kt, None)
        return carry

    lax.fori_loop(0, qi, body, 0)
    step(qi, diag_mask)
    o = acc_ref[...] / l_ref[...]
    o_ref[...] = jnp.where(lane < HEAD_DIM, o[0:tq], o[tq:2 * tq])


def _fox_prompt_call(qb, foxb, cexp, ckt, n_batch, seq, tq):
    nq = seq // tq
    n_pairs = N_HEADS // 2
    qmap = lambda b, hp, q: (b * nq + q, hp)
    return pl.pallas_call(
        functools.partial(_fox_prompt_kernel, tq=tq),
        grid=(n_batch, n_pairs, nq),
        in_specs=[pl.BlockSpec((tq, LANES), qmap),
                  pl.BlockSpec((seq, LANES), lambda b, hp, q: (b, hp)),
                  pl.BlockSpec((seq, LANES), lambda b, hp, q: (b, n_pairs + hp)),
                  pl.BlockSpec((tq, LANES), qmap),
                  pl.BlockSpec((1, 2, nq, tq), lambda b, hp, q: (b, hp, 0, 0))],
        out_specs=pl.BlockSpec((tq, LANES), qmap),
        out_shape=jax.ShapeDtypeStruct(qb.shape, F32),
        scratch_shapes=[pltpu.VMEM((2 * tq, 1), F32), pltpu.VMEM((2 * tq, 1), F32), pltpu.VMEM((2 * tq, LANES), F32)],
        compiler_params=_params(("arbitrary", "arbitrary", "arbitrary")),
        name="fox_prompt",
    )(qb, foxb, foxb, cexp, ckt)


def _pad_rows(x, rows):
    return jnp.concatenate([x, jnp.zeros((rows - x.shape[0], x.shape[1]), x.dtype)], axis=0)


def _nsa_sample_kernel(pt_ref, qa_ref, small_ref, cmp_ref, *refs, n_pages, past, n_sel):
    del pt_ref
    page_refs = refs[:n_pages]
    (slcn_ref, winbuf_ref, winn_ref, msel_ref, eblk_ref, gexp_ref,
     oa_ref, newwin_ref, kbuf, vbuf) = refs[n_pages:]
    s_new = qa_ref.shape[0]
    wb = winbuf_ref.shape[0]
    n_keys = kbuf.shape[0]
    lane = lax.broadcasted_iota(jnp.int32, (s_new, LANES), 1)
    tpos = past + lax.broadcasted_iota(jnp.int32, (s_new, LANES), 0)
    n_cmp = cmp_ref.shape[0]
    cmask = (lax.broadcasted_iota(jnp.int32, (s_new, n_cmp), 1) * CMP_STRIDE + CMP_LEN - 1
             <= past + lax.broadcasted_iota(jnp.int32, (s_new, n_cmp), 0))
    kpos = lax.broadcasted_iota(jnp.int32, (s_new, n_keys), 1)
    key_ok = kpos <= past + lax.broadcasted_iota(jnp.int32, (s_new, n_keys), 0)

    for j in range(n_pages):
        kbuf[j * PAGE_SIZE:(j + 1) * PAGE_SIZE, :] = page_refs[j][:, 0:LANES].astype(BF16)
        vbuf[j * PAGE_SIZE:(j + 1) * PAGE_SIZE, :] = page_refs[j][:, LANES:2 * LANES].astype(BF16)
    slcn = slcn_ref[...]
    kbuf[past:past + PAGE_SIZE, :] = _pad_rows(slcn[:, 0:LANES], PAGE_SIZE).astype(BF16)
    vbuf[past:past + PAGE_SIZE, :] = _pad_rows(slcn[:, LANES:2 * LANES], PAGE_SIZE).astype(BF16)

    winn = winn_ref[...]
    w_rows = wb + PAGE_SIZE
    wk = jnp.concatenate([winbuf_ref[:, 0:LANES], _pad_rows(winn[:, 0:LANES], PAGE_SIZE)], axis=0).astype(BF16)
    wv = jnp.concatenate([winbuf_ref[:, LANES:2 * LANES], _pad_rows(winn[:, LANES:2 * LANES], PAGE_SIZE)], axis=0).astype(BF16)
    wi = lax.broadcasted_iota(jnp.int32, (s_new, w_rows), 1)
    wt = lax.broadcasted_iota(jnp.int32, (s_new, w_rows), 0)
    dist = wb + wt - wi
    win_ok = (dist >= 0) & (dist <= WINDOW) & (past - wb + wi >= 0) & (wi < wb + s_new)
    newwin_ref[0:wb - s_new, :] = winbuf_ref[s_new:wb, :]
    newwin_ref[wb - s_new:wb, :] = winn

    qa = qa_ref[...]
    ck = cmp_ref[:, 0:LANES]
    cv = cmp_ref[:, LANES:2 * LANES].astype(BF16)

    def softmax_pv(s, mask, v):
        out = []
        for r in range(R_NSA):
            out.append(jnp.where(mask, s[r * s_new:(r + 1) * s_new], NEG))
        s = jnp.concatenate(out, axis=0)
        mx = jnp.max(s, axis=-1, keepdims=True)
        p = jnp.exp(s - mx)
        den = jnp.sum(p, axis=-1, keepdims=True)
        return _dot(p.astype(BF16), v) / den

    o_cmp, o_sel, o_win = [], [], []
    for g in range(G_NSA):
        qz = _group_queries(qa, g, lane)
        qzb = qz.astype(BF16)
        oc, imp = _compressed_branch(qz, ck, cv, cmask, msel_ref[...], R_NSA, s_new)
        o_cmp.append(oc)
        sel = _select_blocks(imp, tpos, lane, n_sel).astype(BF16)
        sel_ok = (_dot(sel, eblk_ref[...]) > 0.5) & key_ok
        o_sel.append(softmax_pv(_dot_nt(qzb, kbuf[...]), sel_ok, vbuf[...]))
        o_win.append(softmax_pv(_dot_nt(qzb, wk), win_ok, wv))

    gates = _dot_exact_lhs(small_ref[...], gexp_ref[...])
    d_a = N_HEADS * HEAD_DIM
    oa_ref[...] = (gates[:, 0:d_a] * _assemble_heads(o_cmp, s_new, lane)
                   + gates[:, d_a:2 * d_a] * _assemble_heads(o_sel, s_new, lane)
                   + gates[:, 2 * d_a:3 * d_a] * _assemble_heads(o_win, s_new, lane))


def _nsa_sample_call(pt_flat, qa, small, cmpkv, pool3d, slcn, winbuf, winn, msel, eblk, gexp,
                     n_batch, s_new, pages_per_seq):
    past = pages_per_seq * PAGE_SIZE
    n_sel = -(-(past + s_new) // SEL_BLOCK)
    n_cmp_rows = cmpkv.shape[0] // n_batch
    wb = winbuf.shape[1]
    row = lambda b, pt: (b, 0)
    fixed = lambda b, pt: (0, 0)

    def page_spec(j):
        return pl.BlockSpec((None, PAGE_SIZE, pool3d.shape[2]), lambda b, pt: (pt[b * pages_per_seq + j], 0, 0))

    grid_spec = pltpu.PrefetchScalarGridSpec(
        num_scalar_prefetch=1,
        grid=(n_batch,),
        in_specs=[pl.BlockSpec((s_new, qa.shape[1]), row), pl.BlockSpec((s_new, LANES), row),
                  pl.BlockSpec((n_cmp_rows, 2 * LANES), row)]
                 + [page_spec(j) for j in range(pages_per_seq)]
                 + [pl.BlockSpec((s_new, 2 * LANES), row),
                    pl.BlockSpec((None, wb, 2 * LANES), lambda b, pt: (b, 0, 0)),
                    pl.BlockSpec((s_new, 2 * LANES), row),
                    pl.BlockSpec(msel.shape, fixed), pl.BlockSpec(eblk.shape, fixed), pl.BlockSpec(gexp.shape, fixed)],
        out_specs=[pl.BlockSpec((s_new, qa.shape[1]), row),
                   pl.BlockSpec((None, wb, 2 * LANES), lambda b, pt: (b, 0, 0))],
        scratch_shapes=[pltpu.VMEM((past + PAGE_SIZE, LANES), BF16), pltpu.VMEM((past + PAGE_SIZE, LANES), BF16)],
    )
    return pl.pallas_call(
        functools.partial(_nsa_sample_kernel, n_pages=pages_per_seq, past=past, n_sel=n_sel),
        grid_spec=grid_spec,
        out_shape=[jax.ShapeDtypeStruct(qa.shape, F32), jax.ShapeDtypeStruct(winbuf.shape, F32)],
        compiler_params=_params(("arbitrary",)),
        name="nsa_sample",
    )(pt_flat, qa, small, cmpkv, *([pool3d] * pages_per_seq), slcn, winbuf, winn, msel, eblk, gexp)


def _fox_sample_kernel(pt_ref, q_ref, small_ref, foxn_ref, *refs, n_pages, past):
    del pt_ref
    page_refs = refs[:n_pages]
    logf_refs = refs[n_pages:2 * n_pages]
    psel_ref, triu_ref, eye_ref, o_ref, kbuf, vbuf = refs[2 * n_pages:]
    s_new = q_ref.shape[0]
    n_keys = kbuf.shape[0]
    lane = lax.broadcasted_iota(jnp.int32, (s_new, LANES), 1)
    key_ok = (lax.broadcasted_iota(jnp.int32, (s_new, n_keys), 1)
              <= past + lax.broadcasted_iota(jnp.int32, (s_new, n_keys), 0))

    small_pad = _pad_rows(small_ref[...], LANES)
    a, b, c = _split3(small_pad)
    psel = psel_ref[...]
    new_t = _dot_nt(psel, a) + _dot_nt(psel, b) + _dot_nt(psel, c)
    w0 = jnp.concatenate([r[...] for r in logf_refs] + [new_t], axis=0)
    wc = _dot_exact_lhs(w0, triu_ref[...])
    run = jnp.zeros((N_HEADS, 1), F32)
    c_rows = []
    for j in range(n_pages + 1):
        blk = wc[j * N_HEADS:(j + 1) * N_HEADS]
        c_rows.append(blk + run)
        run = run + blk[:, LANES - 1:LANES]
    a, b, c = _split3(_pad_rows(c_rows[n_pages], LANES))
    eye = eye_ref[...]
    cq_t = _dot_nt(eye, a) + _dot_nt(eye, b) + _dot_nt(eye, c)

    q = q_ref[...]
    foxn = foxn_ref[...]
    d_k = N_HEADS * HEAD_DIM
    for hp in range(N_HEADS // 2):
        ks = slice(hp * LANES, (hp + 1) * LANES)
        vs = slice(d_k + hp * LANES, d_k + (hp + 1) * LANES)
        for j in range(n_pages):
            kbuf[j * PAGE_SIZE:(j + 1) * PAGE_SIZE, :] = page_refs[j][:, ks].astype(BF16)
            vbuf[j * PAGE_SIZE:(j + 1) * PAGE_SIZE, :] = page_refs[j][:, vs].astype(BF16)
        kbuf[past:past + PAGE_SIZE, :] = _pad_rows(foxn[:, ks], PAGE_SIZE).astype(BF16)
        vbuf[past:past + PAGE_SIZE, :] = _pad_rows(foxn[:, vs], PAGE_SIZE).astype(BF16)
        q2 = q[:, ks]
        qz = jnp.concatenate([jnp.where(lane < HEAD_DIM, q2, 0.0), jnp.where(lane >= HEAD_DIM, q2, 0.0)], axis=0)
        s = _dot_nt(qz.astype(BF16), kbuf[...])
        parts = []
        for e in range(2):
            head = 2 * hp + e
            ck_row = jnp.concatenate([cr[head:head + 1, :] for cr in c_rows], axis=1)
            se = s[e * s_new:(e + 1) * s_new] + (cq_t[0:s_new, head:head + 1] - ck_row)
            parts.append(jnp.where(key_ok, se, NEG))
        s = jnp.concatenate(parts, axis=0)
        mx = jnp.max(s, axis=-1, keepdims=True)
        p = jnp.exp(s - mx)
        den = jnp.sum(p, axis=-1, keepdims=True)
        o = _dot(p.astype(BF16), vbuf[...]) / den
        o_ref[:, ks] = jnp.where(lane < HEAD_DIM, o[0:s_new], o[s_new:2 * s_new])


def _fox_sample_call(pt_flat, qb, small, foxn, pool3d, logf_t, psel, triu, eye, n_batch, s_new, pages_per_seq):
    past = pages_per_seq * PAGE_SIZE
    row = lambda b, pt: (b, 0)
    fixed = lambda b, pt: (0, 0)

    def page_spec(arr, j):
        return pl.BlockSpec((None,) + arr.shape[1:], lambda b, pt: (pt[b * pages_per_seq + j], 0, 0))

    grid_spec = pltpu.PrefetchScalarGridSpec(
        num_scalar_prefetch=1,
        grid=(n_batch,),
        in_specs=[pl.BlockSpec((s_new, qb.shape[1]), row), pl.BlockSpec((s_new, LANES), row),
                  pl.BlockSpec((s_new, foxn.shape[1]), row)]
                 + [page_spec(pool3d, j) for j in range(pages_per_seq)]
                 + [page_spec(logf_t, j) for j in range(pages_per_seq)]
                 + [pl.BlockSpec(psel.shape, fixed), pl.BlockSpec(triu.shape, fixed), pl.BlockSpec(eye.shape, fixed)],
        out_specs=pl.BlockSpec((s_new, qb.shape[1]), row),
        scratch_shapes=[pltpu.VMEM((past + PAGE_SIZE, LANES), BF16), pltpu.VMEM((past + PAGE_SIZE, LANES), BF16)],
    )
    return pl.pallas_call(
        functools.partial(_fox_sample_kernel, n_pages=pages_per_seq, past=past),
        grid_spec=grid_spec,
        out_shape=jax.ShapeDtypeStruct(qb.shape, F32),
        compiler_params=_params(("arbitrary",)),
        name="fox_sample",
    )(pt_flat, qb, small, foxn, *([pool3d] * pages_per_seq), *([logf_t] * pages_per_seq), psel, triu, eye)


def _merge_kernel(x_ref, oa_ref, ob_ref, z_ref, w_ref, y_ref):
    d_a = oa_ref.shape[1]
    z = z_ref[...]
    gate = z * _sigmoid(z)
    u = jnp.concatenate([oa_ref[...] * gate[:, 0:d_a], ob_ref[...] * gate[:, d_a:]], axis=1).astype(BF16)
    y_ref[...] = x_ref[...] + _dot(u, w_ref[...])


def _merge_call(x2d, oa, ob, z, w_out_bf, tm, name):
    n, d = x2d.shape
    row = lambda i: (i, 0)
    return pl.pallas_call(
        _merge_kernel,
        grid=(n // tm,),
        in_specs=[pl.BlockSpec((tm, d), row), pl.BlockSpec((tm, oa.shape[1]), row), pl.BlockSpec((tm, ob.shape[1]), row),
                  pl.BlockSpec((tm, z.shape[1]), row), pl.BlockSpec(w_out_bf.shape, lambda i: (0, 0))],
        out_specs=pl.BlockSpec((tm, d), row),
        out_shape=jax.ShapeDtypeStruct((n, d), F32),
        compiler_params=_params(("arbitrary",)),
        name=name,
    )(x2d, oa, ob, z, w_out_bf)


def _row_tile(n, preferred):
    return preferred if n % preferred == 0 else n


def _rope_tables(pos):
    half = HEAD_DIM // 2
    inv = jnp.power(jnp.float32(ROPE_THETA), -jnp.arange(half, dtype=F32) / half)
    ang = pos.astype(F32)[:, None] * inv[None, :]
    cos = jnp.cos(ang)
    sin = jnp.sin(ang)
    return jnp.tile(cos, (1, 4)), jnp.tile(jnp.concatenate([-sin, sin], axis=1), (1, 2))


def _cmp_to_sel(n_cmp, n_sel, rows, cols):
    cs = np.arange(n_cmp) * CMP_STRIDE
    ss = np.arange(n_sel) * SEL_BLOCK
    ov = np.clip(np.minimum(cs[:, None] + CMP_LEN, ss[None, :] + SEL_BLOCK) - np.maximum(cs[:, None], ss[None, :]), 0, None)
    m = np.zeros((rows, cols), np.float32)
    m[:n_cmp, :n_sel] = ov / CMP_LEN
    return jnp.asarray(m, dtype=BF16)


def _block_expand(n_keys):
    e = np.zeros((LANES, n_keys), np.float32)
    k = np.arange(n_keys)
    e[k // SEL_BLOCK, k] = 1.0
    return jnp.asarray(e, dtype=BF16)


def _gate_expand():
    e = np.zeros((LANES, 3 * N_HEADS * HEAD_DIM), np.float32)
    for i in range(3):
        for h in range(N_HEADS):
            e[i * N_HEADS + h, i * N_HEADS * HEAD_DIM + h * HEAD_DIM:i * N_HEADS * HEAD_DIM + (h + 1) * HEAD_DIM] = 1.0
    return jnp.asarray(e, dtype=BF16)


def _head_expand():
    e = np.zeros((LANES, N_HEADS * HEAD_DIM), np.float32)
    for h in range(N_HEADS):
        e[LOGF_LANE + h, h * HEAD_DIM:(h + 1) * HEAD_DIM] = 1.0
    return jnp.asarray(e, dtype=BF16)


def _compress_weights(w1, w2, pe):
    half = CMP_LEN // 2
    eye = jnp.eye(G_NSA, dtype=F32)
    w = w1.reshape(2, half, HEAD_DIM, CMP_HID)
    w1x = jnp.einsum('pldh,gk->lgdpkh', w, eye).reshape(half * G_NSA * HEAD_DIM, 2 * G_NSA * CMP_HID).astype(BF16)
    w2x = jnp.einsum('hd,gk->ghkd', w2, eye).reshape(G_NSA * CMP_HID, G_NSA * HEAD_DIM).astype(BF16)
    pe2 = jnp.broadcast_to(pe.reshape(2, half, 1, HEAD_DIM), (2, half, G_NSA, HEAD_DIM)).reshape(2, -1)
    pex = jnp.concatenate([pe2, jnp.zeros((14, pe2.shape[1]), F32)], axis=0).astype(BF16)
    return w1x, w2x, pex


def kernel(x_prompt, x_sample, cache_nsa_cmp_kv, cache_nsa_slc_kv, cache_nsa_win_kv, cache_fox_kv, cache_fox_logf,
           page_table, g_norm, w_in, b_f, gq_a, gk_cmp, gk_slc, gk_win, pe_cmp_k, pe_cmp_v,
           w_cmp1_k, w_cmp2_k, w_cmp1_v, w_cmp2_v, gq_b, gk_b, w_out):
    depth = g_norm.shape[0]
    assert depth == 1, "single-layer stack"
    n_b, seq, d_model = x_prompt.shape
    n_db, s_new, _ = x_sample.shape
    pages_per_seq = page_table.shape[1]
    past = pages_per_seq * PAGE_SIZE
    wb = cache_nsa_win_kv.shape[2]
    n_pool = cache_nsa_cmp_kv.shape[1]
    d_a = N_HEADS * HEAD_DIM
    kv_w = 2 * G_NSA * HEAD_DIM

    w = w_in[0]
    o_kc, o_ga, o_za, o_qb, o_kb, o_fb, o_zb = 512, 1280, 1304, 1816, 2328, 3352, 3360
    gate_cols = o_ga + (np.arange(N_HEADS)[None, :] * 3 + np.arange(3)[:, None]).reshape(-1)
    small_w = jnp.concatenate([w[:, gate_cols], w[:, o_fb:o_fb + N_HEADS],
                               jnp.zeros((d_model, LANES - GATE_LANES - N_HEADS), F32)], axis=1)
    w_bf = jnp.concatenate([w[:, 0:o_ga], w[:, o_za:o_qb], w[:, o_zb:o_zb + d_a], w[:, o_qb:o_fb], small_w],
                           axis=1).astype(BF16)
    assert w_bf.shape[1] == C_END
    gains = jnp.concatenate([jnp.tile(v[0], 2)[None, :] for v in (gq_a, gk_slc, gk_win, gq_b, gk_b)]
                            + [jnp.zeros((3, LANES), F32)], axis=0)
    bf_row = jnp.zeros((1, LANES), F32).at[0, LOGF_LANE:LOGF_LANE + N_HEADS].set(b_f[0])
    lane_head = np.arange(LANES) // HEAD_DIM
    ones_bd = jnp.asarray(lane_head[:, None] == lane_head[None, :], dtype=BF16)
    hexp = _head_expand()
    gexp = _gate_expand()
    consts = lambda tm: (g_norm, w_bf, gains, bf_row, ones_bd,
                         jnp.asarray(np.tril(np.ones((tm, tm), np.float32)), dtype=BF16), hexp)

    tm_p = 256
    cos_p, sin_p = _rope_tables(jnp.arange(seq, dtype=jnp.int32))
    tiles_per_seq = seq // tm_p
    (cmp_p, slc_p, win_p, fox_p, small_p, z_p, qa_p, qb_p, cexp_p, slcb_p, winb_p, foxb_p) = _proj_call(
        x_prompt.reshape(n_b * seq, d_model), (cos_p, sin_p, lambda i: (i % tiles_per_seq, 0)),
        consts(tm_p), tiles_per_seq, tm_p, "proj_prompt")
    tm_s = _row_tile(n_db * s_new, 256)
    cos_s, sin_s = _rope_tables(past + jnp.arange(s_new, dtype=jnp.int32))
    cos_s, sin_s = jnp.tile(cos_s, (tm_s // s_new, 1)), jnp.tile(sin_s, (tm_s // s_new, 1))
    (cmp_s, slc_s, win_s, fox_s, small_s, z_s, qa_s, qb_s, _, _, _, _) = _proj_call(
        x_sample.reshape(n_db * s_new, d_model), (cos_s, sin_s, lambda i: (0, 0)),
        consts(tm_s), 1, tm_s, "proj_sample")

    w1k, w2k, pek = _compress_weights(w_cmp1_k[0], w_cmp2_k[0], pe_cmp_k[0])
    w1v, w2v, pev = _compress_weights(w_cmp1_v[0], w_cmp2_v[0], pe_cmp_v[0])
    cw = (w1k, w1v, pek, pev, w2k, w2v, jnp.tile(gk_cmp[0], 2)[None, :], ones_bd)
    chunk_w = CMP_STRIDE * kv_w
    ckv_p = _compress_prompt_call(cmp_p.reshape(n_b * seq // CMP_STRIDE, chunk_w), cw, seq // CMP_STRIDE)
    pt_flat = page_table.reshape(-1).astype(jnp.int32)
    cmp_pool = cache_nsa_cmp_kv[0].reshape(n_pool, PAGE_SIZE // CMP_STRIDE, chunk_w)
    ckv_s = _compress_sample_call(pt_flat, cmp_pool, cw, n_db, pages_per_seq, 2)

    n_cmp_p = (seq - CMP_LEN) // CMP_STRIDE + 1
    msel_p = _cmp_to_sel(n_cmp_p, seq // SEL_BLOCK, seq // CMP_STRIDE, LANES)
    oa_p = _nsa_prompt_call(qa_p, small_p, ckv_p, slcb_p, winb_p, msel_p, _block_expand(seq), gexp, n_b, seq, 256)
    n_cmp_s = (past + s_new - CMP_LEN) // CMP_STRIDE + 1
    n_sel_s = -(-(past + s_new) // SEL_BLOCK)
    msel_s = _cmp_to_sel(n_cmp_s, n_sel_s, past // CMP_STRIDE, LANES)
    oa_s, new_win = _nsa_sample_call(
        pt_flat, qa_s, small_s, ckv_s, cache_nsa_slc_kv[0].reshape(n_pool, PAGE_SIZE, kv_w), slc_s,
        cache_nsa_win_kv[0].reshape(n_db, wb, kv_w), win_s, msel_s, _block_expand(past + PAGE_SIZE), gexp,
        n_db, s_new, pages_per_seq)

    tq_f = 512
    c_p = cexp_p.reshape(n_b, seq, N_HEADS, HEAD_DIM)[:, :, :, 0]
    ckt = jnp.transpose(c_p, (0, 2, 1)).reshape(n_b, N_HEADS, seq // tq_f, tq_f)
    ob_p = _fox_prompt_call(qb_p, foxb_p, cexp_p, ckt, n_b, seq, tq_f)
    psel = jnp.asarray(np.arange(LANES)[None, :] == (LOGF_LANE + np.arange(2 * N_HEADS))[:, None], dtype=BF16)
    psel = psel.at[N_HEADS:].set(0)
    triu = jnp.asarray(np.triu(np.ones((LANES, LANES), np.float32)), dtype=BF16)
    eye = jnp.asarray(np.eye(LANES, dtype=np.float32), dtype=BF16)
    ob_s = _fox_sample_call(
        pt_flat, qb_s, small_s, fox_s, cache_fox_kv[0].reshape(n_pool, PAGE_SIZE, 2 * d_a),
        jnp.transpose(cache_fox_logf[0], (0, 2, 1)), psel, triu, eye, n_db, s_new, pages_per_seq)

    w_out_bf = w_out[0].astype(BF16)
    y_p = _merge_call(x_prompt.reshape(n_b * seq, d_model), oa_p, ob_p, z_p, w_out_bf, 512, "merge_prompt")
    y_s = _merge_call(x_sample.reshape(n_db * s_new, d_model), oa_s, ob_s, z_s, w_out_bf,
                      _row_tile(n_db * s_new, 512), "merge_sample")

    kv5 = lambda a, nb, t, heads: a.reshape(1, nb, t, 2, heads, HEAD_DIM)
    wbp = min(WINDOW, seq)
    return (y_p.reshape(n_b, seq, d_model), y_s.reshape(n_db, s_new, d_model),
            kv5(cmp_p, n_b, seq, G_NSA), kv5(cmp_s, n_db, s_new, G_NSA),
            kv5(slc_p, n_b, seq, G_NSA), kv5(slc_s, n_db, s_new, G_NSA),
            kv5(win_p.reshape(n_b, seq, kv_w)[:, seq - wbp:], n_b, wbp, G_NSA), kv5(new_win, n_db, wb, G_NSA),
            kv5(fox_p, n_b, seq, N_HEADS), kv5(fox_s, n_db, s_new, N_HEADS),
            small_p[:, LOGF_LANE:LOGF_LANE + N_HEADS].reshape(1, n_b, seq, N_HEADS),
            small_s[:, LOGF_LANE:LOGF_LANE + N_HEADS].reshape(1, n_db, s_new, N_HEADS))
```

```python
import functools

import numpy as np
import jax
import jax.numpy as jnp
from jax import lax
from jax.experimental import pallas as pl
from jax.experimental.pallas import tpu as pltpu

F32 = jnp.float32
BF16 = jnp.bfloat16

HEAD_DIM = 64
LANES = 128
N_HEADS = 8
G_NSA = 2
R_NSA = N_HEADS // G_NSA
CMP_LEN = 32
CMP_STRIDE = 16
CMP_HID = 2 * HEAD_DIM
SEL_BLOCK = 64
SEL_TOP = 16
WINDOW = 512
PAGE_SIZE = 128
ROPE_THETA = 10000.0
EPS = 1e-6
NEG = -1e30
FORCED = 1e4
QK_SCALE = HEAD_DIM ** -0.5

VMEM_LIMIT = 56 * 1024 * 1024

C_QA, C_CMP, C_SLC, C_WIN, C_Z, C_QB, C_FOX, C_SMALL, C_END = 0, 512, 768, 1024, 1280, 2304, 2816, 3840, 3968
GATE_LANES = 3 * N_HEADS
LOGF_LANE = GATE_LANES


def _dot(a, b):
    return jnp.dot(a, b, preferred_element_type=F32)


def _dot_nt(a, b):
    return lax.dot_general(a, b, (((1,), (1,)), ((), ())), preferred_element_type=F32)


def _split2(x):
    hi = x.astype(BF16)
    lo = (x - hi.astype(F32)).astype(BF16)
    return hi, lo


def _split3(x):
    hi = x.astype(BF16)
    r = x - hi.astype(F32)
    mid = r.astype(BF16)
    lo = (r - mid.astype(F32)).astype(BF16)
    return hi, mid, lo


def _dot_exact_lhs(x, m):
    a, b, c = _split3(x)
    return _dot(a, m) + _dot(b, m) + _dot(c, m)


def _dot_exact_rhs(m, x):
    a, b, c = _split3(x)
    return _dot(m, a) + _dot(m, b) + _dot(m, c)


def _sigmoid(x):
    return 1.0 / (1.0 + jnp.exp(-x))


def _params(sem):
    return pltpu.CompilerParams(dimension_semantics=sem, vmem_limit_bytes=VMEM_LIMIT)


def _proj_kernel(x_ref, g_ref, w_ref, cos_ref, sin_ref, gains_ref, bf_ref, ones_ref, tri_ref, hexp_ref,
                 cmp_ref, slc_ref, win_ref, fox_ref, small_ref, z_ref, qa_ref, qb_ref, cexp_ref, csum_ref,
                 slcb_ref, winb_ref, foxb_ref, carry_ref, *, tm, tiles_per_seq):
    i = pl.program_id(0)
    x = x_ref[...]
    ms = jnp.mean(x * x, axis=-1, keepdims=True)
    h = (x * lax.rsqrt(ms + EPS) * g_ref[...]).astype(BF16)
    cos = cos_ref[...]
    sin = sin_ref[...]
    lane = lax.broadcasted_iota(jnp.int32, (tm, LANES), 1)
    first_half = (lane & (HEAD_DIM // 2)) == 0
    ones = ones_ref[...]

    def proj(c0, c1):
        return _dot(h, w_ref[:, c0:c1])

    def head_norm(y, row):
        hi, lo = _split2(y * y)
        ss = _dot(hi, ones) + _dot(lo, ones)
        return y * lax.rsqrt(ss * (1.0 / HEAD_DIM) + EPS) * gains_ref[row:row + 1, :]

    def rope(y):
        partner = jnp.where(first_half, pltpu.roll(y, LANES - HEAD_DIM // 2, 1), pltpu.roll(y, HEAD_DIM // 2, 1))
        return y * cos + partner * sin

    def chunk(y, c):
        return y[:, c * LANES:(c + 1) * LANES]

    def with_ones(v, e):
        return jnp.where((lane < HEAD_DIM) if e == 0 else (lane >= HEAD_DIM), v, 1.0)

    y = proj(C_QA, C_CMP)
    for c in range(4):
        qa_ref[:, c * LANES:(c + 1) * LANES] = rope(head_norm(chunk(y, c), 0)) * QK_SCALE

    y = proj(C_CMP, C_SLC)
    cmp_ref[:, 0:LANES] = rope(chunk(y, 0))
    cmp_ref[:, LANES:2 * LANES] = chunk(y, 1)

    for (c0, c1, row, o_ref, ob_ref) in ((C_SLC, C_WIN, 1, slc_ref, slcb_ref), (C_WIN, C_Z, 2, win_ref, winb_ref)):
        y = proj(c0, c1)
        k = rope(head_norm(chunk(y, 0), row))
        v = chunk(y, 1)
        o_ref[:, 0:LANES] = k
        o_ref[:, LANES:2 * LANES] = v
        ob_ref[:, 0:LANES] = k.astype(BF16)
        for e in range(2):
            ob_ref[:, (1 + e) * LANES:(2 + e) * LANES] = with_ones(v, e).astype(BF16)

    z_ref[...] = proj(C_Z, C_QB)

    y = proj(C_QB, C_FOX)
    for c in range(4):
        qb_ref[:, c * LANES:(c + 1) * LANES] = head_norm(chunk(y, c), 3) * QK_SCALE

    y = proj(C_FOX, C_SMALL)
    for c in range(8):
        t = chunk(y, c)
        if c < 4:
            t = head_norm(t, 4)
        fox_ref[:, c * LANES:(c + 1) * LANES] = t
        if c < 4:
            foxb_ref[:, c * LANES:(c + 1) * LANES] = t.astype(BF16)
        else:
            for e in range(2):
                foxb_ref[:, (c + 4 * e) * LANES:(c + 4 * e + 1) * LANES] = with_ones(t, e).astype(BF16)

    raw = proj(C_SMALL, C_END)
    zf = raw + bf_ref[...]
    logf = jnp.minimum(zf, 0.0) - jnp.log1p(jnp.exp(-jnp.abs(zf)))
    small = jnp.where(lane < GATE_LANES, _sigmoid(raw), jnp.where(lane < GATE_LANES + N_HEADS, logf, 0.0))
    small_ref[...] = small

    @pl.when(i % tiles_per_seq == 0)
    def _():
        carry_ref[...] = jnp.zeros_like(carry_ref)

    c = carry_ref[0:1, :] + _dot_exact_rhs(tri_ref[...], small)
    carry_ref[0:1, :] = c[tm - 1:tm, :]
    csum_ref[...] = c
    cexp_ref[...] = _dot_exact_lhs(c, hexp_ref[...])


def _proj_call(x2d, tables, consts, tiles_per_seq, tm, name):
    n = x2d.shape[0]
    cos_t, sin_t, table_map = tables
    g_norm, w_bf, gains, bf_row, ones_bd, tri, hexp = consts
    row = lambda i: (i, 0)
    fixed = lambda i: (0, 0)
    full = lambda a: pl.BlockSpec(a.shape, fixed)
    out_widths = (256, 256, 256, 1024, 128, 1024, 512, 512, 512, 128)
    out_shape = [jax.ShapeDtypeStruct((n, w), F32) for w in out_widths]
    bf_widths = (3 * LANES, 3 * LANES, 3 * N_HEADS * HEAD_DIM)
    out_shape += [jax.ShapeDtypeStruct((n, w), BF16) for w in bf_widths]
    out_specs = [pl.BlockSpec((tm, w), row) for w in out_widths + bf_widths]
    return pl.pallas_call(
        functools.partial(_proj_kernel, tm=tm, tiles_per_seq=tiles_per_seq),
        grid=(n // tm,),
        in_specs=[pl.BlockSpec((tm, x2d.shape[1]), row), full(g_norm), full(w_bf),
                  pl.BlockSpec((tm, LANES), table_map), pl.BlockSpec((tm, LANES), table_map),
                  full(gains), full(bf_row), full(ones_bd), full(tri), full(hexp)],
        out_specs=out_specs,
        out_shape=out_shape,
        scratch_shapes=[pltpu.VMEM((8, LANES), F32)],
        compiler_params=_params(("arbitrary",)),
        name=name,
    )(x2d, g_norm, w_bf, cos_t, sin_t, gains, bf_row, ones_bd, tri, hexp)


def _compress_body(y_refs, wk_ref, wv_ref, pek_ref, pev_ref, w2k_ref, w2v_ref, gain_ref, ones_ref, out_ref,
                   rows_per_seq):
    if len(y_refs) == 1:
        y = y_refs[0][...].astype(BF16)
    else:
        y = jnp.concatenate([r[...] for r in y_refs], axis=0).astype(BF16)
    m = y.shape[0]
    chunk_tokens = CMP_STRIDE
    width = 2 * G_NSA * HEAD_DIM
    hid = G_NSA * CMP_HID

    def branch(off, w1_ref, pe_ref, w2_ref):
        ysel = jnp.concatenate([y[:, l * width + off:l * width + off + LANES] for l in range(chunk_tokens)], axis=1)
        r = _dot(jnp.concatenate([ysel, pe_ref[...]], axis=0), w1_ref[...])
        hh = r[:m, :hid] + pltpu.roll(r[:m, hid:], m - 1, 0) + r[m:m + 1, :hid] + r[m + 1:m + 2, hid:]
        act = hh * _sigmoid(hh)
        return _dot(act.astype(BF16), w2_ref[...])

    kc = branch(0, wk_ref, pek_ref, w2k_ref)
    hi, lo = _split2(kc * kc)
    ss = _dot(hi, ones_ref[...]) + _dot(lo, ones_ref[...])
    kc = kc * lax.rsqrt(ss * (1.0 / HEAD_DIM) + EPS) * gain_ref[...]
    vc = branch(LANES, wv_ref, pev_ref, w2v_ref)
    rows = lax.broadcasted_iota(jnp.int32, (m, LANES), 0)
    valid = (rows % rows_per_seq) != rows_per_seq - 1
    out_ref[:, 0:LANES] = jnp.where(valid, kc, 0.0)
    out_ref[:, LANES:2 * LANES] = jnp.where(valid, vc, 0.0)


def _compress_prompt_kernel(y_ref, *rest, rows_per_seq):
    _compress_body([y_ref], *rest, rows_per_seq=rows_per_seq)


def _compress_sample_kernel(pt_ref, *refs, n_pages, rows_per_seq):
    del pt_ref
    _compress_body(list(refs[:n_pages]), *refs[n_pages:], rows_per_seq=rows_per_seq)


def _compress_prompt_call(y2d, cw, rows_per_seq):
    n = y2d.shape[0]
    fixed = lambda i: (0, 0)
    return pl.pallas_call(
        functools.partial(_compress_prompt_kernel, rows_per_seq=rows_per_seq),
        grid=(n // rows_per_seq,),
        in_specs=[pl.BlockSpec((rows_per_seq, y2d.shape[1]), lambda i: (i, 0))] + [pl.BlockSpec(a.shape, fixed) for a in cw],
        out_specs=pl.BlockSpec((rows_per_seq, 2 * LANES), lambda i: (i, 0)),
        out_shape=jax.ShapeDtypeStruct((n, 2 * LANES), F32),
        compiler_params=_params(("arbitrary",)),
        name="compress_prompt",
    )(y2d, *cw)


def _compress_sample_call(pt_flat, pool3d, cw, n_batch, pages_per_seq, batch_per_step):
    rows_per_page = pool3d.shape[1]
    rows_per_seq = pages_per_seq * rows_per_page
    n_pages = batch_per_step * pages_per_seq
    fixed = lambda i, pt: (0, 0)

    def page_spec(k):
        bb, j = divmod(k, pages_per_seq)
        return pl.BlockSpec((None, rows_per_page, pool3d.shape[2]),
                            lambda i, pt: (pt[(i * batch_per_step + bb) * pages_per_seq + j], 0, 0))

    m = batch_per_step * rows_per_seq
    grid_spec = pltpu.PrefetchScalarGridSpec(
        num_scalar_prefetch=1,
        grid=(n_batch // batch_per_step,),
        in_specs=[page_spec(k) for k in range(n_pages)] + [pl.BlockSpec(a.shape, fixed) for a in cw],
        out_specs=pl.BlockSpec((m, 2 * LANES), lambda i, pt: (i, 0)),
    )
    return pl.pallas_call(
        functools.partial(_compress_sample_kernel, n_pages=n_pages, rows_per_seq=rows_per_seq),
        grid_spec=grid_spec,
        out_shape=jax.ShapeDtypeStruct((n_batch * rows_per_seq, 2 * LANES), F32),
        compiler_params=_params(("arbitrary",)),
        name="compress_sample",
    )(pt_flat, *([pool3d] * n_pages), *cw)


def _group_queries(qa, g, lane):
    in_group = (lane < HEAD_DIM) if g == 0 else (lane >= HEAD_DIM)
    rows = []
    for r in range(R_NSA):
        head = R_NSA * g + r
        y = qa[:, (head // 2) * LANES:(head // 2 + 1) * LANES]
        if head % 2 != g:
            y = pltpu.roll(y, HEAD_DIM, 1)
        rows.append(jnp.where(in_group, y, 0.0))
    return jnp.concatenate(rows, axis=0)


def _assemble_heads(res, tq, lane):
    chunks = []
    for c in range(N_HEADS // 2):
        g = c // 2
        a = res[g][(2 * (c % 2)) * tq:(2 * (c % 2) + 1) * tq]
        b = res[g][(2 * (c % 2) + 1) * tq:(2 * (c % 2) + 2) * tq]
        if g == 0:
            b = pltpu.roll(b, HEAD_DIM, 1)
        else:
            a = pltpu.roll(a, HEAD_DIM, 1)
        chunks.append(jnp.where(lane < HEAD_DIM, a, b))
    return jnp.concatenate(chunks, axis=1)


def _flash_step(s, n_rb, rb, fix, values, m_ref, acc_ref):
    n_c = s.shape[1] // LANES
    cols = [jnp.concatenate([fix(r, c, s[r * rb:(r + 1) * rb, c * LANES:(c + 1) * LANES]) for r in range(n_rb)], axis=0)
            for c in range(n_c)]
    mx = cols[0]
    for c in range(1, n_c):
        mx = jnp.maximum(mx, cols[c])
    m_prev = m_ref[...]
    m_new = jnp.maximum(m_prev, jnp.max(mx, axis=-1, keepdims=True))
    alpha = jnp.exp(m_prev - m_new)
    p = jnp.concatenate([jnp.exp(x - m_new) for x in cols], axis=1).astype(BF16)
    if len(values) == 1:
        pv = _dot(p, values[0])
    else:
        pv = jnp.concatenate([_dot(p[r * rb:(r + 1) * rb], values[r]) for r in range(n_rb)], axis=0)
    acc_ref[...] = alpha * acc_ref[...] + pv
    m_ref[...] = m_new


def _flash_init(m_ref, acc_ref):
    m_ref[...] = jnp.full(m_ref.shape, NEG, F32)
    acc_ref[...] = jnp.zeros(acc_ref.shape, F32)


def _flash_result(acc_ref):
    acc = acc_ref[...]
    return acc / pltpu.roll(acc, HEAD_DIM, 1)


def _block_ranks(impm, n_blocks, lane):
    cnt = jnp.zeros(impm.shape, F32)
    for j in range(n_blocks):
        col = impm[:, j:j + 1]
        ge = jnp.where(col >= impm, 1.0, 0.0)
        gt = jnp.where(col > impm, 1.0, 0.0)
        cnt = cnt + jnp.where(lane > j, ge, gt)
    return cnt


def _compressed_branch(qz, ck, cv, cmask, msel, n_heads, tq):
    q_hi, q_lo = _split2(qz)
    k_hi, k_lo = _split2(ck)
    s = _dot_nt(q_hi, k_hi) + _dot_nt(q_hi, k_lo) + _dot_nt(q_lo, k_hi)
    ps = []
    psum = None
    for r in range(n_heads):
        sr = jnp.where(cmask, s[r * tq:(r + 1) * tq], NEG)
        mx = jnp.max(sr, axis=-1, keepdims=True)
        e = jnp.where(cmask, jnp.exp(sr - mx), 0.0)
        den = jnp.sum(e, axis=-1, keepdims=True)
        p = e / jnp.where(den > 0.0, den, 1.0)
        ps.append(p)
        psum = p if psum is None else psum + p
    o = _dot(jnp.concatenate(ps, axis=0).astype(BF16), cv)
    return o, _dot_exact_lhs(psum, msel)


def _select_blocks(imp, tpos, lane, n_blocks):
    forced = (lane == 0) | (lane == tpos // SEL_BLOCK)
    causal = lane * SEL_BLOCK <= tpos
    impm = jnp.where(forced, FORCED, jnp.where(causal, imp, -1.0))
    impm = jnp.where(lane < n_blocks, impm, -3e38)
    cnt = _block_ranks(impm, n_blocks, lane)
    n_top = min(SEL_TOP, n_blocks)
    return jnp.where(cnt < n_top, jnp.where(lane < n_blocks, 1.0, 0.0), 0.0)


def _select_blocks_t(imp, s0, n_blocks, cnt_ref):
    tq = imp.shape[0]
    grp = 8
    imp_t = jnp.transpose(imp)[0:n_blocks]
    blk = lax.broadcasted_iota(jnp.int32, (n_blocks, tq), 0)
    tpos = s0 + lax.broadcasted_iota(jnp.int32, (n_blocks, tq), 1)
    forced = (blk == 0) | (blk == tpos // SEL_BLOCK)
    impm = jnp.where(forced, FORCED, jnp.where(blk * SEL_BLOCK <= tpos, imp_t, -1.0))
    cnt_ref[...] = jnp.zeros(cnt_ref.shape, F32)
    sub = lax.broadcasted_iota(jnp.int32, (grp, tq), 0)
    for jb in range(n_blocks // grp):
        @pl.when(jb * grp * SEL_BLOCK <= s0 + tq - 1)
        def _(jb=jb):
            for rg in range(n_blocks // grp):
                x = impm[rg * grp:(rg + 1) * grp]
                cnt = cnt_ref[rg * grp:(rg + 1) * grp, :]
                for j in range(jb * grp, (jb + 1) * grp):
                    row = impm[j:j + 1, :]
                    if rg > jb:
                        beat = row >= x
                    elif rg < jb:
                        beat = row > x
                    else:
                        cnt = cnt + jnp.where(sub > j - jb * grp, jnp.where(row >= x, 1.0, 0.0),
                                              jnp.where(row > x, 1.0, 0.0))
                        continue
                    cnt = cnt + jnp.where(beat, 1.0, 0.0)
                cnt_ref[rg * grp:(rg + 1) * grp, :] = cnt
    sel_t = jnp.where(cnt_ref[...] < min(SEL_TOP, n_blocks), 1.0, 0.0)
    return jnp.transpose(jnp.concatenate([sel_t, jnp.zeros((LANES - n_blocks, tq), F32)], axis=0))


def _nsa_prompt_kernel(qa_ref, small_ref, cmp_ref, slc_ref, win_ref, msel_ref, eblk_ref, gexp_ref,
                       oa_ref, m_ref, acc_ref, cnt_ref, *, tq, n_sel):
    tk = tq
    qi = pl.program_id(1)
    s0 = qi * tq
    lane = lax.broadcasted_iota(jnp.int32, (tq, LANES), 1)
    trow = lax.broadcasted_iota(jnp.int32, (tq, LANES), 0)
    tpos = s0 + trow
    row_t = lax.broadcasted_iota(jnp.int32, (tq, tk), 0)
    col_t = lax.broadcasted_iota(jnp.int32, (tq, tk), 1)
    diag_mask = col_t <= row_t
    far_mask = (2 * tk + row_t - col_t) <= WINDOW
    n_cmp = cmp_ref.shape[0]
    cmask = (lax.broadcasted_iota(jnp.int32, (tq, n_cmp), 1) * CMP_STRIDE + CMP_LEN - 1
             <= s0 + lax.broadcasted_iota(jnp.int32, (tq, n_cmp), 0))
    qa = qa_ref[...]
    ck = cmp_ref[:, 0:LANES]
    cv = cmp_ref[:, LANES:2 * LANES].astype(BF16)

    def chunk_of(mask, c):
        return mask[:, c * LANES:(c + 1) * LANES]

    def flash_step(qzb, kv_ref, kt, g, bias, mask):
        off = pl.multiple_of(kt * tk, tk)
        k2 = kv_ref[pl.ds(off, tk), 0:LANES]
        v2 = kv_ref[pl.ds(off, tk), (1 + g) * LANES:(2 + g) * LANES]

        def fix(r, c, x):
            if bias is not None:
                x = x + chunk_of(bias, c)
            if mask is not None:
                x = jnp.where(chunk_of(mask, c), x, NEG)
            return x

        _flash_step(_dot_nt(qzb, k2), R_NSA, tq, fix, [v2], m_ref, acc_ref)

    o_cmp, o_sel, o_win = [], [], []
    for g in range(G_NSA):
        qz = _group_queries(qa, g, lane)
        qzb = qz.astype(BF16)
        oc, imp = _compressed_branch(qz, ck, cv, cmask, msel_ref[...], R_NSA, tq)
        o_cmp.append(oc)
        sel = _select_blocks_t(imp, s0, n_sel, cnt_ref).astype(BF16)

        def sel_bias(kt, sel=sel):
            return (_dot(sel, eblk_ref[kt]) - 1.0) * (-NEG)

        _flash_init(m_ref, acc_ref)

        def sel_body(kt, carry, qzb=qzb, g=g, sel_bias=sel_bias):
            flash_step(qzb, slc_ref, kt, g, sel_bias(kt), None)
            return carry

        lax.fori_loop(0, qi, sel_body, 0)
        flash_step(qzb, slc_ref, qi, g, sel_bias(qi), diag_mask)
        o_sel.append(_flash_result(acc_ref))

        _flash_init(m_ref, acc_ref)

        @pl.when(qi >= 2)
        def _(qzb=qzb, g=g):
            flash_step(qzb, win_ref, qi - 2, g, None, far_mask)

        @pl.when(qi >= 1)
        def _(qzb=qzb, g=g):
            flash_step(qzb, win_ref, qi - 1, g, None, None)

        flash_step(qzb, win_ref, qi, g, None, diag_mask)
        o_win.append(_flash_result(acc_ref))

    gates = _dot_exact_lhs(small_ref[...], gexp_ref[...])
    d_a = N_HEADS * HEAD_DIM
    oa_ref[...] = (gates[:, 0:d_a] * _assemble_heads(o_cmp, tq, lane)
                   + gates[:, d_a:2 * d_a] * _assemble_heads(o_sel, tq, lane)
                   + gates[:, 2 * d_a:3 * d_a] * _assemble_heads(o_win, tq, lane))


def _nsa_prompt_call(qa, small, cmpkv, slcb, winb, msel, eblk, gexp, n_batch, seq, tq):
    nq = seq // tq
    n_cmp_rows = cmpkv.shape[0] // n_batch
    n_sel = seq // SEL_BLOCK
    qrow = lambda b, q: (b * nq + q, 0)
    per_b = lambda b, q: (b, 0)
    fixed = lambda b, q: (0, 0)
    return pl.pallas_call(
        functools.partial(_nsa_prompt_kernel, tq=tq, n_sel=n_sel),
        grid=(n_batch, nq),
        in_specs=[pl.BlockSpec((tq, qa.shape[1]), qrow), pl.BlockSpec((tq, LANES), qrow),
                  pl.BlockSpec((n_cmp_rows, 2 * LANES), per_b),
                  pl.BlockSpec((seq, 3 * LANES), per_b), pl.BlockSpec((seq, 3 * LANES), per_b),
                  pl.BlockSpec(msel.shape, fixed), pl.BlockSpec(eblk.shape, lambda b, q: (0, 0, 0)),
                  pl.BlockSpec(gexp.shape, fixed)],
        out_specs=pl.BlockSpec((tq, qa.shape[1]), qrow),
        out_shape=jax.ShapeDtypeStruct(qa.shape, F32),
        scratch_shapes=[pltpu.VMEM((R_NSA * tq, LANES), F32), pltpu.VMEM((R_NSA * tq, LANES), F32),
                        pltpu.VMEM((n_sel, tq), F32)],
        compiler_params=_params(("arbitrary", "arbitrary")),
        name="nsa_prompt",
    )(qa, small, cmpkv, slcb, winb, msel, eblk, gexp)


def _fox_prompt_kernel(q_ref, k_ref, v0_ref, v1_ref, cq_ref, ck_ref, o_ref, m_ref, acc_ref, *, tq):
    tk = tq
    qi = pl.program_id(2)
    lane = lax.broadcasted_iota(jnp.int32, (tq, LANES), 1)
    q = q_ref[...]
    qz = jnp.concatenate([jnp.where(lane < HEAD_DIM, q, 0.0), jnp.where(lane >= HEAD_DIM, q, 0.0)], axis=0).astype(BF16)
    cq = cq_ref[...]
    cq_sw = pltpu.roll(cq, HEAD_DIM, 1)
    cq_rep = [jnp.where(lane < HEAD_DIM, cq, cq_sw), jnp.where(lane >= HEAD_DIM, cq, cq_sw)]
    diag_mask = lax.broadcasted_iota(jnp.int32, (tq, tk), 1) <= lax.broadcasted_iota(jnp.int32, (tq, tk), 0)
    v_refs = (v0_ref, v1_ref)
    _flash_init(m_ref, acc_ref)

    def step(kt, mask):
        off = pl.multiple_of(kt * tk, tk)
        k2 = k_ref[pl.ds(off, tk), :]
        ck = [ck_ref[0, e, pl.ds(kt, 1), :] for e in range(2)]

        def fix(e, c, x):
            x = x + (cq_rep[e] - ck[e][:, c * LANES:(c + 1) * LANES])
            if mask is not None:
                x = jnp.where(mask[:, c * LANES:(c + 1) * LANES], x, NEG)
            return x

        _flash_step(_dot_nt(qz, k2), 2, tq, fix, [r[pl.ds(off, tk), :] for r in v_refs], m_ref, acc_ref)

    def body(kt, carry):
        step(kt, None)
        return carry

    lax.fori_loop(0, qi, body, 0)
    step(qi, diag_mask)
    o = _flash_result(acc_ref)
    o_ref[...] = jnp.where(lane < HEAD_DIM, o[0:tq], o[tq:2 * tq])


def _fox_prompt_call(qb, foxb, cexp, ckt, n_batch, seq, tq):
    nq = seq // tq
    n_pairs = N_HEADS // 2
    qmap = lambda b, hp, q: (b * nq + q, hp)
    return pl.pallas_call(
        functools.partial(_fox_prompt_kernel, tq=tq),
        grid=(n_batch, n_pairs, nq),
        in_specs=[pl.BlockSpec((tq, LANES), qmap),
                  pl.BlockSpec((seq, LANES), lambda b, hp, q: (b, hp)),
                  pl.BlockSpec((seq, LANES), lambda b, hp, q: (b, n_pairs + hp)),
                  pl.BlockSpec((seq, LANES), lambda b, hp, q: (b, 2 * n_pairs + hp)),
                  pl.BlockSpec((tq, LANES), qmap),
                  pl.BlockSpec((1, 2, nq, tq), lambda b, hp, q: (b, hp, 0, 0))],
        out_specs=pl.BlockSpec((tq, LANES), qmap),
        out_shape=jax.ShapeDtypeStruct(qb.shape, F32),
        scratch_shapes=[pltpu.VMEM((2 * tq, LANES), F32), pltpu.VMEM((2 * tq, LANES), F32)],
        compiler_params=_params(("arbitrary", "arbitrary", "arbitrary")),
        name="fox_prompt",
    )(qb, foxb, foxb, foxb, cexp, ckt)


def _pad_rows(x, rows):
    return jnp.concatenate([x, jnp.zeros((rows - x.shape[0], x.shape[1]), x.dtype)], axis=0)


def _nsa_sample_kernel(pt_ref, qa_ref, small_ref, cmp_ref, *refs, n_pages, past, n_sel):
    del pt_ref
    page_refs = refs[:n_pages]
    (slcn_ref, winbuf_ref, winn_ref, msel_ref, eblk_ref, gexp_ref,
     oa_ref, newwin_ref, kbuf, vbuf) = refs[n_pages:]
    s_new = qa_ref.shape[0]
    wb = winbuf_ref.shape[0]
    n_keys = kbuf.shape[0]
    lane = lax.broadcasted_iota(jnp.int32, (s_new, LANES), 1)
    tpos = past + lax.broadcasted_iota(jnp.int32, (s_new, LANES), 0)
    n_cmp = cmp_ref.shape[0]
    cmask = (lax.broadcasted_iota(jnp.int32, (s_new, n_cmp), 1) * CMP_STRIDE + CMP_LEN - 1
             <= past + lax.broadcasted_iota(jnp.int32, (s_new, n_cmp), 0))
    kpos = lax.broadcasted_iota(jnp.int32, (s_new, n_keys), 1)
    key_ok = kpos <= past + lax.broadcasted_iota(jnp.int32, (s_new, n_keys), 0)

    for j in range(n_pages):
        kbuf[j * PAGE_SIZE:(j + 1) * PAGE_SIZE, :] = page_refs[j][:, 0:LANES].astype(BF16)
        vbuf[j * PAGE_SIZE:(j + 1) * PAGE_SIZE, :] = page_refs[j][:, LANES:2 * LANES].astype(BF16)
    slcn = slcn_ref[...]
    kbuf[past:past + PAGE_SIZE, :] = _pad_rows(slcn[:, 0:LANES], PAGE_SIZE).astype(BF16)
    vbuf[past:past + PAGE_SIZE, :] = _pad_rows(slcn[:, LANES:2 * LANES], PAGE_SIZE).astype(BF16)

    winn = winn_ref[...]
    w_rows = wb + PAGE_SIZE
    wk = jnp.concatenate([winbuf_ref[:, 0:LANES], _pad_rows(winn[:, 0:LANES], PAGE_SIZE)], axis=0).astype(BF16)
    wv = jnp.concatenate([winbuf_ref[:, LANES:2 * LANES], _pad_rows(winn[:, LANES:2 * LANES], PAGE_SIZE)], axis=0).astype(BF16)
    wi = lax.broadcasted_iota(jnp.int32, (s_new, w_rows), 1)
    wt = lax.broadcasted_iota(jnp.int32, (s_new, w_rows), 0)
    dist = wb + wt - wi
    win_ok = (dist >= 0) & (dist <= WINDOW) & (past - wb + wi >= 0) & (wi < wb + s_new)
    newwin_ref[0:wb - s_new, :] = winbuf_ref[s_new:wb, :]
    newwin_ref[wb - s_new:wb, :] = winn

    qa = qa_ref[...]
    ck = cmp_ref[:, 0:LANES]
    cv = cmp_ref[:, LANES:2 * LANES].astype(BF16)

    def softmax_pv(s, mask, v):
        out = []
        for r in range(R_NSA):
            out.append(jnp.where(mask, s[r * s_new:(r + 1) * s_new], NEG))
        s = jnp.concatenate(out, axis=0)
        mx = jnp.max(s, axis=-1, keepdims=True)
        p = jnp.exp(s - mx)
        den = jnp.sum(p, axis=-1, keepdims=True)
        return _dot(p.astype(BF16), v) / den

    o_cmp, o_sel, o_win = [], [], []
    for g in range(G_NSA):
        qz = _group_queries(qa, g, lane)
        qzb = qz.astype(BF16)
        oc, imp = _compressed_branch(qz, ck, cv, cmask, msel_ref[...], R_NSA, s_new)
        o_cmp.append(oc)
        sel = _select_blocks(imp, tpos, lane, n_sel).astype(BF16)
        sel_ok = (_dot(sel, eblk_ref[...]) > 0.5) & key_ok
        o_sel.append(softmax_pv(_dot_nt(qzb, kbuf[...]), sel_ok, vbuf[...]))
        o_win.append(softmax_pv(_dot_nt(qzb, wk), win_ok, wv))

    gates = _dot_exact_lhs(small_ref[...], gexp_ref[...])
    d_a = N_HEADS * HEAD_DIM
    oa_ref[...] = (gates[:, 0:d_a] * _assemble_heads(o_cmp, s_new, lane)
                   + gates[:, d_a:2 * d_a] * _assemble_heads(o_sel, s_new, lane)
                   + gates[:, 2 * d_a:3 * d_a] * _assemble_heads(o_win, s_new, lane))


def _nsa_sample_call(pt_flat, qa, small, cmpkv, pool3d, slcn, winbuf, winn, msel, eblk, gexp,
                     n_batch, s_new, pages_per_seq):
    past = pages_per_seq * PAGE_SIZE
    n_sel = -(-(past + s_new) // SEL_BLOCK)
    n_cmp_rows = cmpkv.shape[0] // n_batch
    wb = winbuf.shape[1]
    row = lambda b, pt: (b, 0)
    fixed = lambda b, pt: (0, 0)

    def page_spec(j):
        return pl.BlockSpec((None, PAGE_SIZE, pool3d.shape[2]), lambda b, pt: (pt[b * pages_per_seq + j], 0, 0))

    grid_spec = pltpu.PrefetchScalarGridSpec(
        num_scalar_prefetch=1,
        grid=(n_batch,),
        in_specs=[pl.BlockSpec((s_new, qa.shape[1]), row), pl.BlockSpec((s_new, LANES), row),
                  pl.BlockSpec((n_cmp_rows, 2 * LANES), row)]
                 + [page_spec(j) for j in range(pages_per_seq)]
                 + [pl.BlockSpec((s_new, 2 * LANES), row),
                    pl.BlockSpec((None, wb, 2 * LANES), lambda b, pt: (b, 0, 0)),
                    pl.BlockSpec((s_new, 2 * LANES), row),
                    pl.BlockSpec(msel.shape, fixed), pl.BlockSpec(eblk.shape, fixed), pl.BlockSpec(gexp.shape, fixed)],
        out_specs=[pl.BlockSpec((s_new, qa.shape[1]), row),
                   pl.BlockSpec((None, wb, 2 * LANES), lambda b, pt: (b, 0, 0))],
        scratch_shapes=[pltpu.VMEM((past + PAGE_SIZE, LANES), BF16), pltpu.VMEM((past + PAGE_SIZE, LANES), BF16)],
    )
    return pl.pallas_call(
        functools.partial(_nsa_sample_kernel, n_pages=pages_per_seq, past=past, n_sel=n_sel),
        grid_spec=grid_spec,
        out_shape=[jax.ShapeDtypeStruct(qa.shape, F32), jax.ShapeDtypeStruct(winbuf.shape, F32)],
        compiler_params=_params(("arbitrary",)),
        name="nsa_sample",
    )(pt_flat, qa, small, cmpkv, *([pool3d] * pages_per_seq), slcn, winbuf, winn, msel, eblk, gexp)


def _fox_sample_kernel(pt_ref, q_ref, small_ref, foxn_ref, *refs, n_pages, past):
    del pt_ref
    page_refs = refs[:n_pages]
    logf_refs = refs[n_pages:2 * n_pages]
    psel_ref, triu_ref, eye_ref, o_ref, kbuf, vbuf = refs[2 * n_pages:]
    s_new = q_ref.shape[0]
    n_keys = kbuf.shape[0]
    lane = lax.broadcasted_iota(jnp.int32, (s_new, LANES), 1)
    key_ok = (lax.broadcasted_iota(jnp.int32, (s_new, n_keys), 1)
              <= past + lax.broadcasted_iota(jnp.int32, (s_new, n_keys), 0))

    small_pad = _pad_rows(small_ref[...], LANES)
    a, b, c = _split3(small_pad)
    psel = psel_ref[...]
    new_t = _dot_nt(psel, a) + _dot_nt(psel, b) + _dot_nt(psel, c)
    w0 = jnp.concatenate([r[...] for r in logf_refs] + [new_t], axis=0)
    wc = _dot_exact_lhs(w0, triu_ref[...])
    run = jnp.zeros((N_HEADS, 1), F32)
    c_rows = []
    for j in range(n_pages + 1):
        blk = wc[j * N_HEADS:(j + 1) * N_HEADS]
        c_rows.append(blk + run)
        run = run + blk[:, LANES - 1:LANES]
    a, b, c = _split3(_pad_rows(c_rows[n_pages], LANES))
    eye = eye_ref[...]
    cq_t = _dot_nt(eye, a) + _dot_nt(eye, b) + _dot_nt(eye, c)

    q = q_ref[...]
    foxn = foxn_ref[...]
    d_k = N_HEADS * HEAD_DIM
    for hp in range(N_HEADS // 2):
        ks = slice(hp * LANES, (hp + 1) * LANES)
        vs = slice(d_k + hp * LANES, d_k + (hp + 1) * LANES)
        for j in range(n_pages):
            kbuf[j * PAGE_SIZE:(j + 1) * PAGE_SIZE, :] = page_refs[j][:, ks].astype(BF16)
            vbuf[j * PAGE_SIZE:(j + 1) * PAGE_SIZE, :] = page_refs[j][:, vs].astype(BF16)
        kbuf[past:past + PAGE_SIZE, :] = _pad_rows(foxn[:, ks], PAGE_SIZE).astype(BF16)
        vbuf[past:past + PAGE_SIZE, :] = _pad_rows(foxn[:, vs], PAGE_SIZE).astype(BF16)
        q2 = q[:, ks]
        qz = jnp.concatenate([jnp.where(lane < HEAD_DIM, q2, 0.0), jnp.where(lane >= HEAD_DIM, q2, 0.0)], axis=0)
        s = _dot_nt(qz.astype(BF16), kbuf[...])
        parts = []
        for e in range(2):
            head = 2 * hp + e
            ck_row = jnp.concatenate([cr[head:head + 1, :] for cr in c_rows], axis=1)
            se = s[e * s_new:(e + 1) * s_new] + (cq_t[0:s_new, head:head + 1] - ck_row)
            parts.append(jnp.where(key_ok, se, NEG))
        s = jnp.concatenate(parts, axis=0)
        mx = jnp.max(s, axis=-1, keepdims=True)
        p = jnp.exp(s - mx)
        den = jnp.sum(p, axis=-1, keepdims=True)
        o = _dot(p.astype(BF16), vbuf[...]) / den
        o_ref[:, ks] = jnp.where(lane < HEAD_DIM, o[0:s_new], o[s_new:2 * s_new])


def _fox_sample_call(pt_flat, qb, small, foxn, pool3d, logf_t, psel, triu, eye, n_batch, s_new, pages_per_seq):
    past = pages_per_seq * PAGE_SIZE
    row = lambda b, pt: (b, 0)
    fixed = lambda b, pt: (0, 0)

    def page_spec(arr, j):
        return pl.BlockSpec((None,) + arr.shape[1:], lambda b, pt: (pt[b * pages_per_seq + j], 0, 0))

    grid_spec = pltpu.PrefetchScalarGridSpec(
        num_scalar_prefetch=1,
        grid=(n_batch,),
        in_specs=[pl.BlockSpec((s_new, qb.shape[1]), row), pl.BlockSpec((s_new, LANES), row),
                  pl.BlockSpec((s_new, foxn.shape[1]), row)]
                 + [page_spec(pool3d, j) for j in range(pages_per_seq)]
                 + [page_spec(logf_t, j) for j in range(pages_per_seq)]
                 + [pl.BlockSpec(psel.shape, fixed), pl.BlockSpec(triu.shape, fixed), pl.BlockSpec(eye.shape, fixed)],
        out_specs=pl.BlockSpec((s_new, qb.shape[1]), row),
        scratch_shapes=[pltpu.VMEM((past + PAGE_SIZE, LANES), BF16), pltpu.VMEM((past + PAGE_SIZE, LANES), BF16)],
    )
    return pl.pallas_call(
        functools.partial(_fox_sample_kernel, n_pages=pages_per_seq, past=past),
        grid_spec=grid_spec,
        out_shape=jax.ShapeDtypeStruct(qb.shape, F32),
        compiler_params=_params(("arbitrary",)),
        name="fox_sample",
    )(pt_flat, qb, small, foxn, *([pool3d] * pages_per_seq), *([logf_t] * pages_per_seq), psel, triu, eye)


def _merge_kernel(x_ref, oa_ref, ob_ref, z_ref, w_ref, y_ref):
    d_a = oa_ref.shape[1]
    z = z_ref[...]
    gate = z * _sigmoid(z)
    u = jnp.concatenate([oa_ref[...] * gate[:, 0:d_a], ob_ref[...] * gate[:, d_a:]], axis=1).astype(BF16)
    y_ref[...] = x_ref[...] + _dot(u, w_ref[...])


def _merge_call(x2d, oa, ob, z, w_out_bf, tm, name):
    n, d = x2d.shape
    row = lambda i: (i, 0)
    return pl.pallas_call(
        _merge_kernel,
        grid=(n // tm,),
        in_specs=[pl.BlockSpec((tm, d), row), pl.BlockSpec((tm, oa.shape[1]), row), pl.BlockSpec((tm, ob.shape[1]), row),
                  pl.BlockSpec((tm, z.shape[1]), row), pl.BlockSpec(w_out_bf.shape, lambda i: (0, 0))],
        out_specs=pl.BlockSpec((tm, d), row),
        out_shape=jax.ShapeDtypeStruct((n, d), F32),
        compiler_params=_params(("arbitrary",)),
        name=name,
    )(x2d, oa, ob, z, w_out_bf)


def _row_tile(n, preferred):
    return preferred if n % preferred == 0 else n


def _rope_tables(pos):
    half = HEAD_DIM // 2
    inv = jnp.power(jnp.float32(ROPE_THETA), -jnp.arange(half, dtype=F32) / half)
    ang = pos.astype(F32)[:, None] * inv[None, :]
    cos = jnp.cos(ang)
    sin = jnp.sin(ang)
    return jnp.tile(cos, (1, 4)), jnp.tile(jnp.concatenate([-sin, sin], axis=1), (1, 2))


def _cmp_to_sel(n_cmp, n_sel, rows, cols):
    cs = np.arange(n_cmp) * CMP_STRIDE
    ss = np.arange(n_sel) * SEL_BLOCK
    ov = np.clip(np.minimum(cs[:, None] + CMP_LEN, ss[None, :] + SEL_BLOCK) - np.maximum(cs[:, None], ss[None, :]), 0, None)
    m = np.zeros((rows, cols), np.float32)
    m[:n_cmp, :n_sel] = ov / CMP_LEN
    return jnp.asarray(m, dtype=BF16)


def _block_expand(n_keys):
    e = np.zeros((LANES, n_keys), np.float32)
    k = np.arange(n_keys)
    e[k // SEL_BLOCK, k] = 1.0
    return jnp.asarray(e, dtype=BF16)


def _gate_expand():
    e = np.zeros((LANES, 3 * N_HEADS * HEAD_DIM), np.float32)
    for i in range(3):
        for h in range(N_HEADS):
            e[i * N_HEADS + h, i * N_HEADS * HEAD_DIM + h * HEAD_DIM:i * N_HEADS * HEAD_DIM + (h + 1) * HEAD_DIM] = 1.0
    return jnp.asarray(e, dtype=BF16)


def _head_expand():
    e = np.zeros((LANES, N_HEADS * HEAD_DIM), np.float32)
    for h in range(N_HEADS):
        e[LOGF_LANE + h, h * HEAD_DIM:(h + 1) * HEAD_DIM] = 1.0
    return jnp.asarray(e, dtype=BF16)


def _compress_weights(w1, w2, pe):
    half = CMP_LEN // 2
    eye = jnp.eye(G_NSA, dtype=F32)
    w = w1.reshape(2, half, HEAD_DIM, CMP_HID)
    w1x = jnp.einsum('pldh,gk->lgdpkh', w, eye).reshape(half * G_NSA * HEAD_DIM, 2 * G_NSA * CMP_HID).astype(BF16)
    w2x = jnp.einsum('hd,gk->ghkd', w2, eye).reshape(G_NSA * CMP_HID, G_NSA * HEAD_DIM).astype(BF16)
    pe2 = jnp.broadcast_to(pe.reshape(2, half, 1, HEAD_DIM), (2, half, G_NSA, HEAD_DIM)).reshape(2, -1)
    pex = jnp.concatenate([pe2, jnp.zeros((14, pe2.shape[1]), F32)], axis=0).astype(BF16)
    return w1x, w2x, pex


def kernel(x_prompt, x_sample, cache_nsa_cmp_kv, cache_nsa_slc_kv, cache_nsa_win_kv, cache_fox_kv, cache_fox_logf,
           page_table, g_norm, w_in, b_f, gq_a, gk_cmp, gk_slc, gk_win, pe_cmp_k, pe_cmp_v,
           w_cmp1_k, w_cmp2_k, w_cmp1_v, w_cmp2_v, gq_b, gk_b, w_out):
    depth = g_norm.shape[0]
    assert depth == 1, "single-layer stack"
    n_b, seq, d_model = x_prompt.shape
    n_db, s_new, _ = x_sample.shape
    pages_per_seq = page_table.shape[1]
    past = pages_per_seq * PAGE_SIZE
    wb = cache_nsa_win_kv.shape[2]
    n_pool = cache_nsa_cmp_kv.shape[1]
    d_a = N_HEADS * HEAD_DIM
    kv_w = 2 * G_NSA * HEAD_DIM

    w = w_in[0]
    o_kc, o_ga, o_za, o_qb, o_kb, o_fb, o_zb = 512, 1280, 1304, 1816, 2328, 3352, 3360
    gate_cols = o_ga + (np.arange(N_HEADS)[None, :] * 3 + np.arange(3)[:, None]).reshape(-1)
    small_w = jnp.concatenate([w[:, gate_cols], w[:, o_fb:o_fb + N_HEADS],
                               jnp.zeros((d_model, LANES - GATE_LANES - N_HEADS), F32)], axis=1)
    w_bf = jnp.concatenate([w[:, 0:o_ga], w[:, o_za:o_qb], w[:, o_zb:o_zb + d_a], w[:, o_qb:o_fb], small_w],
                           axis=1).astype(BF16)
    assert w_bf.shape[1] == C_END
    gains = jnp.concatenate([jnp.tile(v[0], 2)[None, :] for v in (gq_a, gk_slc, gk_win, gq_b, gk_b)]
                            + [jnp.zeros((3, LANES), F32)], axis=0)
    bf_row = jnp.zeros((1, LANES), F32).at[0, LOGF_LANE:LOGF_LANE + N_HEADS].set(b_f[0])
    lane_head = np.arange(LANES) // HEAD_DIM
    ones_bd = jnp.asarray(lane_head[:, None] == lane_head[None, :], dtype=BF16)
    hexp = _head_expand()
    gexp = _gate_expand()
    consts = lambda tm: (g_norm, w_bf, gains, bf_row, ones_bd,
                         jnp.asarray(np.tril(np.ones((tm, tm), np.float32)), dtype=BF16), hexp)

    tm_p = 256
    cos_p, sin_p = _rope_tables(jnp.arange(seq, dtype=jnp.int32))
    tiles_per_seq = seq // tm_p
    (cmp_p, slc_p, win_p, fox_p, small_p, z_p, qa_p, qb_p, cexp_p, csum_p, slcb_p, winb_p, foxb_p) = _proj_call(
        x_prompt.reshape(n_b * seq, d_model), (cos_p, sin_p, lambda i: (i % tiles_per_seq, 0)),
        consts(tm_p), tiles_per_seq, tm_p, "proj_prompt")
    tm_s = _row_tile(n_db * s_new, 256)
    cos_s, sin_s = _rope_tables(past + jnp.arange(s_new, dtype=jnp.int32))
    cos_s, sin_s = jnp.tile(cos_s, (tm_s // s_new, 1)), jnp.tile(sin_s, (tm_s // s_new, 1))
    (cmp_s, slc_s, win_s, fox_s, small_s, z_s, qa_s, qb_s, _, _, _, _, _) = _proj_call(
        x_sample.reshape(n_db * s_new, d_model), (cos_s, sin_s, lambda i: (0, 0)),
        consts(tm_s), 1, tm_s, "proj_sample")

    w1k, w2k, pek = _compress_weights(w_cmp1_k[0], w_cmp2_k[0], pe_cmp_k[0])
    w1v, w2v, pev = _compress_weights(w_cmp1_v[0], w_cmp2_v[0], pe_cmp_v[0])
    cw = (w1k, w1v, pek, pev, w2k, w2v, jnp.tile(gk_cmp[0], 2)[None, :], ones_bd)
    chunk_w = CMP_STRIDE * kv_w
    ckv_p = _compress_prompt_call(cmp_p.reshape(n_b * seq // CMP_STRIDE, chunk_w), cw, seq // CMP_STRIDE)
    pt_flat = page_table.reshape(-1).astype(jnp.int32)
    cmp_pool = cache_nsa_cmp_kv[0].reshape(n_pool, PAGE_SIZE // CMP_STRIDE, chunk_w)
    ckv_s = _compress_sample_call(pt_flat, cmp_pool, cw, n_db, pages_per_seq, 2)

    n_cmp_p = (seq - CMP_LEN) // CMP_STRIDE + 1
    msel_p = _cmp_to_sel(n_cmp_p, seq // SEL_BLOCK, seq // CMP_STRIDE, LANES)
    tq_a = 256
    eblk_p = jnp.transpose(_block_expand(seq).reshape(LANES, seq // tq_a, tq_a), (1, 0, 2))
    oa_p = _nsa_prompt_call(qa_p, small_p, ckv_p, slcb_p, winb_p, msel_p, eblk_p, gexp, n_b, seq, tq_a)
    n_cmp_s = (past + s_new - CMP_LEN) // CMP_STRIDE + 1
    n_sel_s = -(-(past + s_new) // SEL_BLOCK)
    msel_s = _cmp_to_sel(n_cmp_s, n_sel_s, past // CMP_STRIDE, LANES)
    oa_s, new_win = _nsa_sample_call(
        pt_flat, qa_s, small_s, ckv_s, cache_nsa_slc_kv[0].reshape(n_pool, PAGE_SIZE, kv_w), slc_s,
        cache_nsa_win_kv[0].reshape(n_db, wb, kv_w), win_s, msel_s, _block_expand(past + PAGE_SIZE), gexp,
        n_db, s_new, pages_per_seq)

    tq_f = 512
    c_p = csum_p[:, LOGF_LANE:LOGF_LANE + N_HEADS].reshape(n_b, seq, N_HEADS)
    ckt = jnp.transpose(c_p, (0, 2, 1)).reshape(n_b, N_HEADS, seq // tq_f, tq_f)
    ob_p = _fox_prompt_call(qb_p, foxb_p, cexp_p, ckt, n_b, seq, tq_f)
    psel = jnp.asarray(np.arange(LANES)[None, :] == (LOGF_LANE + np.arange(2 * N_HEADS))[:, None], dtype=BF16)
    psel = psel.at[N_HEADS:].set(0)
    triu = jnp.asarray(np.triu(np.ones((LANES, LANES), np.float32)), dtype=BF16)
    eye = jnp.asarray(np.eye(LANES, dtype=np.float32), dtype=BF16)
    ob_s = _fox_sample_call(
        pt_flat, qb_s, small_s, fox_s, cache_fox_kv[0].reshape(n_pool, PAGE_SIZE, 2 * d_a),
        jnp.transpose(cache_fox_logf[0], (0, 2, 1)), psel, triu, eye, n_db, s_new, pages_per_seq)

    w_out_bf = w_out[0].astype(BF16)
    y_p = _merge_call(x_prompt.reshape(n_b * seq, d_model), oa_p, ob_p, z_p, w_out_bf, 512, "merge_prompt")
    y_s = _merge_call(x_sample.reshape(n_db * s_new, d_model), oa_s, ob_s, z_s, w_out_bf,
                      _row_tile(n_db * s_new, 512), "merge_sample")

    kv5 = lambda a, nb, t, heads: a.reshape(1, nb, t, 2, heads, HEAD_DIM)
    wbp = min(WINDOW, seq)
    return (y_p.reshape(n_b, seq, d_model), y_s.reshape(n_db, s_new, d_model),
            kv5(cmp_p, n_b, seq, G_NSA), kv5(cmp_s, n_db, s_new, G_NSA),
            kv5(slc_p, n_b, seq, G_NSA), kv5(slc_s, n_db, s_new, G_NSA),
            kv5(win_p.reshape(n_b, seq, kv_w)[:, seq - wbp:], n_b, wbp, G_NSA), kv5(new_win, n_db, wb, G_NSA),
            kv5(fox_p, n_b, seq, N_HEADS), kv5(fox_s, n_db, s_new, N_HEADS),
            small_p[:, LOGF_LANE:LOGF_LANE + N_HEADS].reshape(1, n_b, seq, N_HEADS),
            small_s[:, LOGF_LANE:LOGF_LANE + N_HEADS].reshape(1, n_db, s_new, N_HEADS))
```

```python
import functools

import numpy as np
import jax
import jax.numpy as jnp
from jax import lax
from jax.experimental import pallas as pl
from jax.experimental.pallas import tpu as pltpu

F32 = jnp.float32
BF16 = jnp.bfloat16

HEAD_DIM = 64
LANES = 128
N_HEADS = 8
G_NSA = 2
R_NSA = N_HEADS // G_NSA
CMP_LEN = 32
CMP_STRIDE = 16
CMP_HID = 2 * HEAD_DIM
SEL_BLOCK = 64
SEL_TOP = 16
WINDOW = 512
PAGE_SIZE = 128
ROPE_THETA = 10000.0
EPS = 1e-6
NEG = -1e30
FORCED = 1e4
QK_SCALE = HEAD_DIM ** -0.5

VMEM_LIMIT = 56 * 1024 * 1024

C_QA, C_CMP, C_SLC, C_WIN, C_Z, C_QB, C_FOX, C_SMALL, C_END = 0, 512, 768, 1024, 1280, 2304, 2816, 3840, 3968
GATE_LANES = 3 * N_HEADS
LOGF_LANE = GATE_LANES


def _dot(a, b):
    return jnp.dot(a, b, preferred_element_type=F32)


def _dot_nt(a, b):
    return lax.dot_general(a, b, (((1,), (1,)), ((), ())), preferred_element_type=F32)


def _split2(x):
    hi = x.astype(BF16)
    lo = (x - hi.astype(F32)).astype(BF16)
    return hi, lo


def _split3(x):
    hi = x.astype(BF16)
    r = x - hi.astype(F32)
    mid = r.astype(BF16)
    lo = (r - mid.astype(F32)).astype(BF16)
    return hi, mid, lo


def _dot_exact_lhs(x, m):
    a, b, c = _split3(x)
    return _dot(a, m) + _dot(b, m) + _dot(c, m)


def _dot_exact_rhs(m, x):
    a, b, c = _split3(x)
    return _dot(m, a) + _dot(m, b) + _dot(m, c)


def _sigmoid(x):
    return 1.0 / (1.0 + jnp.exp(-x))


def _params(sem):
    return pltpu.CompilerParams(dimension_semantics=sem, vmem_limit_bytes=VMEM_LIMIT)


def _proj_kernel(x_ref, g_ref, w_ref, cos_ref, sin_ref, gains_ref, bf_ref, ones_ref, tri_ref, hexp_ref,
                 cmp_ref, slc_ref, win_ref, fox_ref, small_ref, z_ref, qa_ref, qb_ref, cexp_ref, csum_ref, cmpd_ref,
                 slcb_ref, winb_ref, foxb_ref, carry_ref, *, tm, tiles_per_seq):
    i = pl.program_id(0)
    x = x_ref[...]
    ms = jnp.mean(x * x, axis=-1, keepdims=True)
    h = (x * lax.rsqrt(ms + EPS) * g_ref[...]).astype(BF16)
    cos = cos_ref[...]
    sin = sin_ref[...]
    lane = lax.broadcasted_iota(jnp.int32, (tm, LANES), 1)
    first_half = (lane & (HEAD_DIM // 2)) == 0
    ones = ones_ref[...]

    def proj(c0, c1):
        return _dot(h, w_ref[:, c0:c1])

    def head_norm(y, row):
        hi, lo = _split2(y * y)
        ss = _dot(hi, ones) + _dot(lo, ones)
        return y * lax.rsqrt(ss * (1.0 / HEAD_DIM) + EPS) * gains_ref[row:row + 1, :]

    def rope(y):
        partner = jnp.where(first_half, pltpu.roll(y, LANES - HEAD_DIM // 2, 1), pltpu.roll(y, HEAD_DIM // 2, 1))
        return y * cos + partner * sin

    def chunk(y, c):
        return y[:, c * LANES:(c + 1) * LANES]

    def with_ones(v, e):
        return jnp.where((lane < HEAD_DIM) if e == 0 else (lane >= HEAD_DIM), v, 1.0)

    y = proj(C_QA, C_CMP)
    for c in range(4):
        qa_ref[:, c * LANES:(c + 1) * LANES] = rope(head_norm(chunk(y, c), 0)) * QK_SCALE

    def store_heads(o_ref, t, slot):
        n_slots = o_ref.shape[0] // tm
        o_ref[pl.ds(slot, tm, stride=n_slots), :] = t[:, 0:HEAD_DIM]
        o_ref[pl.ds(slot + 1, tm, stride=n_slots), :] = t[:, HEAD_DIM:LANES]

    y = proj(C_CMP, C_SLC)
    k = rope(chunk(y, 0))
    v = chunk(y, 1)
    cmpd_ref[:, 0:LANES] = k
    cmpd_ref[:, LANES:2 * LANES] = v
    store_heads(cmp_ref, k, 0)
    store_heads(cmp_ref, v, 2)

    for (c0, c1, row, o_ref, ob_ref) in ((C_SLC, C_WIN, 1, slc_ref, slcb_ref), (C_WIN, C_Z, 2, win_ref, winb_ref)):
        y = proj(c0, c1)
        k = rope(head_norm(chunk(y, 0), row))
        v = chunk(y, 1)
        store_heads(o_ref, k, 0)
        store_heads(o_ref, v, 2)
        ob_ref[:, 0:LANES] = k.astype(BF16)
        for e in range(2):
            ob_ref[:, (1 + e) * LANES:(2 + e) * LANES] = with_ones(v, e).astype(BF16)

    z_ref[...] = proj(C_Z, C_QB)

    y = proj(C_QB, C_FOX)
    for c in range(4):
        qb_ref[:, c * LANES:(c + 1) * LANES] = head_norm(chunk(y, c), 3) * QK_SCALE

    y = proj(C_FOX, C_SMALL)
    for c in range(8):
        t = chunk(y, c)
        if c < 4:
            t = head_norm(t, 4)
        store_heads(fox_ref, t, 2 * c)
        if c < 4:
            foxb_ref[:, c * LANES:(c + 1) * LANES] = t.astype(BF16)
        else:
            for e in range(2):
                foxb_ref[:, (c + 4 * e) * LANES:(c + 4 * e + 1) * LANES] = with_ones(t, e).astype(BF16)

    raw = proj(C_SMALL, C_END)
    zf = raw + bf_ref[...]
    logf = jnp.minimum(zf, 0.0) - jnp.log1p(jnp.exp(-jnp.abs(zf)))
    small = jnp.where(lane < GATE_LANES, _sigmoid(raw), jnp.where(lane < GATE_LANES + N_HEADS, logf, 0.0))
    small_ref[...] = small

    @pl.when(i % tiles_per_seq == 0)
    def _():
        carry_ref[...] = jnp.zeros_like(carry_ref)

    c = carry_ref[0:1, :] + _dot_exact_rhs(tri_ref[...], small)
    carry_ref[0:1, :] = c[tm - 1:tm, :]
    csum_ref[...] = c
    cexp_ref[...] = _dot_exact_lhs(c, hexp_ref[...])


def _proj_call(x2d, tables, consts, tiles_per_seq, tm, name):
    n = x2d.shape[0]
    cos_t, sin_t, table_map = tables
    g_norm, w_bf, gains, bf_row, ones_bd, tri, hexp = consts
    row = lambda i: (i, 0)
    fixed = lambda i: (0, 0)
    full = lambda a: pl.BlockSpec(a.shape, fixed)
    head_rows = (2 * G_NSA, 2 * G_NSA, 2 * G_NSA, 2 * N_HEADS)
    out_widths = (128, 1024, 512, 512, 512, 128, 256)
    bf_widths = (3 * LANES, 3 * LANES, 3 * N_HEADS * HEAD_DIM)
    out_shape = [jax.ShapeDtypeStruct((n * r, HEAD_DIM), F32) for r in head_rows]
    out_shape += [jax.ShapeDtypeStruct((n, w), F32) for w in out_widths]
    out_shape += [jax.ShapeDtypeStruct((n, w), BF16) for w in bf_widths]
    out_specs = [pl.BlockSpec((tm * r, HEAD_DIM), row) for r in head_rows]
    out_specs += [pl.BlockSpec((tm, w), row) for w in out_widths + bf_widths]
    return pl.pallas_call(
        functools.partial(_proj_kernel, tm=tm, tiles_per_seq=tiles_per_seq),
        grid=(n // tm,),
        in_specs=[pl.BlockSpec((tm, x2d.shape[1]), row), full(g_norm), full(w_bf),
                  pl.BlockSpec((tm, LANES), table_map), pl.BlockSpec((tm, LANES), table_map),
                  full(gains), full(bf_row), full(ones_bd), full(tri), full(hexp)],
        out_specs=out_specs,
        out_shape=out_shape,
        scratch_shapes=[pltpu.VMEM((8, LANES), F32)],
        compiler_params=_params(("arbitrary",)),
        name=name,
    )(x2d, g_norm, w_bf, cos_t, sin_t, gains, bf_row, ones_bd, tri, hexp)


def _compress_body(y_refs, wk_ref, wv_ref, pek_ref, pev_ref, w2k_ref, w2v_ref, gain_ref, ones_ref, out_ref,
                   rows_per_seq):
    if len(y_refs) == 1:
        y = y_refs[0][...].astype(BF16)
    else:
        y = jnp.concatenate([r[...] for r in y_refs], axis=0).astype(BF16)
    m = y.shape[0]
    chunk_tokens = CMP_STRIDE
    width = 2 * G_NSA * HEAD_DIM
    hid = G_NSA * CMP_HID

    def branch(off, w1_ref, pe_ref, w2_ref):
        ysel = jnp.concatenate([y[:, l * width + off:l * width + off + LANES] for l in range(chunk_tokens)], axis=1)
        r = _dot(jnp.concatenate([ysel, pe_ref[...]], axis=0), w1_ref[...])
        hh = r[:m, :hid] + pltpu.roll(r[:m, hid:], m - 1, 0) + r[m:m + 1, :hid] + r[m + 1:m + 2, hid:]
        act = hh * _sigmoid(hh)
        return _dot(act.astype(BF16), w2_ref[...])

    kc = branch(0, wk_ref, pek_ref, w2k_ref)
    hi, lo = _split2(kc * kc)
    ss = _dot(hi, ones_ref[...]) + _dot(lo, ones_ref[...])
    kc = kc * lax.rsqrt(ss * (1.0 / HEAD_DIM) + EPS) * gain_ref[...]
    vc = branch(LANES, wv_ref, pev_ref, w2v_ref)
    rows = lax.broadcasted_iota(jnp.int32, (m, LANES), 0)
    valid = (rows % rows_per_seq) != rows_per_seq - 1
    out_ref[:, 0:LANES] = jnp.where(valid, kc, 0.0)
    out_ref[:, LANES:2 * LANES] = jnp.where(valid, vc, 0.0)


def _compress_prompt_kernel(y_ref, *rest, rows_per_seq):
    _compress_body([y_ref], *rest, rows_per_seq=rows_per_seq)


def _compress_sample_kernel(pt_ref, *refs, n_pages, rows_per_seq):
    del pt_ref
    page_refs = refs[:n_pages]
    w1_ref, pe_ref, w2_ref, gain_ref, out_ref, st1_ref, st2_ref = refs[n_pages:]
    tok_rows = 2 * G_NSA
    fan = 4
    assert tok_rows == fan and CMP_STRIDE == fan * fan
    chunks = PAGE_SIZE // CMP_STRIDE
    m = n_pages * chunks
    rows = lax.broadcasted_iota(jnp.int32, (G_NSA * m, HEAD_DIM), 0)
    valid = (rows % rows_per_seq) != rows_per_seq - 1
    for s in range(tok_rows):
        for p, r in enumerate(page_refs):
            st1_ref[s, p * PAGE_SIZE:(p + 1) * PAGE_SIZE, :] = r[pl.ds(s, PAGE_SIZE, stride=fan), :]
        for b in range(fan):
            st2_ref[s, b] = st1_ref[s, pl.ds(b, n_pages * PAGE_SIZE // fan, stride=fan), :]
    for kind in range(2):
        acc = None
        for l in range(CMP_STRIDE):
            pieces = [st2_ref[kind * G_NSA + g, l % fan, pl.ds(l // fan, m, stride=fan), :] for g in range(G_NSA)]
            x = jnp.concatenate([jnp.concatenate(pieces, axis=0).astype(BF16), pe_ref[kind, l]], axis=0)
            r = _dot(x, w1_ref[kind, l])
            acc = r if acc is None else acc + r
        gm = G_NSA * m
        hh = (acc[:gm, :CMP_HID] + pltpu.roll(acc[:gm, CMP_HID:], gm - 1, 0)
              + acc[gm:gm + 1, :CMP_HID] + acc[gm + 1:gm + 2, CMP_HID:])
        out = _dot((hh * _sigmoid(hh)).astype(BF16), w2_ref[kind])
        if kind == 0:
            out = out * lax.rsqrt(jnp.mean(out * out, axis=-1, keepdims=True) + EPS) * gain_ref[...]
        out = jnp.where(valid, out, 0.0)
        for g in range(G_NSA):
            out_ref[kind * G_NSA + g] = out[g * m:(g + 1) * m]


def _compress_prompt_call(y2d, cw, rows_per_seq):
    n = y2d.shape[0]
    fixed = lambda i: (0, 0)
    return pl.pallas_call(
        functools.partial(_compress_prompt_kernel, rows_per_seq=rows_per_seq),
        grid=(n // rows_per_seq,),
        in_specs=[pl.BlockSpec((rows_per_seq, y2d.shape[1]), lambda i: (i, 0))] + [pl.BlockSpec(a.shape, fixed) for a in cw],
        out_specs=pl.BlockSpec((rows_per_seq, 2 * LANES), lambda i: (i, 0)),
        out_shape=jax.ShapeDtypeStruct((n, 2 * LANES), F32),
        compiler_params=_params(("arbitrary",)),
        name="compress_prompt",
    )(y2d, *cw)


def _compress_sample_call(pt_flat, pool_rows, cw, n_batch, pages_per_seq, batch_per_step):
    page_rows = PAGE_SIZE * 2 * G_NSA
    rows_per_seq = pages_per_seq * (PAGE_SIZE // CMP_STRIDE)
    n_pages = batch_per_step * pages_per_seq

    def page_spec(k):
        bb, j = divmod(k, pages_per_seq)
        return pl.BlockSpec((page_rows, HEAD_DIM),
                            lambda i, pt: (pt[(i * batch_per_step + bb) * pages_per_seq + j], 0))

    m = batch_per_step * rows_per_seq
    grid_spec = pltpu.PrefetchScalarGridSpec(
        num_scalar_prefetch=1,
        grid=(n_batch // batch_per_step,),
        in_specs=[page_spec(k) for k in range(n_pages)]
                 + [pl.BlockSpec(a.shape, lambda i, pt, nd=a.ndim: (0,) * nd) for a in cw],
        out_specs=pl.BlockSpec((2 * G_NSA, m, HEAD_DIM), lambda i, pt: (0, i, 0)),
        scratch_shapes=[pltpu.VMEM((2 * G_NSA, n_pages * PAGE_SIZE, HEAD_DIM), F32),
                        pltpu.VMEM((2 * G_NSA, 4, n_pages * PAGE_SIZE // 4, HEAD_DIM), F32)],
    )
    return pl.pallas_call(
        functools.partial(_compress_sample_kernel, n_pages=n_pages, rows_per_seq=rows_per_seq),
        grid_spec=grid_spec,
        out_shape=jax.ShapeDtypeStruct((2 * G_NSA, n_batch * rows_per_seq, HEAD_DIM), F32),
        compiler_params=_params(("arbitrary",)),
        name="compress_sample",
    )(pt_flat, *([pool_rows] * n_pages), *cw)


def _group_queries(qa, g, lane):
    in_group = (lane < HEAD_DIM) if g == 0 else (lane >= HEAD_DIM)
    rows = []
    for r in range(R_NSA):
        head = R_NSA * g + r
        y = qa[:, (head // 2) * LANES:(head // 2 + 1) * LANES]
        if head % 2 != g:
            y = pltpu.roll(y, HEAD_DIM, 1)
        rows.append(jnp.where(in_group, y, 0.0))
    return jnp.concatenate(rows, axis=0)


def _assemble_heads(res, tq, lane):
    chunks = []
    for c in range(N_HEADS // 2):
        g = c // 2
        a = res[g][(2 * (c % 2)) * tq:(2 * (c % 2) + 1) * tq]
        b = res[g][(2 * (c % 2) + 1) * tq:(2 * (c % 2) + 2) * tq]
        if g == 0:
            b = pltpu.roll(b, HEAD_DIM, 1)
        else:
            a = pltpu.roll(a, HEAD_DIM, 1)
        chunks.append(jnp.where(lane < HEAD_DIM, a, b))
    return jnp.concatenate(chunks, axis=1)


def _flash_step(s, n_rb, rb, fix, values, m_ref, acc_ref):
    n_c = s.shape[1] // LANES
    cols = [jnp.concatenate([fix(r, c, s[r * rb:(r + 1) * rb, c * LANES:(c + 1) * LANES]) for r in range(n_rb)], axis=0)
            for c in range(n_c)]
    mx = cols[0]
    for c in range(1, n_c):
        mx = jnp.maximum(mx, cols[c])
    m_prev = m_ref[...]
    m_new = jnp.maximum(m_prev, jnp.max(mx, axis=-1, keepdims=True))
    alpha = jnp.exp(m_prev - m_new)
    p = jnp.concatenate([jnp.exp(x - m_new) for x in cols], axis=1).astype(BF16)
    if len(values) == 1:
        pv = _dot(p, values[0])
    else:
        pv = jnp.concatenate([_dot(p[r * rb:(r + 1) * rb], values[r]) for r in range(n_rb)], axis=0)
    acc_ref[...] = alpha * acc_ref[...] + pv
    m_ref[...] = m_new


def _flash_init(m_ref, acc_ref):
    m_ref[...] = jnp.full(m_ref.shape, NEG, F32)
    acc_ref[...] = jnp.zeros(acc_ref.shape, F32)


def _flash_result(acc_ref):
    acc = acc_ref[...]
    return acc / pltpu.roll(acc, HEAD_DIM, 1)


def _block_ranks(impm, n_blocks, lane):
    cnt = jnp.zeros(impm.shape, F32)
    for j in range(n_blocks):
        col = impm[:, j:j + 1]
        ge = jnp.where(col >= impm, 1.0, 0.0)
        gt = jnp.where(col > impm, 1.0, 0.0)
        cnt = cnt + jnp.where(lane > j, ge, gt)
    return cnt


def _compressed_branch(qz, ck, cv, cmask, msel, n_heads, tq):
    q_hi, q_lo = _split2(qz)
    k_hi, k_lo = _split2(ck)
    s = _dot_nt(q_hi, k_hi) + _dot_nt(q_hi, k_lo) + _dot_nt(q_lo, k_hi)
    ps = []
    psum = None
    for r in range(n_heads):
        sr = jnp.where(cmask, s[r * tq:(r + 1) * tq], NEG)
        mx = jnp.max(sr, axis=-1, keepdims=True)
        e = jnp.where(cmask, jnp.exp(sr - mx), 0.0)
        den = jnp.sum(e, axis=-1, keepdims=True)
        p = e / jnp.where(den > 0.0, den, 1.0)
        ps.append(p)
        psum = p if psum is None else psum + p
    o = _dot(jnp.concatenate(ps, axis=0).astype(BF16), cv)
    return o, _dot_exact_lhs(psum, msel)


def _select_blocks(imp, tpos, lane, n_blocks):
    forced = (lane == 0) | (lane == tpos // SEL_BLOCK)
    causal = lane * SEL_BLOCK <= tpos
    impm = jnp.where(forced, FORCED, jnp.where(causal, imp, -1.0))
    impm = jnp.where(lane < n_blocks, impm, -3e38)
    cnt = _block_ranks(impm, n_blocks, lane)
    n_top = min(SEL_TOP, n_blocks)
    return jnp.where(cnt < n_top, jnp.where(lane < n_blocks, 1.0, 0.0), 0.0)


def _select_blocks_t(imp, s0, n_blocks, cnt_ref):
    tq = imp.shape[0]
    grp = 8
    imp_t = jnp.transpose(imp)[0:n_blocks]
    blk = lax.broadcasted_iota(jnp.int32, (n_blocks, tq), 0)
    tpos = s0 + lax.broadcasted_iota(jnp.int32, (n_blocks, tq), 1)
    forced = (blk == 0) | (blk == tpos // SEL_BLOCK)
    impm = jnp.where(forced, FORCED, jnp.where(blk * SEL_BLOCK <= tpos, imp_t, -1.0))
    cnt_ref[...] = jnp.zeros(cnt_ref.shape, F32)
    sub = lax.broadcasted_iota(jnp.int32, (grp, tq), 0)
    for jb in range(n_blocks // grp):
        @pl.when(jb * grp * SEL_BLOCK <= s0 + tq - 1)
        def _(jb=jb):
            for rg in range(n_blocks // grp):
                x = impm[rg * grp:(rg + 1) * grp]
                cnt = cnt_ref[rg * grp:(rg + 1) * grp, :]
                for j in range(jb * grp, (jb + 1) * grp):
                    row = impm[j:j + 1, :]
                    if rg > jb:
                        beat = row >= x
                    elif rg < jb:
                        beat = row > x
                    else:
                        cnt = cnt + jnp.where(sub > j - jb * grp, jnp.where(row >= x, 1.0, 0.0),
                                              jnp.where(row > x, 1.0, 0.0))
                        continue
                    cnt = cnt + jnp.where(beat, 1.0, 0.0)
                cnt_ref[rg * grp:(rg + 1) * grp, :] = cnt
    sel_t = jnp.where(cnt_ref[...] < min(SEL_TOP, n_blocks), 1.0, 0.0)
    return jnp.transpose(jnp.concatenate([sel_t, jnp.zeros((LANES - n_blocks, tq), F32)], axis=0))


def _nsa_prompt_kernel(qa_ref, small_ref, cmp_ref, slc_ref, win_ref, msel_ref, eblk_ref, gexp_ref,
                       oa_ref, m_ref, acc_ref, cnt_ref, *, tq, n_sel):
    tk = tq
    qi = pl.program_id(1)
    s0 = qi * tq
    lane = lax.broadcasted_iota(jnp.int32, (tq, LANES), 1)
    trow = lax.broadcasted_iota(jnp.int32, (tq, LANES), 0)
    tpos = s0 + trow
    row_t = lax.broadcasted_iota(jnp.int32, (tq, tk), 0)
    col_t = lax.broadcasted_iota(jnp.int32, (tq, tk), 1)
    diag_mask = col_t <= row_t
    far_mask = (2 * tk + row_t - col_t) <= WINDOW
    n_cmp = cmp_ref.shape[0]
    cmask = (lax.broadcasted_iota(jnp.int32, (tq, n_cmp), 1) * CMP_STRIDE + CMP_LEN - 1
             <= s0 + lax.broadcasted_iota(jnp.int32, (tq, n_cmp), 0))
    qa = qa_ref[...]
    ck = cmp_ref[:, 0:LANES]
    cv = cmp_ref[:, LANES:2 * LANES].astype(BF16)

    def chunk_of(mask, c):
        return mask[:, c * LANES:(c + 1) * LANES]

    def flash_step(qzb, kv_ref, kt, g, bias, mask):
        off = pl.multiple_of(kt * tk, tk)
        k2 = kv_ref[pl.ds(off, tk), 0:LANES]
        v2 = kv_ref[pl.ds(off, tk), (1 + g) * LANES:(2 + g) * LANES]

        def fix(r, c, x):
            if bias is not None:
                x = x + chunk_of(bias, c)
            if mask is not None:
                x = jnp.where(chunk_of(mask, c), x, NEG)
            return x

        _flash_step(_dot_nt(qzb, k2), R_NSA, tq, fix, [v2], m_ref, acc_ref)

    o_cmp, o_sel, o_win = [], [], []
    for g in range(G_NSA):
        qz = _group_queries(qa, g, lane)
        qzb = qz.astype(BF16)
        oc, imp = _compressed_branch(qz, ck, cv, cmask, msel_ref[...], R_NSA, tq)
        o_cmp.append(oc)
        sel = _select_blocks_t(imp, s0, n_sel, cnt_ref).astype(BF16)

        def sel_bias(kt, sel=sel):
            return (_dot(sel, eblk_ref[kt]) - 1.0) * (-NEG)

        _flash_init(m_ref, acc_ref)

        def sel_body(kt, carry, qzb=qzb, g=g, sel_bias=sel_bias):
            flash_step(qzb, slc_ref, kt, g, sel_bias(kt), None)
            return carry

        lax.fori_loop(0, qi, sel_body, 0)
        flash_step(qzb, slc_ref, qi, g, sel_bias(qi), diag_mask)
        o_sel.append(_flash_result(acc_ref))

        _flash_init(m_ref, acc_ref)

        @pl.when(qi >= 2)
        def _(qzb=qzb, g=g):
            flash_step(qzb, win_ref, qi - 2, g, None, far_mask)

        @pl.when(qi >= 1)
        def _(qzb=qzb, g=g):
            flash_step(qzb, win_ref, qi - 1, g, None, None)

        flash_step(qzb, win_ref, qi, g, None, diag_mask)
        o_win.append(_flash_result(acc_ref))

    gates = _dot_exact_lhs(small_ref[...], gexp_ref[...])
    d_a = N_HEADS * HEAD_DIM
    oa_ref[...] = (gates[:, 0:d_a] * _assemble_heads(o_cmp, tq, lane)
                   + gates[:, d_a:2 * d_a] * _assemble_heads(o_sel, tq, lane)
                   + gates[:, 2 * d_a:3 * d_a] * _assemble_heads(o_win, tq, lane))


def _nsa_prompt_call(qa, small, cmpkv, slcb, winb, msel, eblk, gexp, n_batch, seq, tq):
    nq = seq // tq
    n_cmp_rows = cmpkv.shape[0] // n_batch
    n_sel = seq // SEL_BLOCK
    qrow = lambda b, q: (b * nq + q, 0)
    per_b = lambda b, q: (b, 0)
    fixed = lambda b, q: (0, 0)
    return pl.pallas_call(
        functools.partial(_nsa_prompt_kernel, tq=tq, n_sel=n_sel),
        grid=(n_batch, nq),
        in_specs=[pl.BlockSpec((tq, qa.shape[1]), qrow), pl.BlockSpec((tq, LANES), qrow),
                  pl.BlockSpec((n_cmp_rows, 2 * LANES), per_b),
                  pl.BlockSpec((seq, 3 * LANES), per_b), pl.BlockSpec((seq, 3 * LANES), per_b),
                  pl.BlockSpec(msel.shape, fixed), pl.BlockSpec(eblk.shape, lambda b, q: (0, 0, 0)),
                  pl.BlockSpec(gexp.shape, fixed)],
        out_specs=pl.BlockSpec((tq, qa.shape[1]), qrow),
        out_shape=jax.ShapeDtypeStruct(qa.shape, F32),
        scratch_shapes=[pltpu.VMEM((R_NSA * tq, LANES), F32), pltpu.VMEM((R_NSA * tq, LANES), F32),
                        pltpu.VMEM((n_sel, tq), F32)],
        compiler_params=_params(("arbitrary", "arbitrary")),
        name="nsa_prompt",
    )(qa, small, cmpkv, slcb, winb, msel, eblk, gexp)


def _fox_prompt_kernel(q_ref, k_ref, v0_ref, v1_ref, cq_ref, ck_ref, o_ref, m_ref, acc_ref, *, tq):
    tk = tq
    qi = pl.program_id(2)
    lane = lax.broadcasted_iota(jnp.int32, (tq, LANES), 1)
    q = q_ref[...]
    qz = jnp.concatenate([jnp.where(lane < HEAD_DIM, q, 0.0), jnp.where(lane >= HEAD_DIM, q, 0.0)], axis=0).astype(BF16)
    cq = cq_ref[...]
    cq_sw = pltpu.roll(cq, HEAD_DIM, 1)
    cq_rep = [jnp.where(lane < HEAD_DIM, cq, cq_sw), jnp.where(lane >= HEAD_DIM, cq, cq_sw)]
    diag_mask = lax.broadcasted_iota(jnp.int32, (tq, tk), 1) <= lax.broadcasted_iota(jnp.int32, (tq, tk), 0)
    v_refs = (v0_ref, v1_ref)
    _flash_init(m_ref, acc_ref)

    def step(kt, mask):
        off = pl.multiple_of(kt * tk, tk)
        k2 = k_ref[pl.ds(off, tk), :]
        ck = [ck_ref[0, e, pl.ds(kt, 1), :] for e in range(2)]

        def fix(e, c, x):
            x = x + (cq_rep[e] - ck[e][:, c * LANES:(c + 1) * LANES])
            if mask is not None:
                x = jnp.where(mask[:, c * LANES:(c + 1) * LANES], x, NEG)
            return x

        _flash_step(_dot_nt(qz, k2), 2, tq, fix, [r[pl.ds(off, tk), :] for r in v_refs], m_ref, acc_ref)

    def body(kt, carry):
        step(kt, None)
        return carry

    lax.fori_loop(0, qi, body, 0)
    step(qi, diag_mask)
    o = _flash_result(acc_ref)
    o_ref[...] = jnp.where(lane < HEAD_DIM, o[0:tq], o[tq:2 * tq])


def _fox_prompt_call(qb, foxb, cexp, ckt, n_batch, seq, tq):
    nq = seq // tq
    n_pairs = N_HEADS // 2
    qmap = lambda b, hp, q: (b * nq + q, hp)
    return pl.pallas_call(
        functools.partial(_fox_prompt_kernel, tq=tq),
        grid=(n_batch, n_pairs, nq),
        in_specs=[pl.BlockSpec((tq, LANES), qmap),
                  pl.BlockSpec((seq, LANES), lambda b, hp, q: (b, hp)),
                  pl.BlockSpec((seq, LANES), lambda b, hp, q: (b, n_pairs + hp)),
                  pl.BlockSpec((seq, LANES), lambda b, hp, q: (b, 2 * n_pairs + hp)),
                  pl.BlockSpec((tq, LANES), qmap),
                  pl.BlockSpec((1, 2, nq, tq), lambda b, hp, q: (b, hp, 0, 0))],
        out_specs=pl.BlockSpec((tq, LANES), qmap),
        out_shape=jax.ShapeDtypeStruct(qb.shape, F32),
        scratch_shapes=[pltpu.VMEM((2 * tq, LANES), F32), pltpu.VMEM((2 * tq, LANES), F32)],
        compiler_params=_params(("arbitrary", "arbitrary", "arbitrary")),
        name="fox_prompt",
    )(qb, foxb, foxb, foxb, cexp, ckt)


def _pad_rows(x, rows):
    return jnp.concatenate([x, jnp.zeros((rows - x.shape[0], x.shape[1]), x.dtype)], axis=0)


def _nsa_sample_kernel(pt_ref, qa_ref, small_ref, cmp_ref, *refs, n_pages, past, n_sel):
    del pt_ref
    page_refs = refs[:n_pages]
    slcn_ref, winbuf_ref, winn_ref, msel_ref, eblk_ref, oa_ref, newwin_ref = refs[n_pages:]
    s_new = qa_ref.shape[0]
    tok_rows = 2 * G_NSA
    wb = winbuf_ref.shape[0] // tok_rows
    n_keys = past + PAGE_SIZE
    lane = lax.broadcasted_iota(jnp.int32, (s_new, LANES), 1)
    tpos = past + lax.broadcasted_iota(jnp.int32, (s_new, LANES), 0)
    n_cmp = cmp_ref.shape[1]
    cmask = (lax.broadcasted_iota(jnp.int32, (s_new, n_cmp), 1) * CMP_STRIDE + CMP_LEN - 1
             <= past + lax.broadcasted_iota(jnp.int32, (s_new, n_cmp), 0))
    kpos = lax.broadcasted_iota(jnp.int32, (s_new, n_keys), 1)
    key_ok = kpos <= past + lax.broadcasted_iota(jnp.int32, (s_new, n_keys), 0)
    w_rows = wb + PAGE_SIZE
    wi = lax.broadcasted_iota(jnp.int32, (s_new, w_rows), 1)
    wt = lax.broadcasted_iota(jnp.int32, (s_new, w_rows), 0)
    dist = wb + wt - wi
    win_ok = (dist >= 0) & (dist <= WINDOW) & (past - wb + wi >= 0) & (wi < wb + s_new)
    newwin_ref[0:(wb - s_new) * tok_rows, :] = winbuf_ref[s_new * tok_rows:wb * tok_rows, :]
    newwin_ref[(wb - s_new) * tok_rows:wb * tok_rows, :] = winn_ref[...]

    def slot_rows(ref, slot, n):
        return ref[pl.ds(slot, n, stride=tok_rows), :]

    def gathered(slot):
        parts = [slot_rows(r, slot, PAGE_SIZE) for r in page_refs]
        parts.append(_pad_rows(slot_rows(slcn_ref, slot, s_new), PAGE_SIZE))
        return jnp.concatenate(parts, axis=0).astype(BF16)

    def window(slot):
        return jnp.concatenate([slot_rows(winbuf_ref, slot, wb),
                                _pad_rows(slot_rows(winn_ref, slot, s_new), PAGE_SIZE)], axis=0).astype(BF16)

    def softmax_pv(s, mask, v):
        s = jnp.concatenate([jnp.where(mask, s[r * s_new:(r + 1) * s_new], NEG) for r in range(R_NSA)], axis=0)
        mx = jnp.max(s, axis=-1, keepdims=True)
        p = jnp.exp(s - mx)
        den = jnp.sum(p, axis=-1, keepdims=True)
        return _dot(p.astype(BF16), v) / den

    qa = qa_ref[...]
    small = small_ref[...]
    heads = [None] * N_HEADS
    for g in range(G_NSA):
        q = jnp.concatenate([qa[:, (R_NSA * g + r) * HEAD_DIM:(R_NSA * g + r + 1) * HEAD_DIM] for r in range(R_NSA)],
                            axis=0)
        oc, imp = _compressed_branch(q, cmp_ref[g], cmp_ref[G_NSA + g].astype(BF16), cmask, msel_ref[...], R_NSA, s_new)
        sel = _select_blocks(imp, tpos, lane, n_sel).astype(BF16)
        sel_ok = (_dot(sel, eblk_ref[...]) > 0.5) & key_ok
        qb = q.astype(BF16)
        osel = softmax_pv(_dot_nt(qb, gathered(g)), sel_ok, gathered(G_NSA + g))
        owin = softmax_pv(_dot_nt(qb, window(g)), win_ok, window(G_NSA + g))
        for r in range(R_NSA):
            head = R_NSA * g + r
            rows = slice(r * s_new, (r + 1) * s_new)
            gate = [small[:, i * N_HEADS + head:i * N_HEADS + head + 1] for i in range(3)]
            heads[head] = gate[0] * oc[rows] + gate[1] * osel[rows] + gate[2] * owin[rows]
    oa_ref[...] = jnp.concatenate(heads, axis=1)


def _nsa_sample_call(pt_flat, qa, small, cmpkv, pool_rows, slcn, winbuf, winn, msel, eblk, n_batch, s_new, pages_per_seq):
    past = pages_per_seq * PAGE_SIZE
    n_sel = -(-(past + s_new) // SEL_BLOCK)
    n_cmp_rows = cmpkv.shape[1] // n_batch
    tok_rows = 2 * G_NSA
    wb_rows = winbuf.shape[0] // n_batch
    row = lambda b, pt: (b, 0)
    fixed = lambda b, pt: (0, 0)

    def page_spec(j):
        return pl.BlockSpec((PAGE_SIZE * tok_rows, HEAD_DIM), lambda b, pt: (pt[b * pages_per_seq + j], 0))

    grid_spec = pltpu.PrefetchScalarGridSpec(
        num_scalar_prefetch=1,
        grid=(n_batch,),
        in_specs=[pl.BlockSpec((s_new, qa.shape[1]), row), pl.BlockSpec((s_new, LANES), row),
                  pl.BlockSpec((tok_rows, n_cmp_rows, HEAD_DIM), lambda b, pt: (0, b, 0))]
                 + [page_spec(j) for j in range(pages_per_seq)]
                 + [pl.BlockSpec((s_new * tok_rows, HEAD_DIM), row),
                    pl.BlockSpec((wb_rows, HEAD_DIM), row),
                    pl.BlockSpec((s_new * tok_rows, HEAD_DIM), row),
                    pl.BlockSpec(msel.shape, fixed), pl.BlockSpec(eblk.shape, fixed)],
        out_specs=[pl.BlockSpec((s_new, qa.shape[1]), row), pl.BlockSpec((wb_rows, HEAD_DIM), row)],
    )
    return pl.pallas_call(
        functools.partial(_nsa_sample_kernel, n_pages=pages_per_seq, past=past, n_sel=n_sel),
        grid_spec=grid_spec,
        out_shape=[jax.ShapeDtypeStruct(qa.shape, F32), jax.ShapeDtypeStruct(winbuf.shape, F32)],
        compiler_params=_params(("arbitrary",)),
        name="nsa_sample",
    )(pt_flat, qa, small, cmpkv, *([pool_rows] * pages_per_seq), slcn, winbuf, winn, msel, eblk)


def _fox_sample_kernel(pt_ref, q_ref, small_ref, foxn_ref, *refs, n_pages, past):
    del pt_ref
    page_refs = refs[:n_pages]
    logf_refs = refs[n_pages:2 * n_pages]
    psel_ref, triu_ref, eye_ref, o_ref, stage_ref, kv_ref = refs[2 * n_pages:]
    s_new = q_ref.shape[0]
    n_keys = past + PAGE_SIZE
    tok_rows = 2 * N_HEADS
    key_ok = (lax.broadcasted_iota(jnp.int32, (s_new, n_keys), 1)
              <= past + lax.broadcasted_iota(jnp.int32, (s_new, n_keys), 0))

    small_pad = _pad_rows(small_ref[...], LANES)
    a, b, c = _split3(small_pad)
    psel = psel_ref[...]
    new_t = _dot_nt(psel, a) + _dot_nt(psel, b) + _dot_nt(psel, c)
    w0 = jnp.concatenate([r[...] for r in logf_refs] + [new_t], axis=0)
    wc = _dot_exact_lhs(w0, triu_ref[...])
    run = jnp.zeros((N_HEADS, 1), F32)
    c_rows = []
    for j in range(n_pages + 1):
        blk = wc[j * N_HEADS:(j + 1) * N_HEADS]
        c_rows.append(blk + run)
        run = run + blk[:, LANES - 1:LANES]
    a, b, c = _split3(_pad_rows(c_rows[n_pages], LANES))
    eye = eye_ref[...]
    cq_t = _dot_nt(eye, a) + _dot_nt(eye, b) + _dot_nt(eye, c)

    fan = 4
    assert tok_rows == fan * fan
    for j, page in enumerate(page_refs):
        st = stage_ref.at[j % 2]
        for b in range(fan):
            st[b] = page[pl.ds(b, fan * PAGE_SIZE, stride=fan), :]
        for slot in range(tok_rows):
            a, b = divmod(slot, fan)
            kv_ref[slot, j * PAGE_SIZE:(j + 1) * PAGE_SIZE, :] = st[b, pl.ds(a, PAGE_SIZE, stride=fan), :].astype(BF16)
    for slot in range(tok_rows):
        new = _pad_rows(foxn_ref[pl.ds(slot, s_new, stride=tok_rows), :], PAGE_SIZE)
        kv_ref[slot, past:past + PAGE_SIZE, :] = new.astype(BF16)

    q = q_ref[...]
    outs = []
    for head in range(N_HEADS):
        qh = q[:, head * HEAD_DIM:(head + 1) * HEAD_DIM].astype(BF16)
        ck_row = jnp.concatenate([cr[head:head + 1, :] for cr in c_rows], axis=1)
        s = _dot_nt(qh, kv_ref[head]) + (cq_t[0:s_new, head:head + 1] - ck_row)
        s = jnp.where(key_ok, s, NEG)
        mx = jnp.max(s, axis=-1, keepdims=True)
        p = jnp.exp(s - mx)
        den = jnp.sum(p, axis=-1, keepdims=True)
        outs.append(_dot(p.astype(BF16), kv_ref[N_HEADS + head]) / den)
    o_ref[...] = jnp.concatenate(outs, axis=1)


def _fox_sample_call(pt_flat, qb, small, foxn, pool_rows, logf_t, psel, triu, eye, n_batch, s_new, pages_per_seq):
    past = pages_per_seq * PAGE_SIZE
    tok_rows = 2 * N_HEADS
    row = lambda b, pt: (b, 0)
    fixed = lambda b, pt: (0, 0)

    def page_spec(j):
        return pl.BlockSpec((PAGE_SIZE * tok_rows, HEAD_DIM), lambda b, pt: (pt[b * pages_per_seq + j], 0))

    def logf_spec(j):
        return pl.BlockSpec((None,) + logf_t.shape[1:], lambda b, pt: (pt[b * pages_per_seq + j], 0, 0))

    grid_spec = pltpu.PrefetchScalarGridSpec(
        num_scalar_prefetch=1,
        grid=(n_batch,),
        in_specs=[pl.BlockSpec((s_new, qb.shape[1]), row), pl.BlockSpec((s_new, LANES), row),
                  pl.BlockSpec((s_new * tok_rows, HEAD_DIM), row)]
                 + [page_spec(j) for j in range(pages_per_seq)]
                 + [logf_spec(j) for j in range(pages_per_seq)]
                 + [pl.BlockSpec(psel.shape, fixed), pl.BlockSpec(triu.shape, fixed), pl.BlockSpec(eye.shape, fixed)],
        out_specs=pl.BlockSpec((s_new, qb.shape[1]), row),
        scratch_shapes=[pltpu.VMEM((2, 4, 4 * PAGE_SIZE, HEAD_DIM), F32),
                        pltpu.VMEM((tok_rows, past + PAGE_SIZE, HEAD_DIM), BF16)],
    )
    return pl.pallas_call(
        functools.partial(_fox_sample_kernel, n_pages=pages_per_seq, past=past),
        grid_spec=grid_spec,
        out_shape=jax.ShapeDtypeStruct(qb.shape, F32),
        compiler_params=_params(("arbitrary",)),
        name="fox_sample",
    )(pt_flat, qb, small, foxn, *([pool_rows] * pages_per_seq), *([logf_t] * pages_per_seq), psel, triu, eye)


def _merge_kernel(x_ref, oa_ref, ob_ref, z_ref, w_ref, y_ref):
    d_a = oa_ref.shape[1]
    z = z_ref[...]
    gate = z * _sigmoid(z)
    u = jnp.concatenate([oa_ref[...] * gate[:, 0:d_a], ob_ref[...] * gate[:, d_a:]], axis=1).astype(BF16)
    y_ref[...] = x_ref[...] + _dot(u, w_ref[...])


def _merge_call(x2d, oa, ob, z, w_out_bf, tm, name):
    n, d = x2d.shape
    row = lambda i: (i, 0)
    return pl.pallas_call(
        _merge_kernel,
        grid=(n // tm,),
        in_specs=[pl.BlockSpec((tm, d), row), pl.BlockSpec((tm, oa.shape[1]), row), pl.BlockSpec((tm, ob.shape[1]), row),
                  pl.BlockSpec((tm, z.shape[1]), row), pl.BlockSpec(w_out_bf.shape, lambda i: (0, 0))],
        out_specs=pl.BlockSpec((tm, d), row),
        out_shape=jax.ShapeDtypeStruct((n, d), F32),
        compiler_params=_params(("arbitrary",)),
        name=name,
    )(x2d, oa, ob, z, w_out_bf)


def _row_tile(n, preferred):
    return preferred if n % preferred == 0 else n


def _rope_tables(pos):
    half = HEAD_DIM // 2
    inv = jnp.power(jnp.float32(ROPE_THETA), -jnp.arange(half, dtype=F32) / half)
    ang = pos.astype(F32)[:, None] * inv[None, :]
    cos = jnp.cos(ang)
    sin = jnp.sin(ang)
    return jnp.tile(cos, (1, 4)), jnp.tile(jnp.concatenate([-sin, sin], axis=1), (1, 2))


def _cmp_to_sel(n_cmp, n_sel, rows, cols):
    cs = np.arange(n_cmp) * CMP_STRIDE
    ss = np.arange(n_sel) * SEL_BLOCK
    ov = np.clip(np.minimum(cs[:, None] + CMP_LEN, ss[None, :] + SEL_BLOCK) - np.maximum(cs[:, None], ss[None, :]), 0, None)
    m = np.zeros((rows, cols), np.float32)
    m[:n_cmp, :n_sel] = ov / CMP_LEN
    return jnp.asarray(m, dtype=BF16)


def _block_expand(n_keys):
    e = np.zeros((LANES, n_keys), np.float32)
    k = np.arange(n_keys)
    e[k // SEL_BLOCK, k] = 1.0
    return jnp.asarray(e, dtype=BF16)


def _gate_expand():
    e = np.zeros((LANES, 3 * N_HEADS * HEAD_DIM), np.float32)
    for i in range(3):
        for h in range(N_HEADS):
            e[i * N_HEADS + h, i * N_HEADS * HEAD_DIM + h * HEAD_DIM:i * N_HEADS * HEAD_DIM + (h + 1) * HEAD_DIM] = 1.0
    return jnp.asarray(e, dtype=BF16)


def _head_expand():
    e = np.zeros((LANES, N_HEADS * HEAD_DIM), np.float32)
    for h in range(N_HEADS):
        e[LOGF_LANE + h, h * HEAD_DIM:(h + 1) * HEAD_DIM] = 1.0
    return jnp.asarray(e, dtype=BF16)


def _compress_weights(w1, w2, pe):
    half = CMP_LEN // 2
    eye = jnp.eye(G_NSA, dtype=F32)
    w = w1.reshape(2, half, HEAD_DIM, CMP_HID)
    w1x = jnp.einsum('pldh,gk->lgdpkh', w, eye).reshape(half * G_NSA * HEAD_DIM, 2 * G_NSA * CMP_HID).astype(BF16)
    w2x = jnp.einsum('hd,gk->ghkd', w2, eye).reshape(G_NSA * CMP_HID, G_NSA * HEAD_DIM).astype(BF16)
    pe2 = jnp.broadcast_to(pe.reshape(2, half, 1, HEAD_DIM), (2, half, G_NSA, HEAD_DIM)).reshape(2, -1)
    pex = jnp.concatenate([pe2, jnp.zeros((14, pe2.shape[1]), F32)], axis=0).astype(BF16)
    return w1x, w2x, pex


def _compress_weights_rows(w1s, w2s, pes, gain):
    half = CMP_LEN // 2
    w1 = jnp.stack([jnp.concatenate([w[:half], w[half:]], axis=2) for w in w1s]).astype(BF16)
    pe = jnp.stack([jnp.concatenate([jnp.stack([p[:half], p[half:]], axis=1),
                                     jnp.zeros((half, 14, HEAD_DIM), F32)], axis=1) for p in pes]).astype(BF16)
    return w1, pe, jnp.stack(w2s).astype(BF16), gain[None, :]


def kernel(x_prompt, x_sample, cache_nsa_cmp_kv, cache_nsa_slc_kv, cache_nsa_win_kv, cache_fox_kv, cache_fox_logf,
           page_table, g_norm, w_in, b_f, gq_a, gk_cmp, gk_slc, gk_win, pe_cmp_k, pe_cmp_v,
           w_cmp1_k, w_cmp2_k, w_cmp1_v, w_cmp2_v, gq_b, gk_b, w_out):
    depth = g_norm.shape[0]
    assert depth == 1, "single-layer stack"
    n_b, seq, d_model = x_prompt.shape
    n_db, s_new, _ = x_sample.shape
    pages_per_seq = page_table.shape[1]
    past = pages_per_seq * PAGE_SIZE
    wb = cache_nsa_win_kv.shape[2]
    n_pool = cache_nsa_cmp_kv.shape[1]
    d_a = N_HEADS * HEAD_DIM
    kv_w = 2 * G_NSA * HEAD_DIM

    w = w_in[0]
    o_kc, o_ga, o_za, o_qb, o_kb, o_fb, o_zb = 512, 1280, 1304, 1816, 2328, 3352, 3360
    gate_cols = o_ga + (np.arange(N_HEADS)[None, :] * 3 + np.arange(3)[:, None]).reshape(-1)
    small_w = jnp.concatenate([w[:, gate_cols], w[:, o_fb:o_fb + N_HEADS],
                               jnp.zeros((d_model, LANES - GATE_LANES - N_HEADS), F32)], axis=1)
    w_bf = jnp.concatenate([w[:, 0:o_ga], w[:, o_za:o_qb], w[:, o_zb:o_zb + d_a], w[:, o_qb:o_fb], small_w],
                           axis=1).astype(BF16)
    assert w_bf.shape[1] == C_END
    gains = jnp.concatenate([jnp.tile(v[0], 2)[None, :] for v in (gq_a, gk_slc, gk_win, gq_b, gk_b)]
                            + [jnp.zeros((3, LANES), F32)], axis=0)
    bf_row = jnp.zeros((1, LANES), F32).at[0, LOGF_LANE:LOGF_LANE + N_HEADS].set(b_f[0])
    lane_head = np.arange(LANES) // HEAD_DIM
    ones_bd = jnp.asarray(lane_head[:, None] == lane_head[None, :], dtype=BF16)
    hexp = _head_expand()
    gexp = _gate_expand()
    consts = lambda tm: (g_norm, w_bf, gains, bf_row, ones_bd,
                         jnp.asarray(np.tril(np.ones((tm, tm), np.float32)), dtype=BF16), hexp)

    tm_p = 256
    cos_p, sin_p = _rope_tables(jnp.arange(seq, dtype=jnp.int32))
    tiles_per_seq = seq // tm_p
    (cmp_p, slc_p, win_p, fox_p, small_p, z_p, qa_p, qb_p, cexp_p, csum_p, cmpd_p, slcb_p, winb_p, foxb_p) = _proj_call(
        x_prompt.reshape(n_b * seq, d_model), (cos_p, sin_p, lambda i: (i % tiles_per_seq, 0)),
        consts(tm_p), tiles_per_seq, tm_p, "proj_prompt")
    tm_s = _row_tile(n_db * s_new, 256)
    cos_s, sin_s = _rope_tables(past + jnp.arange(s_new, dtype=jnp.int32))
    cos_s, sin_s = jnp.tile(cos_s, (tm_s // s_new, 1)), jnp.tile(sin_s, (tm_s // s_new, 1))
    (cmp_s, slc_s, win_s, fox_s, small_s, z_s, qa_s, qb_s, _, _, _, _, _, _) = _proj_call(
        x_sample.reshape(n_db * s_new, d_model), (cos_s, sin_s, lambda i: (0, 0)),
        consts(tm_s), 1, tm_s, "proj_sample")

    w1k, w2k, pek = _compress_weights(w_cmp1_k[0], w_cmp2_k[0], pe_cmp_k[0])
    w1v, w2v, pev = _compress_weights(w_cmp1_v[0], w_cmp2_v[0], pe_cmp_v[0])
    cw = (w1k, w1v, pek, pev, w2k, w2v, jnp.tile(gk_cmp[0], 2)[None, :], ones_bd)
    chunk_w = CMP_STRIDE * kv_w
    ckv_p = _compress_prompt_call(cmpd_p.reshape(n_b * seq // CMP_STRIDE, chunk_w), cw, seq // CMP_STRIDE)
    pt_flat = page_table.reshape(-1).astype(jnp.int32)
    cw_rows = _compress_weights_rows((w_cmp1_k[0], w_cmp1_v[0]), (w_cmp2_k[0], w_cmp2_v[0]),
                                     (pe_cmp_k[0], pe_cmp_v[0]), gk_cmp[0])
    cache_rows = lambda a: a[0].reshape(-1, HEAD_DIM)
    ckv_s = _compress_sample_call(pt_flat, cache_rows(cache_nsa_cmp_kv), cw_rows, n_db, pages_per_seq, 2)

    n_cmp_p = (seq - CMP_LEN) // CMP_STRIDE + 1
    msel_p = _cmp_to_sel(n_cmp_p, seq // SEL_BLOCK, seq // CMP_STRIDE, LANES)
    tq_a = 256
    eblk_p = jnp.transpose(_block_expand(seq).reshape(LANES, seq // tq_a, tq_a), (1, 0, 2))
    oa_p = _nsa_prompt_call(qa_p, small_p, ckv_p, slcb_p, winb_p, msel_p, eblk_p, gexp, n_b, seq, tq_a)
    n_cmp_s = (past + s_new - CMP_LEN) // CMP_STRIDE + 1
    n_sel_s = -(-(past + s_new) // SEL_BLOCK)
    msel_s = _cmp_to_sel(n_cmp_s, n_sel_s, past // CMP_STRIDE, LANES)
    oa_s, new_win = _nsa_sample_call(
        pt_flat, qa_s, small_s, ckv_s, cache_rows(cache_nsa_slc_kv), slc_s,
        cache_rows(cache_nsa_win_kv), win_s, msel_s, _block_expand(past + PAGE_SIZE), n_db, s_new, pages_per_seq)

    tq_f = 512
    c_p = csum_p[:, LOGF_LANE:LOGF_LANE + N_HEADS].reshape(n_b, seq, N_HEADS)
    ckt = jnp.transpose(c_p, (0, 2, 1)).reshape(n_b, N_HEADS, seq // tq_f, tq_f)
    ob_p = _fox_prompt_call(qb_p, foxb_p, cexp_p, ckt, n_b, seq, tq_f)
    psel = jnp.asarray(np.arange(LANES)[None, :] == (LOGF_LANE + np.arange(2 * N_HEADS))[:, None], dtype=BF16)
    psel = psel.at[N_HEADS:].set(0)
    triu = jnp.asarray(np.triu(np.ones((LANES, LANES), np.float32)), dtype=BF16)
    eye = jnp.asarray(np.eye(LANES, dtype=np.float32), dtype=BF16)
    ob_s = _fox_sample_call(
        pt_flat, qb_s, small_s, fox_s, cache_rows(cache_fox_kv),
        jnp.transpose(cache_fox_logf[0], (0, 2, 1)), psel, triu, eye, n_db, s_new, pages_per_seq)

    w_out_bf = w_out[0].astype(BF16)
    y_p = _merge_call(x_prompt.reshape(n_b * seq, d_model), oa_p, ob_p, z_p, w_out_bf, 512, "merge_prompt")
    y_s = _merge_call(x_sample.reshape(n_db * s_new, d_model), oa_s, ob_s, z_s, w_out_bf,
                      _row_tile(n_db * s_new, 512), "merge_sample")

    kv5 = lambda a, nb, t, heads: a.reshape(1, nb, t, 2, heads, HEAD_DIM)
    wbp = min(WINDOW, seq)
    return (y_p.reshape(n_b, seq, d_model), y_s.reshape(n_db, s_new, d_model),
            kv5(cmp_p, n_b, seq, G_NSA), kv5(cmp_s, n_db, s_new, G_NSA),
            kv5(slc_p, n_b, seq, G_NSA), kv5(slc_s, n_db, s_new, G_NSA),
            kv5(win_p.reshape(n_b, seq, 2 * G_NSA, HEAD_DIM)[:, seq - wbp:], n_b, wbp, G_NSA),
            kv5(new_win, n_db, wb, G_NSA),
            kv5(fox_p, n_b, seq, N_HEADS), kv5(fox_s, n_db, s_new, N_HEADS),
            small_p[:, LOGF_LANE:LOGF_LANE + N_HEADS].reshape(1, n_b, seq, N_HEADS),
            small_s[:, LOGF_LANE:LOGF_LANE + N_HEADS].reshape(1, n_db, s_new, N_HEADS))
```

```python
import functools

import numpy as np
import jax
import jax.numpy as jnp
from jax import lax
from jax.experimental import pallas as pl
from jax.experimental.pallas import tpu as pltpu

F32 = jnp.float32
BF16 = jnp.bfloat16

HEAD_DIM = 64
LANES = 128
N_HEADS = 8
G_NSA = 2
R_NSA = N_HEADS // G_NSA
CMP_LEN = 32
CMP_STRIDE = 16
CMP_HID = 2 * HEAD_DIM
SEL_BLOCK = 64
SEL_TOP = 16
WINDOW = 512
PAGE_SIZE = 128
ROPE_THETA = 10000.0
EPS = 1e-6
NEG = -1e30
FORCED = 1e4
QK_SCALE = HEAD_DIM ** -0.5

VMEM_LIMIT = 56 * 1024 * 1024

C_QA, C_CMP, C_SLC, C_WIN, C_Z, C_QB, C_FOX, C_SMALL, C_END = 0, 512, 768, 1024, 1280, 2304, 2816, 3840, 3968
GATE_LANES = 3 * N_HEADS
LOGF_LANE = GATE_LANES


def _dot(a, b):
    return jnp.dot(a, b, preferred_element_type=F32)


def _dot_nt(a, b):
    return lax.dot_general(a, b, (((1,), (1,)), ((), ())), preferred_element_type=F32)


def _split2(x):
    hi = x.astype(BF16)
    lo = (x - hi.astype(F32)).astype(BF16)
    return hi, lo


def _split3(x):
    hi = x.astype(BF16)
    r = x - hi.astype(F32)
    mid = r.astype(BF16)
    lo = (r - mid.astype(F32)).astype(BF16)
    return hi, mid, lo


def _dot_exact_lhs(x, m):
    a, b, c = _split3(x)
    return _dot(a, m) + _dot(b, m) + _dot(c, m)


def _dot_exact_rhs(m, x):
    a, b, c = _split3(x)
    return _dot(m, a) + _dot(m, b) + _dot(m, c)


def _sigmoid(x):
    return 1.0 / (1.0 + jnp.exp(-x))


def _params(sem):
    return pltpu.CompilerParams(dimension_semantics=sem, vmem_limit_bytes=VMEM_LIMIT)


def _proj_kernel(x_ref, g_ref, w_ref, cos_ref, sin_ref, gains_ref, bf_ref, ones_ref, tri_ref, hexp_ref,
                 cmp_ref, slc_ref, win_ref, fox_ref, small_ref, z_ref, qa_ref, qb_ref, cexp_ref, csum_ref,
                 slcb_ref, winb_ref, foxb_ref, carry_ref, *, tm, tiles_per_seq):
    i = pl.program_id(0)
    x = x_ref[...]
    ms = jnp.mean(x * x, axis=-1, keepdims=True)
    h = (x * lax.rsqrt(ms + EPS) * g_ref[...]).astype(BF16)
    cos = cos_ref[...]
    sin = sin_ref[...]
    lane = lax.broadcasted_iota(jnp.int32, (tm, LANES), 1)
    first_half = (lane & (HEAD_DIM // 2)) == 0
    ones = ones_ref[...]

    def proj(c0, c1):
        return _dot(h, w_ref[:, c0:c1])

    def head_norm(y, row):
        hi, lo = _split2(y * y)
        ss = _dot(hi, ones) + _dot(lo, ones)
        return y * lax.rsqrt(ss * (1.0 / HEAD_DIM) + EPS) * gains_ref[row:row + 1, :]

    def rope(y):
        partner = jnp.where(first_half, pltpu.roll(y, LANES - HEAD_DIM // 2, 1), pltpu.roll(y, HEAD_DIM // 2, 1))
        return y * cos + partner * sin

    def chunk(y, c):
        return y[:, c * LANES:(c + 1) * LANES]

    def with_ones(v, e):
        return jnp.where((lane < HEAD_DIM) if e == 0 else (lane >= HEAD_DIM), v, 1.0)

    y = proj(C_QA, C_CMP)
    for c in range(4):
        qa_ref[:, c * LANES:(c + 1) * LANES] = rope(head_norm(chunk(y, c), 0)) * QK_SCALE

    y = proj(C_CMP, C_SLC)
    cmp_ref[:, 0:LANES] = rope(chunk(y, 0))
    cmp_ref[:, LANES:2 * LANES] = chunk(y, 1)

    for (c0, c1, row, o_ref, ob_ref) in ((C_SLC, C_WIN, 1, slc_ref, slcb_ref), (C_WIN, C_Z, 2, win_ref, winb_ref)):
        y = proj(c0, c1)
        k = rope(head_norm(chunk(y, 0), row))
        v = chunk(y, 1)
        o_ref[:, 0:LANES] = k
        o_ref[:, LANES:2 * LANES] = v
        ob_ref[:, 0:LANES] = k.astype(BF16)
        for e in range(2):
            ob_ref[:, (1 + e) * LANES:(2 + e) * LANES] = with_ones(v, e).astype(BF16)

    z_ref[...] = proj(C_Z, C_QB)

    y = proj(C_QB, C_FOX)
    for c in range(4):
        qb_ref[:, c * LANES:(c + 1) * LANES] = head_norm(chunk(y, c), 3) * QK_SCALE

    y = proj(C_FOX, C_SMALL)
    for c in range(8):
        t = chunk(y, c)
        if c < 4:
            t = head_norm(t, 4)
        fox_ref[:, c * LANES:(c + 1) * LANES] = t
        if c < 4:
            foxb_ref[:, c * LANES:(c + 1) * LANES] = t.astype(BF16)
        else:
            for e in range(2):
                foxb_ref[:, (c + 4 * e) * LANES:(c + 4 * e + 1) * LANES] = with_ones(t, e).astype(BF16)

    raw = proj(C_SMALL, C_END)
    zf = raw + bf_ref[...]
    logf = jnp.minimum(zf, 0.0) - jnp.log1p(jnp.exp(-jnp.abs(zf)))
    small = jnp.where(lane < GATE_LANES, _sigmoid(raw), jnp.where(lane < GATE_LANES + N_HEADS, logf, 0.0))
    small_ref[...] = small

    @pl.when(i % tiles_per_seq == 0)
    def _():
        carry_ref[...] = jnp.zeros_like(carry_ref)

    c = carry_ref[0:1, :] + _dot_exact_rhs(tri_ref[...], small)
    carry_ref[0:1, :] = c[tm - 1:tm, :]
    csum_ref[...] = c
    cexp_ref[...] = _dot_exact_lhs(c, hexp_ref[...])


def _proj_call(x2d, tables, consts, tiles_per_seq, tm, name):
    n = x2d.shape[0]
    cos_t, sin_t, table_map = tables
    g_norm, w_bf, gains, bf_row, ones_bd, tri, hexp = consts
    row = lambda i: (i, 0)
    fixed = lambda i: (0, 0)
    full = lambda a: pl.BlockSpec(a.shape, fixed)
    out_widths = (256, 256, 256, 1024, 128, 1024, 512, 512, 512, 128)
    out_shape = [jax.ShapeDtypeStruct((n, w), F32) for w in out_widths]
    bf_widths = (3 * LANES, 3 * LANES, 3 * N_HEADS * HEAD_DIM)
    out_shape += [jax.ShapeDtypeStruct((n, w), BF16) for w in bf_widths]
    out_specs = [pl.BlockSpec((tm, w), row) for w in out_widths + bf_widths]
    return pl.pallas_call(
        functools.partial(_proj_kernel, tm=tm, tiles_per_seq=tiles_per_seq),
        grid=(n // tm,),
        in_specs=[pl.BlockSpec((tm, x2d.shape[1]), row), full(g_norm), full(w_bf),
                  pl.BlockSpec((tm, LANES), table_map), pl.BlockSpec((tm, LANES), table_map),
                  full(gains), full(bf_row), full(ones_bd), full(tri), full(hexp)],
        out_specs=out_specs,
        out_shape=out_shape,
        scratch_shapes=[pltpu.VMEM((8, LANES), F32)],
        compiler_params=_params(("arbitrary",)),
        name=name,
    )(x2d, g_norm, w_bf, cos_t, sin_t, gains, bf_row, ones_bd, tri, hexp)


def _compress_body(y_refs, wk_ref, wv_ref, pek_ref, pev_ref, w2k_ref, w2v_ref, gain_ref, ones_ref, out_ref,
                   rows_per_seq):
    if len(y_refs) == 1:
        y = y_refs[0][...].astype(BF16)
    else:
        y = jnp.concatenate([r[...] for r in y_refs], axis=0).astype(BF16)
    m = y.shape[0]
    chunk_tokens = CMP_STRIDE
    width = 2 * G_NSA * HEAD_DIM
    hid = G_NSA * CMP_HID

    def branch(off, w1_ref, pe_ref, w2_ref):
        ysel = jnp.concatenate([y[:, l * width + off:l * width + off + LANES] for l in range(chunk_tokens)], axis=1)
        r = _dot(jnp.concatenate([ysel, pe_ref[...]], axis=0), w1_ref[...])
        hh = r[:m, :hid] + pltpu.roll(r[:m, hid:], m - 1, 0) + r[m:m + 1, :hid] + r[m + 1:m + 2, hid:]
        act = hh * _sigmoid(hh)
        return _dot(act.astype(BF16), w2_ref[...])

    kc = branch(0, wk_ref, pek_ref, w2k_ref)
    hi, lo = _split2(kc * kc)
    ss = _dot(hi, ones_ref[...]) + _dot(lo, ones_ref[...])
    kc = kc * lax.rsqrt(ss * (1.0 / HEAD_DIM) + EPS) * gain_ref[...]
    vc = branch(LANES, wv_ref, pev_ref, w2v_ref)
    rows = lax.broadcasted_iota(jnp.int32, (m, LANES), 0)
    valid = (rows % rows_per_seq) != rows_per_seq - 1
    out_ref[:, 0:LANES] = jnp.where(valid, kc, 0.0)
    out_ref[:, LANES:2 * LANES] = jnp.where(valid, vc, 0.0)


def _compress_prompt_kernel(y_ref, *rest, rows_per_seq):
    _compress_body([y_ref], *rest, rows_per_seq=rows_per_seq)


def _compress_sample_kernel(pt_ref, *refs, n_pages, rows_per_seq):
    del pt_ref
    _compress_body(list(refs[:n_pages]), *refs[n_pages:], rows_per_seq=rows_per_seq)


def _compress_prompt_call(y2d, cw, rows_per_seq):
    n = y2d.shape[0]
    fixed = lambda i: (0, 0)
    return pl.pallas_call(
        functools.partial(_compress_prompt_kernel, rows_per_seq=rows_per_seq),
        grid=(n // rows_per_seq,),
        in_specs=[pl.BlockSpec((rows_per_seq, y2d.shape[1]), lambda i: (i, 0))] + [pl.BlockSpec(a.shape, fixed) for a in cw],
        out_specs=pl.BlockSpec((rows_per_seq, 2 * LANES), lambda i: (i, 0)),
        out_shape=jax.ShapeDtypeStruct((n, 2 * LANES), F32),
        compiler_params=_params(("arbitrary",)),
        name="compress_prompt",
    )(y2d, *cw)


def _compress_sample_call(pt_flat, pool3d, cw, n_batch, pages_per_seq, batch_per_step):
    rows_per_page = pool3d.shape[1]
    rows_per_seq = pages_per_seq * rows_per_page
    n_pages = batch_per_step * pages_per_seq
    fixed = lambda i, pt: (0, 0)

    def page_spec(k):
        bb, j = divmod(k, pages_per_seq)
        return pl.BlockSpec((None, rows_per_page, pool3d.shape[2]),
                            lambda i, pt: (pt[(i * batch_per_step + bb) * pages_per_seq + j], 0, 0))

    m = batch_per_step * rows_per_seq
    grid_spec = pltpu.PrefetchScalarGridSpec(
        num_scalar_prefetch=1,
        grid=(n_batch // batch_per_step,),
        in_specs=[page_spec(k) for k in range(n_pages)] + [pl.BlockSpec(a.shape, fixed) for a in cw],
        out_specs=pl.BlockSpec((m, 2 * LANES), lambda i, pt: (i, 0)),
    )
    return pl.pallas_call(
        functools.partial(_compress_sample_kernel, n_pages=n_pages, rows_per_seq=rows_per_seq),
        grid_spec=grid_spec,
        out_shape=jax.ShapeDtypeStruct((n_batch * rows_per_seq, 2 * LANES), F32),
        compiler_params=_params(("arbitrary",)),
        name="compress_sample",
    )(pt_flat, *([pool3d] * n_pages), *cw)


def _group_queries(qa, g, lane):
    in_group = (lane < HEAD_DIM) if g == 0 else (lane >= HEAD_DIM)
    rows = []
    for r in range(R_NSA):
        head = R_NSA * g + r
        y = qa[:, (head // 2) * LANES:(head // 2 + 1) * LANES]
        if head % 2 != g:
            y = pltpu.roll(y, HEAD_DIM, 1)
        rows.append(jnp.where(in_group, y, 0.0))
    return jnp.concatenate(rows, axis=0)


def _assemble_heads(res, tq, lane):
    chunks = []
    for c in range(N_HEADS // 2):
        g = c // 2
        a = res[g][(2 * (c % 2)) * tq:(2 * (c % 2) + 1) * tq]
        b = res[g][(2 * (c % 2) + 1) * tq:(2 * (c % 2) + 2) * tq]
        if g == 0:
            b = pltpu.roll(b, HEAD_DIM, 1)
        else:
            a = pltpu.roll(a, HEAD_DIM, 1)
        chunks.append(jnp.where(lane < HEAD_DIM, a, b))
    return jnp.concatenate(chunks, axis=1)


def _flash_step(s, n_rb, rb, fix, values, m_ref, acc_ref):
    n_c = s.shape[1] // LANES
    cols = [jnp.concatenate([fix(r, c, s[r * rb:(r + 1) * rb, c * LANES:(c + 1) * LANES]) for r in range(n_rb)], axis=0)
            for c in range(n_c)]
    mx = cols[0]
    for c in range(1, n_c):
        mx = jnp.maximum(mx, cols[c])
    m_prev = m_ref[...]
    m_new = jnp.maximum(m_prev, jnp.max(mx, axis=-1, keepdims=True))
    alpha = jnp.exp(m_prev - m_new)
    p = jnp.concatenate([jnp.exp(x - m_new) for x in cols], axis=1).astype(BF16)
    if len(values) == 1:
        pv = _dot(p, values[0])
    else:
        pv = jnp.concatenate([_dot(p[r * rb:(r + 1) * rb], values[r]) for r in range(n_rb)], axis=0)
    acc_ref[...] = alpha * acc_ref[...] + pv
    m_ref[...] = m_new


def _flash_init(m_ref, acc_ref):
    m_ref[...] = jnp.full(m_ref.shape, NEG, F32)
    acc_ref[...] = jnp.zeros(acc_ref.shape, F32)


def _flash_result(acc_ref):
    acc = acc_ref[...]
    return acc / pltpu.roll(acc, HEAD_DIM, 1)


def _block_ranks(impm, n_blocks, lane):
    cnt = jnp.zeros(impm.shape, F32)
    for j in range(n_blocks):
        col = impm[:, j:j + 1]
        ge = jnp.where(col >= impm, 1.0, 0.0)
        gt = jnp.where(col > impm, 1.0, 0.0)
        cnt = cnt + jnp.where(lane > j, ge, gt)
    return cnt


def _compressed_branch(qz, ck, cv, cmask, msel, n_heads, tq):
    q_hi, q_lo = _split2(qz)
    k_hi, k_lo = _split2(ck)
    s = _dot_nt(q_hi, k_hi) + _dot_nt(q_hi, k_lo) + _dot_nt(q_lo, k_hi)
    ps = []
    psum = None
    for r in range(n_heads):
        sr = jnp.where(cmask, s[r * tq:(r + 1) * tq], NEG)
        mx = jnp.max(sr, axis=-1, keepdims=True)
        e = jnp.where(cmask, jnp.exp(sr - mx), 0.0)
        den = jnp.sum(e, axis=-1, keepdims=True)
        p = e / jnp.where(den > 0.0, den, 1.0)
        ps.append(p)
        psum = p if psum is None else psum + p
    o = _dot(jnp.concatenate(ps, axis=0).astype(BF16), cv)
    return o, _dot_exact_lhs(psum, msel)


def _select_blocks(imp, tpos, lane, n_blocks):
    forced = (lane == 0) | (lane == tpos // SEL_BLOCK)
    causal = lane * SEL_BLOCK <= tpos
    impm = jnp.where(forced, FORCED, jnp.where(causal, imp, -1.0))
    impm = jnp.where(lane < n_blocks, impm, -3e38)
    cnt = _block_ranks(impm, n_blocks, lane)
    n_top = min(SEL_TOP, n_blocks)
    return jnp.where(cnt < n_top, jnp.where(lane < n_blocks, 1.0, 0.0), 0.0)


def _select_blocks_t(imp, s0, n_blocks, cnt_ref):
    tq = imp.shape[0]
    grp = 8
    imp_t = jnp.transpose(imp)[0:n_blocks]
    blk = lax.broadcasted_iota(jnp.int32, (n_blocks, tq), 0)
    tpos = s0 + lax.broadcasted_iota(jnp.int32, (n_blocks, tq), 1)
    forced = (blk == 0) | (blk == tpos // SEL_BLOCK)
    impm = jnp.where(forced, FORCED, jnp.where(blk * SEL_BLOCK <= tpos, imp_t, -1.0))
    cnt_ref[...] = jnp.zeros(cnt_ref.shape, F32)
    sub = lax.broadcasted_iota(jnp.int32, (grp, tq), 0)
    for jb in range(n_blocks // grp):
        @pl.when(jb * grp * SEL_BLOCK <= s0 + tq - 1)
        def _(jb=jb):
            for rg in range(n_blocks // grp):
                x = impm[rg * grp:(rg + 1) * grp]
                cnt = cnt_ref[rg * grp:(rg + 1) * grp, :]
                for j in range(jb * grp, (jb + 1) * grp):
                    row = impm[j:j + 1, :]
                    if rg > jb:
                        beat = row >= x
                    elif rg < jb:
                        beat = row > x
                    else:
                        cnt = cnt + jnp.where(sub > j - jb * grp, jnp.where(row >= x, 1.0, 0.0),
                                              jnp.where(row > x, 1.0, 0.0))
                        continue
                    cnt = cnt + jnp.where(beat, 1.0, 0.0)
                cnt_ref[rg * grp:(rg + 1) * grp, :] = cnt
    sel_t = jnp.where(cnt_ref[...] < min(SEL_TOP, n_blocks), 1.0, 0.0)
    return jnp.transpose(jnp.concatenate([sel_t, jnp.zeros((LANES - n_blocks, tq), F32)], axis=0))


def _nsa_prompt_kernel(qa_ref, small_ref, cmp_ref, slc_ref, win_ref, msel_ref, eblk_ref, gexp_ref,
                       oa_ref, m_ref, acc_ref, cnt_ref, *, tq, n_sel):
    tk = tq
    qi = pl.program_id(1)
    s0 = qi * tq
    lane = lax.broadcasted_iota(jnp.int32, (tq, LANES), 1)
    trow = lax.broadcasted_iota(jnp.int32, (tq, LANES), 0)
    tpos = s0 + trow
    row_t = lax.broadcasted_iota(jnp.int32, (tq, tk), 0)
    col_t = lax.broadcasted_iota(jnp.int32, (tq, tk), 1)
    diag_mask = col_t <= row_t
    far_mask = (2 * tk + row_t - col_t) <= WINDOW
    n_cmp = cmp_ref.shape[0]
    cmask = (lax.broadcasted_iota(jnp.int32, (tq, n_cmp), 1) * CMP_STRIDE + CMP_LEN - 1
             <= s0 + lax.broadcasted_iota(jnp.int32, (tq, n_cmp), 0))
    qa = qa_ref[...]
    ck = cmp_ref[:, 0:LANES]
    cv = cmp_ref[:, LANES:2 * LANES].astype(BF16)

    def chunk_of(mask, c):
        return mask[:, c * LANES:(c + 1) * LANES]

    def flash_step(qzb, kv_ref, kt, g, bias, mask):
        off = pl.multiple_of(kt * tk, tk)
        k2 = kv_ref[pl.ds(off, tk), 0:LANES]
        v2 = kv_ref[pl.ds(off, tk), (1 + g) * LANES:(2 + g) * LANES]

        def fix(r, c, x):
            if bias is not None:
                x = x + chunk_of(bias, c)
            if mask is not None:
                x = jnp.where(chunk_of(mask, c), x, NEG)
            return x

        _flash_step(_dot_nt(qzb, k2), R_NSA, tq, fix, [v2], m_ref, acc_ref)

    o_cmp, o_sel, o_win = [], [], []
    for g in range(G_NSA):
        qz = _group_queries(qa, g, lane)
        qzb = qz.astype(BF16)
        oc, imp = _compressed_branch(qz, ck, cv, cmask, msel_ref[...], R_NSA, tq)
        o_cmp.append(oc)
        sel = _select_blocks_t(imp, s0, n_sel, cnt_ref).astype(BF16)

        def sel_bias(kt, sel=sel):
            return (_dot(sel, eblk_ref[kt]) - 1.0) * (-NEG)

        _flash_init(m_ref, acc_ref)

        def sel_body(kt, carry, qzb=qzb, g=g, sel_bias=sel_bias):
            flash_step(qzb, slc_ref, kt, g, sel_bias(kt), None)
            return carry

        lax.fori_loop(0, qi, sel_body, 0)
        flash_step(qzb, slc_ref, qi, g, sel_bias(qi), diag_mask)
        o_sel.append(_flash_result(acc_ref))

        _flash_init(m_ref, acc_ref)

        @pl.when(qi >= 2)
        def _(qzb=qzb, g=g):
            flash_step(qzb, win_ref, qi - 2, g, None, far_mask)

        @pl.when(qi >= 1)
        def _(qzb=qzb, g=g):
            flash_step(qzb, win_ref, qi - 1, g, None, None)

        flash_step(qzb, win_ref, qi, g, None, diag_mask)
        o_win.append(_flash_result(acc_ref))

    gates = _dot_exact_lhs(small_ref[...], gexp_ref[...])
    d_a = N_HEADS * HEAD_DIM
    oa_ref[...] = (gates[:, 0:d_a] * _assemble_heads(o_cmp, tq, lane)
                   + gates[:, d_a:2 * d_a] * _assemble_heads(o_sel, tq, lane)
                   + gates[:, 2 * d_a:3 * d_a] * _assemble_heads(o_win, tq, lane))


def _nsa_prompt_call(qa, small, cmpkv, slcb, winb, msel, eblk, gexp, n_batch, seq, tq):
    nq = seq // tq
    n_cmp_rows = cmpkv.shape[0] // n_batch
    n_sel = seq // SEL_BLOCK
    qrow = lambda b, q: (b * nq + q, 0)
    per_b = lambda b, q: (b, 0)
    fixed = lambda b, q: (0, 0)
    return pl.pallas_call(
        functools.partial(_nsa_prompt_kernel, tq=tq, n_sel=n_sel),
        grid=(n_batch, nq),
        in_specs=[pl.BlockSpec((tq, qa.shape[1]), qrow), pl.BlockSpec((tq, LANES), qrow),
                  pl.BlockSpec((n_cmp_rows, 2 * LANES), per_b),
                  pl.BlockSpec((seq, 3 * LANES), per_b), pl.BlockSpec((seq, 3 * LANES), per_b),
                  pl.BlockSpec(msel.shape, fixed), pl.BlockSpec(eblk.shape, lambda b, q: (0, 0, 0)),
                  pl.BlockSpec(gexp.shape, fixed)],
        out_specs=pl.BlockSpec((tq, qa.shape[1]), qrow),
        out_shape=jax.ShapeDtypeStruct(qa.shape, F32),
        scratch_shapes=[pltpu.VMEM((R_NSA * tq, LANES), F32), pltpu.VMEM((R_NSA * tq, LANES), F32),
                        pltpu.VMEM((n_sel, tq), F32)],
        compiler_params=_params(("arbitrary", "arbitrary")),
        name="nsa_prompt",
    )(qa, small, cmpkv, slcb, winb, msel, eblk, gexp)


def _fox_prompt_kernel(q_ref, k_ref, v0_ref, v1_ref, cq_ref, ck_ref, o_ref, m_ref, acc_ref, *, tq):
    tk = tq
    qi = pl.program_id(2)
    lane = lax.broadcasted_iota(jnp.int32, (tq, LANES), 1)
    q = q_ref[...]
    qz = jnp.concatenate([jnp.where(lane < HEAD_DIM, q, 0.0), jnp.where(lane >= HEAD_DIM, q, 0.0)], axis=0).astype(BF16)
    cq = cq_ref[...]
    cq_sw = pltpu.roll(cq, HEAD_DIM, 1)
    cq_rep = [jnp.where(lane < HEAD_DIM, cq, cq_sw), jnp.where(lane >= HEAD_DIM, cq, cq_sw)]
    diag_mask = lax.broadcasted_iota(jnp.int32, (tq, tk), 1) <= lax.broadcasted_iota(jnp.int32, (tq, tk), 0)
    v_refs = (v0_ref, v1_ref)
    _flash_init(m_ref, acc_ref)

    def step(kt, mask):
        off = pl.multiple_of(kt * tk, tk)
        k2 = k_ref[pl.ds(off, tk), :]
        ck = [ck_ref[0, e, pl.ds(kt, 1), :] for e in range(2)]

        def fix(e, c, x):
            x = x + (cq_rep[e] - ck[e][:, c * LANES:(c + 1) * LANES])
            if mask is not None:
                x = jnp.where(mask[:, c * LANES:(c + 1) * LANES], x, NEG)
            return x

        _flash_step(_dot_nt(qz, k2), 2, tq, fix, [r[pl.ds(off, tk), :] for r in v_refs], m_ref, acc_ref)

    def body(kt, carry):
        step(kt, None)
        return carry

    lax.fori_loop(0, qi, body, 0)
    step(qi, diag_mask)
    o = _flash_result(acc_ref)
    o_ref[...] = jnp.where(lane < HEAD_DIM, o[0:tq], o[tq:2 * tq])


def _fox_prompt_call(qb, foxb, cexp, ckt, n_batch, seq, tq):
    nq = seq // tq
    n_pairs = N_HEADS // 2
    qmap = lambda b, hp, q: (b * nq + q, hp)
    return pl.pallas_call(
        functools.partial(_fox_prompt_kernel, tq=tq),
        grid=(n_batch, n_pairs, nq),
        in_specs=[pl.BlockSpec((tq, LANES), qmap),
                  pl.BlockSpec((seq, LANES), lambda b, hp, q: (b, hp)),
                  pl.BlockSpec((seq, LANES), lambda b, hp, q: (b, n_pairs + hp)),
                  pl.BlockSpec((seq, LANES), lambda b, hp, q: (b, 2 * n_pairs + hp)),
                  pl.BlockSpec((tq, LANES), qmap),
                  pl.BlockSpec((1, 2, nq, tq), lambda b, hp, q: (b, hp, 0, 0))],
        out_specs=pl.BlockSpec((tq, LANES), qmap),
        out_shape=jax.ShapeDtypeStruct(qb.shape, F32),
        scratch_shapes=[pltpu.VMEM((2 * tq, LANES), F32), pltpu.VMEM((2 * tq, LANES), F32)],
        compiler_params=_params(("arbitrary", "arbitrary", "arbitrary")),
        name="fox_prompt",
    )(qb, foxb, foxb, foxb, cexp, ckt)


def _pad_rows(x, rows):
    return jnp.concatenate([x, jnp.zeros((rows - x.shape[0], x.shape[1]), x.dtype)], axis=0)


def _nsa_sample_kernel(pt_ref, qa_ref, small_ref, cmp_ref, *refs, n_pages, past, n_sel):
    del pt_ref
    page_refs = refs[:n_pages]
    (slcn_ref, winbuf_ref, winn_ref, msel_ref, eblk_ref, gexp_ref, eye_ref, oa_ref, newwin_ref) = refs[n_pages:]
    s_new = qa_ref.shape[0]
    wb = winbuf_ref.shape[1]
    kd = G_NSA * HEAD_DIM
    n_keys = past + PAGE_SIZE
    lane = lax.broadcasted_iota(jnp.int32, (s_new, LANES), 1)
    tpos = past + lax.broadcasted_iota(jnp.int32, (s_new, LANES), 0)
    n_cmp = cmp_ref.shape[0]
    cmask = (lax.broadcasted_iota(jnp.int32, (s_new, n_cmp), 1) * CMP_STRIDE + CMP_LEN - 1
             <= past + lax.broadcasted_iota(jnp.int32, (s_new, n_cmp), 0))
    kpos = lax.broadcasted_iota(jnp.int32, (s_new, n_keys), 1)
    key_ok = kpos <= past + lax.broadcasted_iota(jnp.int32, (s_new, n_keys), 0)
    w_keys = wb + PAGE_SIZE
    wi = lax.broadcasted_iota(jnp.int32, (s_new, w_keys), 1)
    wt = lax.broadcasted_iota(jnp.int32, (s_new, w_keys), 0)
    dist = wb + wt - wi
    win_ok = (dist >= 0) & (dist <= WINDOW) & (past - wb + wi >= 0) & (wi < wb + s_new)

    qa = qa_ref[...]
    ck = cmp_ref[:, 0:LANES]
    cv = cmp_ref[:, LANES:2 * LANES].astype(BF16)
    slcn = slcn_ref[...]
    winn = winn_ref[...]
    qz = [_group_queries(qa, g, lane) for g in range(G_NSA)]
    q_all = jnp.concatenate(qz, axis=0).astype(BF16)

    def attend(kt, vt, k_new, v_new, masks):
        n_past = kt.shape[1]
        s = jnp.concatenate([_dot(q_all, kt), _dot_nt(q_all, k_new)], axis=1)
        s = jnp.concatenate([jnp.where(masks[g], s[(g * R_NSA + r) * s_new:(g * R_NSA + r + 1) * s_new], NEG)
                             for g in range(G_NSA) for r in range(R_NSA)], axis=0)
        mx = jnp.max(s, axis=-1, keepdims=True)
        p = jnp.exp(s - mx)
        den = jnp.sum(p, axis=-1, keepdims=True)
        pb = p.astype(BF16)
        return (_dot_nt(pb[:, :n_past], vt) + _dot(pb[:, n_past:], v_new)) / den

    def new_rows(x):
        return _pad_rows(x, PAGE_SIZE).astype(BF16)

    o_cmp, sel_ok = [], []
    for g in range(G_NSA):
        oc, imp = _compressed_branch(qz[g], ck, cv, cmask, msel_ref[...], R_NSA, s_new)
        o_cmp.append(oc)
        sel = _select_blocks(imp, tpos, lane, n_sel).astype(BF16)
        sel_ok.append((_dot(sel, eblk_ref[...]) > 0.5) & key_ok)

    kt = jnp.concatenate([r[0:kd, :] for r in page_refs], axis=1).astype(BF16)
    vt = jnp.concatenate([r[kd:2 * kd, :] for r in page_refs], axis=1).astype(BF16)
    o_s = attend(kt, vt, new_rows(slcn[:, 0:LANES]), new_rows(slcn[:, LANES:2 * LANES]), sel_ok)
    o_w = attend(winbuf_ref[0:kd, :].astype(BF16), winbuf_ref[kd:2 * kd, :].astype(BF16),
                 new_rows(winn[:, 0:LANES]), new_rows(winn[:, LANES:2 * LANES]), [win_ok] * G_NSA)
    half = R_NSA * s_new
    o_sel = [o_s[0:half], o_s[half:2 * half]]
    o_win = [o_w[0:half], o_w[half:2 * half]]

    gates = _dot_exact_lhs(small_ref[...], gexp_ref[...])
    d_a = N_HEADS * HEAD_DIM
    oa_ref[...] = (gates[:, 0:d_a] * _assemble_heads(o_cmp, s_new, lane)
                   + gates[:, d_a:2 * d_a] * _assemble_heads(o_sel, s_new, lane)
                   + gates[:, 2 * d_a:3 * d_a] * _assemble_heads(o_win, s_new, lane))

    xn = jnp.concatenate([jnp.zeros((LANES - s_new, 2 * LANES), F32), winn], axis=0)
    a, b, c = _split3(xn)
    eye = eye_ref[...]
    new_t = _dot_nt(eye, a) + _dot_nt(eye, b) + _dot_nt(eye, c)
    n_chunks = wb // LANES
    shifted = [pltpu.roll(winbuf_ref[:, c * LANES:(c + 1) * LANES], LANES - s_new, 1) for c in range(n_chunks)]
    shifted.append(new_t)
    lane_w = lax.broadcasted_iota(jnp.int32, (2 * kd, LANES), 1)
    for c in range(n_chunks):
        newwin_ref[:, c * LANES:(c + 1) * LANES] = jnp.where(lane_w < LANES - s_new, shifted[c], shifted[c + 1])


def _nsa_sample_call(pt_flat, qa, small, cmpkv, pool_t, slcn, winbuf_t, winn, msel, eblk, gexp, eye,
                     n_batch, s_new, pages_per_seq):
    past = pages_per_seq * PAGE_SIZE
    n_sel = -(-(past + s_new) // SEL_BLOCK)
    n_cmp_rows = cmpkv.shape[0] // n_batch
    kv_rows = 2 * G_NSA * HEAD_DIM
    wb = winbuf_t.shape[1]
    row = lambda b, pt: (b, 0)
    fixed = lambda b, pt: (0, 0)

    def page_spec(j):
        return pl.BlockSpec((kv_rows, PAGE_SIZE), lambda b, pt: (pt[b * pages_per_seq + j], 0))

    grid_spec = pltpu.PrefetchScalarGridSpec(
        num_scalar_prefetch=1,
        grid=(n_batch,),
        in_specs=[pl.BlockSpec((s_new, qa.shape[1]), row), pl.BlockSpec((s_new, LANES), row),
                  pl.BlockSpec((n_cmp_rows, 2 * LANES), row)]
                 + [page_spec(j) for j in range(pages_per_seq)]
                 + [pl.BlockSpec((s_new, 2 * LANES), row), pl.BlockSpec((kv_rows, wb), row),
                    pl.BlockSpec((s_new, 2 * LANES), row),
                    pl.BlockSpec(msel.shape, fixed), pl.BlockSpec(eblk.shape, fixed), pl.BlockSpec(gexp.shape, fixed),
                    pl.BlockSpec(eye.shape, fixed)],
        out_specs=[pl.BlockSpec((s_new, qa.shape[1]), row), pl.BlockSpec((kv_rows, wb), row)],
    )
    return pl.pallas_call(
        functools.partial(_nsa_sample_kernel, n_pages=pages_per_seq, past=past, n_sel=n_sel),
        grid_spec=grid_spec,
        out_shape=[jax.ShapeDtypeStruct(qa.shape, F32), jax.ShapeDtypeStruct(winbuf_t.shape, F32)],
        compiler_params=_params(("arbitrary",)),
        name="nsa_sample",
    )(pt_flat, qa, small, cmpkv, *([pool_t] * pages_per_seq), slcn, winbuf_t, winn, msel, eblk, gexp, eye)


def _fox_sample_kernel(pt_ref, q_ref, small_ref, foxn_ref, *refs, n_pages, past):
    del pt_ref
    page_refs = refs[:n_pages]
    logf_refs = refs[n_pages:2 * n_pages]
    psel_ref, triu_ref, eye_ref, o_ref = refs[2 * n_pages:]
    s_new = q_ref.shape[0]
    n_keys = past + PAGE_SIZE
    key_ok = (lax.broadcasted_iota(jnp.int32, (s_new, n_keys), 1)
              <= past + lax.broadcasted_iota(jnp.int32, (s_new, n_keys), 0))

    small_pad = _pad_rows(small_ref[...], LANES)
    a, b, c = _split3(small_pad)
    psel = psel_ref[...]
    new_t = _dot_nt(psel, a) + _dot_nt(psel, b) + _dot_nt(psel, c)
    w0 = jnp.concatenate([r[...] for r in logf_refs] + [new_t], axis=0)
    wc = _dot_exact_lhs(w0, triu_ref[...])
    run = jnp.zeros((N_HEADS, 1), F32)
    c_rows = []
    for j in range(n_pages + 1):
        blk = wc[j * N_HEADS:(j + 1) * N_HEADS]
        c_rows.append(blk + run)
        run = run + blk[:, LANES - 1:LANES]
    a, b, c = _split3(_pad_rows(c_rows[n_pages], LANES))
    eye = eye_ref[...]
    cq_t = _dot_nt(eye, a) + _dot_nt(eye, b) + _dot_nt(eye, c)

    q = q_ref[...]
    foxn = foxn_ref[...]
    d_k = N_HEADS * HEAD_DIM
    n_hg = 4
    w_hg = n_hg * HEAD_DIM
    lane4 = lax.broadcasted_iota(jnp.int32, (s_new, w_hg), 1)
    own = [(lane4 >= i * HEAD_DIM) & (lane4 < (i + 1) * HEAD_DIM) for i in range(n_hg)]
    for hg in range(N_HEADS // n_hg):
        cols = slice(hg * w_hg, (hg + 1) * w_hg)
        q4 = q[:, cols]
        qbd = jnp.concatenate([jnp.where(own[i], q4, 0.0) for i in range(n_hg)], axis=0).astype(BF16)
        kt = jnp.concatenate([r[hg * w_hg:(hg + 1) * w_hg, :] for r in page_refs], axis=1).astype(BF16)
        vt = jnp.concatenate([r[d_k + hg * w_hg:d_k + (hg + 1) * w_hg, :] for r in page_refs], axis=1).astype(BF16)
        k_new = _pad_rows(foxn[:, cols], PAGE_SIZE).astype(BF16)
        v_new = _pad_rows(foxn[:, d_k + hg * w_hg:d_k + (hg + 1) * w_hg], PAGE_SIZE).astype(BF16)
        s = jnp.concatenate([_dot(qbd, kt), _dot_nt(qbd, k_new)], axis=1)
        rows = []
        for i in range(n_hg):
            head = hg * n_hg + i
            ck_row = jnp.concatenate([cr[head:head + 1, :] for cr in c_rows], axis=1)
            sh = s[i * s_new:(i + 1) * s_new] + (cq_t[0:s_new, head:head + 1] - ck_row)
            rows.append(jnp.where(key_ok, sh, NEG))
        s = jnp.concatenate(rows, axis=0)
        mx = jnp.max(s, axis=-1, keepdims=True)
        p = jnp.exp(s - mx)
        den = jnp.sum(p, axis=-1, keepdims=True)
        pb = p.astype(BF16)
        o = (_dot_nt(pb[:, :past], vt) + _dot(pb[:, past:], v_new)) / den
        out = jnp.where(own[0], o[0:s_new], 0.0)
        for i in range(1, n_hg):
            out = out + jnp.where(own[i], o[i * s_new:(i + 1) * s_new], 0.0)
        o_ref[:, cols] = out


def _fox_sample_call(pt_flat, qb, small, foxn, pool_t, logf_t, psel, triu, eye, n_batch, s_new, pages_per_seq):
    past = pages_per_seq * PAGE_SIZE
    kv_rows = 2 * N_HEADS * HEAD_DIM
    row = lambda b, pt: (b, 0)
    fixed = lambda b, pt: (0, 0)

    def page_spec(j):
        return pl.BlockSpec((kv_rows, PAGE_SIZE), lambda b, pt: (pt[b * pages_per_seq + j], 0))

    def logf_spec(j):
        return pl.BlockSpec((None,) + logf_t.shape[1:], lambda b, pt: (pt[b * pages_per_seq + j], 0, 0))

    grid_spec = pltpu.PrefetchScalarGridSpec(
        num_scalar_prefetch=1,
        grid=(n_batch,),
        in_specs=[pl.BlockSpec((s_new, qb.shape[1]), row), pl.BlockSpec((s_new, LANES), row),
                  pl.BlockSpec((s_new, foxn.shape[1]), row)]
                 + [page_spec(j) for j in range(pages_per_seq)]
                 + [logf_spec(j) for j in range(pages_per_seq)]
                 + [pl.BlockSpec(psel.shape, fixed), pl.BlockSpec(triu.shape, fixed), pl.BlockSpec(eye.shape, fixed)],
        out_specs=pl.BlockSpec((s_new, qb.shape[1]), row),
    )
    return pl.pallas_call(
        functools.partial(_fox_sample_kernel, n_pages=pages_per_seq, past=past),
        grid_spec=grid_spec,
        out_shape=jax.ShapeDtypeStruct(qb.shape, F32),
        compiler_params=_params(("arbitrary",)),
        name="fox_sample",
    )(pt_flat, qb, small, foxn, *([pool_t] * pages_per_seq), *([logf_t] * pages_per_seq), psel, triu, eye)


def _merge_kernel(x_ref, oa_ref, ob_ref, z_ref, w_ref, y_ref):
    d_a = oa_ref.shape[1]
    z = z_ref[...]
    gate = z * _sigmoid(z)
    u = jnp.concatenate([oa_ref[...] * gate[:, 0:d_a], ob_ref[...] * gate[:, d_a:]], axis=1).astype(BF16)
    y_ref[...] = x_ref[...] + _dot(u, w_ref[...])


def _merge_call(x2d, oa, ob, z, w_out_bf, tm, name):
    n, d = x2d.shape
    row = lambda i: (i, 0)
    return pl.pallas_call(
        _merge_kernel,
        grid=(n // tm,),
        in_specs=[pl.BlockSpec((tm, d), row), pl.BlockSpec((tm, oa.shape[1]), row), pl.BlockSpec((tm, ob.shape[1]), row),
                  pl.BlockSpec((tm, z.shape[1]), row), pl.BlockSpec(w_out_bf.shape, lambda i: (0, 0))],
        out_specs=pl.BlockSpec((tm, d), row),
        out_shape=jax.ShapeDtypeStruct((n, d), F32),
        compiler_params=_params(("arbitrary",)),
        name=name,
    )(x2d, oa, ob, z, w_out_bf)


def _row_tile(n, preferred):
    return preferred if n % preferred == 0 else n


def _rope_tables(pos):
    half = HEAD_DIM // 2
    inv = jnp.power(jnp.float32(ROPE_THETA), -jnp.arange(half, dtype=F32) / half)
    ang = pos.astype(F32)[:, None] * inv[None, :]
    cos = jnp.cos(ang)
    sin = jnp.sin(ang)
    return jnp.tile(cos, (1, 4)), jnp.tile(jnp.concatenate([-sin, sin], axis=1), (1, 2))


def _cmp_to_sel(n_cmp, n_sel, rows, cols):
    cs = np.arange(n_cmp) * CMP_STRIDE
    ss = np.arange(n_sel) * SEL_BLOCK
    ov = np.clip(np.minimum(cs[:, None] + CMP_LEN, ss[None, :] + SEL_BLOCK) - np.maximum(cs[:, None], ss[None, :]), 0, None)
    m = np.zeros((rows, cols), np.float32)
    m[:n_cmp, :n_sel] = ov / CMP_LEN
    return jnp.asarray(m, dtype=BF16)


def _block_expand(n_keys):
    e = np.zeros((LANES, n_keys), np.float32)
    k = np.arange(n_keys)
    e[k // SEL_BLOCK, k] = 1.0
    return jnp.asarray(e, dtype=BF16)


def _gate_expand():
    e = np.zeros((LANES, 3 * N_HEADS * HEAD_DIM), np.float32)
    for i in range(3):
        for h in range(N_HEADS):
            e[i * N_HEADS + h, i * N_HEADS * HEAD_DIM + h * HEAD_DIM:i * N_HEADS * HEAD_DIM + (h + 1) * HEAD_DIM] = 1.0
    return jnp.asarray(e, dtype=BF16)


def _head_expand():
    e = np.zeros((LANES, N_HEADS * HEAD_DIM), np.float32)
    for h in range(N_HEADS):
        e[LOGF_LANE + h, h * HEAD_DIM:(h + 1) * HEAD_DIM] = 1.0
    return jnp.asarray(e, dtype=BF16)


def _compress_weights(w1, w2, pe):
    half = CMP_LEN // 2
    eye = jnp.eye(G_NSA, dtype=F32)
    w = w1.reshape(2, half, HEAD_DIM, CMP_HID)
    w1x = jnp.einsum('pldh,gk->lgdpkh', w, eye).reshape(half * G_NSA * HEAD_DIM, 2 * G_NSA * CMP_HID).astype(BF16)
    w2x = jnp.einsum('hd,gk->ghkd', w2, eye).reshape(G_NSA * CMP_HID, G_NSA * HEAD_DIM).astype(BF16)
    pe2 = jnp.broadcast_to(pe.reshape(2, half, 1, HEAD_DIM), (2, half, G_NSA, HEAD_DIM)).reshape(2, -1)
    pex = jnp.concatenate([pe2, jnp.zeros((14, pe2.shape[1]), F32)], axis=0).astype(BF16)
    return w1x, w2x, pex


def _stored_tiles(cache):
    return jnp.transpose(cache, (0, 2, 3, 4, 1)).reshape(-1, cache.shape[1])


def kernel(x_prompt, x_sample, cache_nsa_cmp_kv, cache_nsa_slc_kv, cache_nsa_win_kv, cache_fox_kv, cache_fox_logf,
           page_table, g_norm, w_in, b_f, gq_a, gk_cmp, gk_slc, gk_win, pe_cmp_k, pe_cmp_v,
           w_cmp1_k, w_cmp2_k, w_cmp1_v, w_cmp2_v, gq_b, gk_b, w_out):
    depth = g_norm.shape[0]
    assert depth == 1, "single-layer stack"
    n_b, seq, d_model = x_prompt.shape
    n_db, s_new, _ = x_sample.shape
    pages_per_seq = page_table.shape[1]
    past = pages_per_seq * PAGE_SIZE
    wb = cache_nsa_win_kv.shape[2]
    n_pool = cache_nsa_cmp_kv.shape[1]
    d_a = N_HEADS * HEAD_DIM
    kv_w = 2 * G_NSA * HEAD_DIM

    w = w_in[0]
    o_kc, o_ga, o_za, o_qb, o_kb, o_fb, o_zb = 512, 1280, 1304, 1816, 2328, 3352, 3360
    gate_cols = o_ga + (np.arange(N_HEADS)[None, :] * 3 + np.arange(3)[:, None]).reshape(-1)
    small_w = jnp.concatenate([w[:, gate_cols], w[:, o_fb:o_fb + N_HEADS],
                               jnp.zeros((d_model, LANES - GATE_LANES - N_HEADS), F32)], axis=1)
    w_bf = jnp.concatenate([w[:, 0:o_ga], w[:, o_za:o_qb], w[:, o_zb:o_zb + d_a], w[:, o_qb:o_fb], small_w],
                           axis=1).astype(BF16)
    assert w_bf.shape[1] == C_END
    gains = jnp.concatenate([jnp.tile(v[0], 2)[None, :] for v in (gq_a, gk_slc, gk_win, gq_b, gk_b)]
                            + [jnp.zeros((3, LANES), F32)], axis=0)
    bf_row = jnp.zeros((1, LANES), F32).at[0, LOGF_LANE:LOGF_LANE + N_HEADS].set(b_f[0])
    lane_head = np.arange(LANES) // HEAD_DIM
    ones_bd = jnp.asarray(lane_head[:, None] == lane_head[None, :], dtype=BF16)
    hexp = _head_expand()
    gexp = _gate_expand()
    consts = lambda tm: (g_norm, w_bf, gains, bf_row, ones_bd,
                         jnp.asarray(np.tril(np.ones((tm, tm), np.float32)), dtype=BF16), hexp)

    tm_p = 256
    cos_p, sin_p = _rope_tables(jnp.arange(seq, dtype=jnp.int32))
    tiles_per_seq = seq // tm_p
    (cmp_p, slc_p, win_p, fox_p, small_p, z_p, qa_p, qb_p, cexp_p, csum_p, slcb_p, winb_p, foxb_p) = _proj_call(
        x_prompt.reshape(n_b * seq, d_model), (cos_p, sin_p, lambda i: (i % tiles_per_seq, 0)),
        consts(tm_p), tiles_per_seq, tm_p, "proj_prompt")
    tm_s = _row_tile(n_db * s_new, 256)
    cos_s, sin_s = _rope_tables(past + jnp.arange(s_new, dtype=jnp.int32))
    cos_s, sin_s = jnp.tile(cos_s, (tm_s // s_new, 1)), jnp.tile(sin_s, (tm_s // s_new, 1))
    (cmp_s, slc_s, win_s, fox_s, small_s, z_s, qa_s, qb_s, _, _, _, _, _) = _proj_call(
        x_sample.reshape(n_db * s_new, d_model), (cos_s, sin_s, lambda i: (0, 0)),
        consts(tm_s), 1, tm_s, "proj_sample")

    w1k, w2k, pek = _compress_weights(w_cmp1_k[0], w_cmp2_k[0], pe_cmp_k[0])
    w1v, w2v, pev = _compress_weights(w_cmp1_v[0], w_cmp2_v[0], pe_cmp_v[0])
    cw = (w1k, w1v, pek, pev, w2k, w2v, jnp.tile(gk_cmp[0], 2)[None, :], ones_bd)
    chunk_w = CMP_STRIDE * kv_w
    ckv_p = _compress_prompt_call(cmp_p.reshape(n_b * seq // CMP_STRIDE, chunk_w), cw, seq // CMP_STRIDE)
    pt_flat = page_table.reshape(-1).astype(jnp.int32)
    cmp_pool = cache_nsa_cmp_kv[0].reshape(n_pool, PAGE_SIZE // CMP_STRIDE, chunk_w)
    ckv_s = _compress_sample_call(pt_flat, cmp_pool, cw, n_db, pages_per_seq, 2)

    n_cmp_p = (seq - CMP_LEN) // CMP_STRIDE + 1
    msel_p = _cmp_to_sel(n_cmp_p, seq // SEL_BLOCK, seq // CMP_STRIDE, LANES)
    tq_a = 256
    eblk_p = jnp.transpose(_block_expand(seq).reshape(LANES, seq // tq_a, tq_a), (1, 0, 2))
    oa_p = _nsa_prompt_call(qa_p, small_p, ckv_p, slcb_p, winb_p, msel_p, eblk_p, gexp, n_b, seq, tq_a)
    n_cmp_s = (past + s_new - CMP_LEN) // CMP_STRIDE + 1
    n_sel_s = -(-(past + s_new) // SEL_BLOCK)
    msel_s = _cmp_to_sel(n_cmp_s, n_sel_s, past // CMP_STRIDE, LANES)
    eye_kv = jnp.asarray(np.eye(kv_w, dtype=np.float32), dtype=BF16)
    oa_s, new_win_t = _nsa_sample_call(
        pt_flat, qa_s, small_s, ckv_s, _stored_tiles(cache_nsa_slc_kv[0]), slc_s,
        _stored_tiles(cache_nsa_win_kv[0]), win_s, msel_s, _block_expand(past + PAGE_SIZE), gexp, eye_kv,
        n_db, s_new, pages_per_seq)
    new_win = jnp.transpose(new_win_t.reshape(n_db, 2, G_NSA, HEAD_DIM, wb), (0, 4, 1, 2, 3))[None]

    tq_f = 512
    c_p = csum_p[:, LOGF_LANE:LOGF_LANE + N_HEADS].reshape(n_b, seq, N_HEADS)
    ckt = jnp.transpose(c_p, (0, 2, 1)).reshape(n_b, N_HEADS, seq // tq_f, tq_f)
    ob_p = _fox_prompt_call(qb_p, foxb_p, cexp_p, ckt, n_b, seq, tq_f)
    psel = jnp.asarray(np.arange(LANES)[None, :] == (LOGF_LANE + np.arange(2 * N_HEADS))[:, None], dtype=BF16)
    psel = psel.at[N_HEADS:].set(0)
    triu = jnp.asarray(np.triu(np.ones((LANES, LANES), np.float32)), dtype=BF16)
    eye = jnp.asarray(np.eye(LANES, dtype=np.float32), dtype=BF16)
    ob_s = _fox_sample_call(
        pt_flat, qb_s, small_s, fox_s, _stored_tiles(cache_fox_kv[0]),
        jnp.transpose(cache_fox_logf[0], (0, 2, 1)), psel, triu, eye, n_db, s_new, pages_per_seq)

    w_out_bf = w_out[0].astype(BF16)
    y_p = _merge_call(x_prompt.reshape(n_b * seq, d_model), oa_p, ob_p, z_p, w_out_bf, 512, "merge_prompt")
    y_s = _merge_call(x_sample.reshape(n_db * s_new, d_model), oa_s, ob_s, z_s, w_out_bf,
                      _row_tile(n_db * s_new, 512), "merge_sample")

    kv5 = lambda a, nb, t, heads: a.reshape(1, nb, t, 2, heads, HEAD_DIM)
    wbp = min(WINDOW, seq)
    return (y_p.reshape(n_b, seq, d_model), y_s.reshape(n_db, s_new, d_model),
            kv5(cmp_p, n_b, seq, G_NSA), kv5(cmp_s, n_db, s_new, G_NSA),
            kv5(slc_p, n_b, seq, G_NSA), kv5(slc_s, n_db, s_new, G_NSA),
            kv5(win_p.reshape(n_b, seq, kv_w)[:, seq - wbp:], n_b, wbp, G_NSA), new_win,
            kv5(fox_p, n_b, seq, N_HEADS), kv5(fox_s, n_db, s_new, N_HEADS),
            small_p[:, LOGF_LANE:LOGF_LANE + N_HEADS].reshape(1, n_b, seq, N_HEADS),
            small_s[:, LOGF_LANE:LOGF_LANE + N_HEADS].reshape(1, n_db, s_new, N_HEADS))
```

```python
import functools

import numpy as np
import jax
import jax.numpy as jnp
from jax import lax
from jax.experimental import pallas as pl
from jax.experimental.pallas import tpu as pltpu

F32 = jnp.float32
BF16 = jnp.bfloat16

HEAD_DIM = 64
LANES = 128
N_HEADS = 8
G_NSA = 2
R_NSA = N_HEADS // G_NSA
CMP_LEN = 32
CMP_STRIDE = 16
CMP_HID = 2 * HEAD_DIM
SEL_BLOCK = 64
SEL_TOP = 16
WINDOW = 512
PAGE_SIZE = 128
ROPE_THETA = 10000.0
EPS = 1e-6
NEG = -1e30
FORCED = 1e4
QK_SCALE = HEAD_DIM ** -0.5

VMEM_LIMIT = 56 * 1024 * 1024

C_QA, C_CMP, C_SLC, C_WIN, C_Z, C_QB, C_FOX, C_SMALL, C_END = 0, 512, 768, 1024, 1280, 2304, 2816, 3840, 3968
GATE_LANES = 3 * N_HEADS
LOGF_LANE = GATE_LANES


def _dot(a, b):
    return jnp.dot(a, b, preferred_element_type=F32)


def _dot_nt(a, b):
    return lax.dot_general(a, b, (((1,), (1,)), ((), ())), preferred_element_type=F32)


def _split2(x):
    hi = x.astype(BF16)
    lo = (x - hi.astype(F32)).astype(BF16)
    return hi, lo


def _split3(x):
    hi = x.astype(BF16)
    r = x - hi.astype(F32)
    mid = r.astype(BF16)
    lo = (r - mid.astype(F32)).astype(BF16)
    return hi, mid, lo


def _dot_exact_lhs(x, m):
    a, b, c = _split3(x)
    return _dot(a, m) + _dot(b, m) + _dot(c, m)


def _dot_exact_rhs(m, x):
    a, b, c = _split3(x)
    return _dot(m, a) + _dot(m, b) + _dot(m, c)


def _sigmoid(x):
    return 1.0 / (1.0 + jnp.exp(-x))


def _params(sem):
    return pltpu.CompilerParams(dimension_semantics=sem, vmem_limit_bytes=VMEM_LIMIT)


def _proj_kernel(x_ref, g_ref, w_ref, cos_ref, sin_ref, gains_ref, bf_ref, ones_ref, tri_ref, hexp_ref,
                 cmp_ref, slc_ref, win_ref, fox_ref, small_ref, z_ref, qa_ref, qb_ref, cexp_ref, csum_ref, cmpd_ref,
                 slcb_ref, winb_ref, foxb_ref, carry_ref, *, tm, tiles_per_seq, kv_t):
    i = pl.program_id(0)
    x = x_ref[...]
    ms = jnp.mean(x * x, axis=-1, keepdims=True)
    h = (x * lax.rsqrt(ms + EPS) * g_ref[...]).astype(BF16)
    cos = cos_ref[...]
    sin = sin_ref[...]
    lane = lax.broadcasted_iota(jnp.int32, (tm, LANES), 1)
    first_half = (lane & (HEAD_DIM // 2)) == 0
    ones = ones_ref[...]

    def proj(c0, c1):
        return _dot(h, w_ref[:, c0:c1])

    def head_norm(y, row):
        hi, lo = _split2(y * y)
        ss = _dot(hi, ones) + _dot(lo, ones)
        return y * lax.rsqrt(ss * (1.0 / HEAD_DIM) + EPS) * gains_ref[row:row + 1, :]

    def rope(y):
        partner = jnp.where(first_half, pltpu.roll(y, LANES - HEAD_DIM // 2, 1), pltpu.roll(y, HEAD_DIM // 2, 1))
        return y * cos + partner * sin

    def chunk(y, c):
        return y[:, c * LANES:(c + 1) * LANES]

    def with_ones(v, e):
        return jnp.where((lane < HEAD_DIM) if e == 0 else (lane >= HEAD_DIM), v, 1.0)

    y = proj(C_QA, C_CMP)
    for c in range(4):
        qa_ref[:, c * LANES:(c + 1) * LANES] = rope(head_norm(chunk(y, c), 0)) * QK_SCALE

    def store_kv(o_ref, t, c):
        if kv_t:
            o_ref[c * LANES:(c + 1) * LANES, :] = t.T
        else:
            o_ref[:, c * LANES:(c + 1) * LANES] = t

    y = proj(C_CMP, C_SLC)
    k = rope(chunk(y, 0))
    v = chunk(y, 1)
    cmpd_ref[:, 0:LANES] = k
    cmpd_ref[:, LANES:2 * LANES] = v
    store_kv(cmp_ref, k, 0)
    store_kv(cmp_ref, v, 1)

    for (c0, c1, row, o_ref, ob_ref) in ((C_SLC, C_WIN, 1, slc_ref, slcb_ref), (C_WIN, C_Z, 2, win_ref, winb_ref)):
        y = proj(c0, c1)
        k = rope(head_norm(chunk(y, 0), row))
        v = chunk(y, 1)
        store_kv(o_ref, k, 0)
        store_kv(o_ref, v, 1)
        ob_ref[:, 0:LANES] = k.astype(BF16)
        for e in range(2):
            ob_ref[:, (1 + e) * LANES:(2 + e) * LANES] = with_ones(v, e).astype(BF16)

    z_ref[...] = proj(C_Z, C_QB)

    y = proj(C_QB, C_FOX)
    for c in range(4):
        qb_ref[:, c * LANES:(c + 1) * LANES] = head_norm(chunk(y, c), 3) * QK_SCALE

    y = proj(C_FOX, C_SMALL)
    for c in range(8):
        t = chunk(y, c)
        if c < 4:
            t = head_norm(t, 4)
        store_kv(fox_ref, t, c)
        if c < 4:
            foxb_ref[:, c * LANES:(c + 1) * LANES] = t.astype(BF16)
        else:
            for e in range(2):
                foxb_ref[:, (c + 4 * e) * LANES:(c + 4 * e + 1) * LANES] = with_ones(t, e).astype(BF16)

    raw = proj(C_SMALL, C_END)
    zf = raw + bf_ref[...]
    logf = jnp.minimum(zf, 0.0) - jnp.log1p(jnp.exp(-jnp.abs(zf)))
    small = jnp.where(lane < GATE_LANES, _sigmoid(raw), jnp.where(lane < GATE_LANES + N_HEADS, logf, 0.0))
    small_ref[...] = small

    @pl.when(i % tiles_per_seq == 0)
    def _():
        carry_ref[...] = jnp.zeros_like(carry_ref)

    c = carry_ref[0:1, :] + _dot_exact_rhs(tri_ref[...], small)
    carry_ref[0:1, :] = c[tm - 1:tm, :]
    csum_ref[...] = c
    cexp_ref[...] = _dot_exact_lhs(c, hexp_ref[...])


def _proj_call(x2d, tables, consts, tiles_per_seq, tm, name, kv_t):
    n = x2d.shape[0]
    cos_t, sin_t, table_map = tables
    g_norm, w_bf, gains, bf_row, ones_bd, tri, hexp = consts
    row = lambda i: (i, 0)
    fixed = lambda i: (0, 0)
    full = lambda a: pl.BlockSpec(a.shape, fixed)
    kv_widths = (256, 256, 256, 1024)
    out_widths = (128, 1024, 512, 512, 512, 128, 256)
    bf_widths = (3 * LANES, 3 * LANES, 3 * N_HEADS * HEAD_DIM)
    if kv_t:
        seq = tiles_per_seq * tm
        out_shape = [jax.ShapeDtypeStruct((n // seq * w, seq), F32) for w in kv_widths]
        out_specs = [pl.BlockSpec((w, tm), lambda i: (i // tiles_per_seq, i % tiles_per_seq)) for w in kv_widths]
    else:
        out_shape = [jax.ShapeDtypeStruct((n, w), F32) for w in kv_widths]
        out_specs = [pl.BlockSpec((tm, w), row) for w in kv_widths]
    out_shape += [jax.ShapeDtypeStruct((n, w), F32) for w in out_widths]
    out_shape += [jax.ShapeDtypeStruct((n, w), BF16) for w in bf_widths]
    out_specs += [pl.BlockSpec((tm, w), row) for w in out_widths + bf_widths]
    return pl.pallas_call(
        functools.partial(_proj_kernel, tm=tm, tiles_per_seq=tiles_per_seq, kv_t=kv_t),
        grid=(n // tm,),
        in_specs=[pl.BlockSpec((tm, x2d.shape[1]), row), full(g_norm), full(w_bf),
                  pl.BlockSpec((tm, LANES), table_map), pl.BlockSpec((tm, LANES), table_map),
                  full(gains), full(bf_row), full(ones_bd), full(tri), full(hexp)],
        out_specs=out_specs,
        out_shape=out_shape,
        scratch_shapes=[pltpu.VMEM((8, LANES), F32)],
        compiler_params=_params(("arbitrary",)),
        name=name,
    )(x2d, g_norm, w_bf, cos_t, sin_t, gains, bf_row, ones_bd, tri, hexp)


def _compress_body(chunk_rows, m, wk_ref, wv_ref, pek_ref, pev_ref, w2k_ref, w2v_ref, gain_ref, ones_ref, out_ref,
                   rows_per_seq):
    hid = G_NSA * CMP_HID

    def branch(kind, w1_ref, pe_ref, w2_ref):
        r = _dot(jnp.concatenate([chunk_rows(kind), pe_ref[...]], axis=0), w1_ref[...])
        hh = r[:m, :hid] + pltpu.roll(r[:m, hid:], m - 1, 0) + r[m:m + 1, :hid] + r[m + 1:m + 2, hid:]
        act = hh * _sigmoid(hh)
        return _dot(act.astype(BF16), w2_ref[...])

    kc = branch(0, wk_ref, pek_ref, w2k_ref)
    hi, lo = _split2(kc * kc)
    ss = _dot(hi, ones_ref[...]) + _dot(lo, ones_ref[...])
    kc = kc * lax.rsqrt(ss * (1.0 / HEAD_DIM) + EPS) * gain_ref[...]
    vc = branch(1, wv_ref, pev_ref, w2v_ref)
    rows = lax.broadcasted_iota(jnp.int32, (m, LANES), 0)
    valid = (rows % rows_per_seq) != rows_per_seq - 1
    out_ref[:, 0:LANES] = jnp.where(valid, kc, 0.0)
    out_ref[:, LANES:2 * LANES] = jnp.where(valid, vc, 0.0)


def _compress_prompt_kernel(y_ref, *rest, rows_per_seq):
    y = y_ref[...].astype(BF16)
    width = 2 * G_NSA * HEAD_DIM

    def chunk_rows(kind):
        return jnp.concatenate([y[:, l * width + kind * LANES:l * width + (kind + 1) * LANES]
                                for l in range(CMP_STRIDE)], axis=1)

    _compress_body(chunk_rows, y.shape[0], *rest, rows_per_seq=rows_per_seq)


def _compress_sample_kernel(pt_ref, *refs, n_pages, rows_per_seq):
    del pt_ref
    page_refs = refs[:n_pages]
    perm_ref = refs[n_pages]
    chunks = PAGE_SIZE // CMP_STRIDE

    def chunk_rows(kind):
        tiles = [_dot_nt(perm_ref[...], r[kind * LANES:(kind + 1) * LANES, :].astype(BF16)) for r in page_refs]
        return jnp.concatenate([jnp.concatenate([t[l * chunks:(l + 1) * chunks] for t in tiles], axis=0)
                                for l in range(CMP_STRIDE)], axis=1).astype(BF16)

    _compress_body(chunk_rows, n_pages * chunks, *refs[n_pages + 1:], rows_per_seq=rows_per_seq)


def _compress_prompt_call(y2d, cw, rows_per_seq):
    n = y2d.shape[0]
    fixed = lambda i: (0, 0)
    return pl.pallas_call(
        functools.partial(_compress_prompt_kernel, rows_per_seq=rows_per_seq),
        grid=(n // rows_per_seq,),
        in_specs=[pl.BlockSpec((rows_per_seq, y2d.shape[1]), lambda i: (i, 0))] + [pl.BlockSpec(a.shape, fixed) for a in cw],
        out_specs=pl.BlockSpec((rows_per_seq, 2 * LANES), lambda i: (i, 0)),
        out_shape=jax.ShapeDtypeStruct((n, 2 * LANES), F32),
        compiler_params=_params(("arbitrary",)),
        name="compress_prompt",
    )(y2d, *cw)


def _compress_sample_call(pt_flat, pool_t, perm, cw, n_batch, pages_per_seq, batch_per_step):
    kv_rows = 2 * G_NSA * HEAD_DIM
    rows_per_seq = pages_per_seq * (PAGE_SIZE // CMP_STRIDE)
    n_pages = batch_per_step * pages_per_seq
    fixed = lambda i, pt: (0, 0)

    def page_spec(k):
        bb, j = divmod(k, pages_per_seq)
        return pl.BlockSpec((kv_rows, PAGE_SIZE), lambda i, pt: (pt[(i * batch_per_step + bb) * pages_per_seq + j], 0))

    m = batch_per_step * rows_per_seq
    grid_spec = pltpu.PrefetchScalarGridSpec(
        num_scalar_prefetch=1,
        grid=(n_batch // batch_per_step,),
        in_specs=[page_spec(k) for k in range(n_pages)] + [pl.BlockSpec(a.shape, fixed) for a in (perm,) + tuple(cw)],
        out_specs=pl.BlockSpec((m, 2 * LANES), lambda i, pt: (i, 0)),
    )
    return pl.pallas_call(
        functools.partial(_compress_sample_kernel, n_pages=n_pages, rows_per_seq=rows_per_seq),
        grid_spec=grid_spec,
        out_shape=jax.ShapeDtypeStruct((n_batch * rows_per_seq, 2 * LANES), F32),
        compiler_params=_params(("arbitrary",)),
        name="compress_sample",
    )(pt_flat, *([pool_t] * n_pages), perm, *cw)


def _group_queries(qa, g, lane):
    in_group = (lane < HEAD_DIM) if g == 0 else (lane >= HEAD_DIM)
    rows = []
    for r in range(R_NSA):
        head = R_NSA * g + r
        y = qa[:, (head // 2) * LANES:(head // 2 + 1) * LANES]
        if head % 2 != g:
            y = pltpu.roll(y, HEAD_DIM, 1)
        rows.append(jnp.where(in_group, y, 0.0))
    return jnp.concatenate(rows, axis=0)


def _assemble_heads(res, tq, lane):
    chunks = []
    for c in range(N_HEADS // 2):
        g = c // 2
        a = res[g][(2 * (c % 2)) * tq:(2 * (c % 2) + 1) * tq]
        b = res[g][(2 * (c % 2) + 1) * tq:(2 * (c % 2) + 2) * tq]
        if g == 0:
            b = pltpu.roll(b, HEAD_DIM, 1)
        else:
            a = pltpu.roll(a, HEAD_DIM, 1)
        chunks.append(jnp.where(lane < HEAD_DIM, a, b))
    return jnp.concatenate(chunks, axis=1)


def _flash_step(s, n_rb, rb, fix, values, m_ref, acc_ref):
    n_c = s.shape[1] // LANES
    cols = [jnp.concatenate([fix(r, c, s[r * rb:(r + 1) * rb, c * LANES:(c + 1) * LANES]) for r in range(n_rb)], axis=0)
            for c in range(n_c)]
    mx = cols[0]
    for c in range(1, n_c):
        mx = jnp.maximum(mx, cols[c])
    m_prev = m_ref[...]
    m_new = jnp.maximum(m_prev, jnp.max(mx, axis=-1, keepdims=True))
    alpha = jnp.exp(m_prev - m_new)
    p = jnp.concatenate([jnp.exp(x - m_new) for x in cols], axis=1).astype(BF16)
    if len(values) == 1:
        pv = _dot(p, values[0])
    else:
        pv = jnp.concatenate([_dot(p[r * rb:(r + 1) * rb], values[r]) for r in range(n_rb)], axis=0)
    acc_ref[...] = alpha * acc_ref[...] + pv
    m_ref[...] = m_new


def _flash_init(m_ref, acc_ref):
    m_ref[...] = jnp.full(m_ref.shape, NEG, F32)
    acc_ref[...] = jnp.zeros(acc_ref.shape, F32)


def _flash_result(acc_ref):
    acc = acc_ref[...]
    return acc / pltpu.roll(acc, HEAD_DIM, 1)


def _block_ranks(impm, n_blocks, lane):
    cnt = jnp.zeros(impm.shape, F32)
    for j in range(n_blocks):
        col = impm[:, j:j + 1]
        ge = jnp.where(col >= impm, 1.0, 0.0)
        gt = jnp.where(col > impm, 1.0, 0.0)
        cnt = cnt + jnp.where(lane > j, ge, gt)
    return cnt


def _compressed_branch(qz, ck, cv, cmask, msel, n_heads, tq):
    q_hi, q_lo = _split2(qz)
    k_hi, k_lo = _split2(ck)
    s = _dot_nt(q_hi, k_hi) + _dot_nt(q_hi, k_lo) + _dot_nt(q_lo, k_hi)
    ps = []
    psum = None
    for r in range(n_heads):
        sr = jnp.where(cmask, s[r * tq:(r + 1) * tq], NEG)
        mx = jnp.max(sr, axis=-1, keepdims=True)
        e = jnp.where(cmask, jnp.exp(sr - mx), 0.0)
        den = jnp.sum(e, axis=-1, keepdims=True)
        p = e / jnp.where(den > 0.0, den, 1.0)
        ps.append(p)
        psum = p if psum is None else psum + p
    o = _dot(jnp.concatenate(ps, axis=0).astype(BF16), cv)
    return o, _dot_exact_lhs(psum, msel)


def _select_blocks(imp, tpos, lane, n_blocks):
    forced = (lane == 0) | (lane == tpos // SEL_BLOCK)
    causal = lane * SEL_BLOCK <= tpos
    impm = jnp.where(forced, FORCED, jnp.where(causal, imp, -1.0))
    impm = jnp.where(lane < n_blocks, impm, -3e38)
    cnt = _block_ranks(impm, n_blocks, lane)
    n_top = min(SEL_TOP, n_blocks)
    return jnp.where(cnt < n_top, jnp.where(lane < n_blocks, 1.0, 0.0), 0.0)


def _select_blocks_t(imp, s0, n_blocks, cnt_ref):
    tq = imp.shape[0]
    grp = 8
    imp_t = jnp.transpose(imp)[0:n_blocks]
    blk = lax.broadcasted_iota(jnp.int32, (n_blocks, tq), 0)
    tpos = s0 + lax.broadcasted_iota(jnp.int32, (n_blocks, tq), 1)
    forced = (blk == 0) | (blk == tpos // SEL_BLOCK)
    impm = jnp.where(forced, FORCED, jnp.where(blk * SEL_BLOCK <= tpos, imp_t, -1.0))
    cnt_ref[...] = jnp.zeros(cnt_ref.shape, F32)
    sub = lax.broadcasted_iota(jnp.int32, (grp, tq), 0)
    for jb in range(n_blocks // grp):
        @pl.when(jb * grp * SEL_BLOCK <= s0 + tq - 1)
        def _(jb=jb):
            for rg in range(n_blocks // grp):
                x = impm[rg * grp:(rg + 1) * grp]
                cnt = cnt_ref[rg * grp:(rg + 1) * grp, :]
                for j in range(jb * grp, (jb + 1) * grp):
                    row = impm[j:j + 1, :]
                    if rg > jb:
                        beat = row >= x
                    elif rg < jb:
                        beat = row > x
                    else:
                        cnt = cnt + jnp.where(sub > j - jb * grp, jnp.where(row >= x, 1.0, 0.0),
                                              jnp.where(row > x, 1.0, 0.0))
                        continue
                    cnt = cnt + jnp.where(beat, 1.0, 0.0)
                cnt_ref[rg * grp:(rg + 1) * grp, :] = cnt
    sel_t = jnp.where(cnt_ref[...] < min(SEL_TOP, n_blocks), 1.0, 0.0)
    return jnp.transpose(jnp.concatenate([sel_t, jnp.zeros((LANES - n_blocks, tq), F32)], axis=0))


def _nsa_prompt_kernel(qa_ref, small_ref, cmp_ref, slc_ref, win_ref, msel_ref, eblk_ref, gexp_ref,
                       oa_ref, m_ref, acc_ref, cnt_ref, *, tq, n_sel):
    tk = tq
    qi = pl.program_id(1)
    s0 = qi * tq
    lane = lax.broadcasted_iota(jnp.int32, (tq, LANES), 1)
    trow = lax.broadcasted_iota(jnp.int32, (tq, LANES), 0)
    tpos = s0 + trow
    row_t = lax.broadcasted_iota(jnp.int32, (tq, tk), 0)
    col_t = lax.broadcasted_iota(jnp.int32, (tq, tk), 1)
    diag_mask = col_t <= row_t
    far_mask = (2 * tk + row_t - col_t) <= WINDOW
    n_cmp = cmp_ref.shape[0]
    cmask = (lax.broadcasted_iota(jnp.int32, (tq, n_cmp), 1) * CMP_STRIDE + CMP_LEN - 1
             <= s0 + lax.broadcasted_iota(jnp.int32, (tq, n_cmp), 0))
    qa = qa_ref[...]
    ck = cmp_ref[:, 0:LANES]
    cv = cmp_ref[:, LANES:2 * LANES].astype(BF16)

    def chunk_of(mask, c):
        return mask[:, c * LANES:(c + 1) * LANES]

    def flash_step(qzb, kv_ref, kt, g, bias, mask):
        off = pl.multiple_of(kt * tk, tk)
        k2 = kv_ref[pl.ds(off, tk), 0:LANES]
        v2 = kv_ref[pl.ds(off, tk), (1 + g) * LANES:(2 + g) * LANES]

        def fix(r, c, x):
            if bias is not None:
                x = x + chunk_of(bias, c)
            if mask is not None:
                x = jnp.where(chunk_of(mask, c), x, NEG)
            return x

        _flash_step(_dot_nt(qzb, k2), R_NSA, tq, fix, [v2], m_ref, acc_ref)

    o_cmp, o_sel, o_win = [], [], []
    for g in range(G_NSA):
        qz = _group_queries(qa, g, lane)
        qzb = qz.astype(BF16)
        oc, imp = _compressed_branch(qz, ck, cv, cmask, msel_ref[...], R_NSA, tq)
        o_cmp.append(oc)
        sel = _select_blocks_t(imp, s0, n_sel, cnt_ref).astype(BF16)

        def sel_bias(kt, sel=sel):
            return (_dot(sel, eblk_ref[kt]) - 1.0) * (-NEG)

        _flash_init(m_ref, acc_ref)

        def sel_body(kt, carry, qzb=qzb, g=g, sel_bias=sel_bias):
            flash_step(qzb, slc_ref, kt, g, sel_bias(kt), None)
            return carry

        lax.fori_loop(0, qi, sel_body, 0)
        flash_step(qzb, slc_ref, qi, g, sel_bias(qi), diag_mask)
        o_sel.append(_flash_result(acc_ref))

        _flash_init(m_ref, acc_ref)

        @pl.when(qi >= 2)
        def _(qzb=qzb, g=g):
            flash_step(qzb, win_ref, qi - 2, g, None, far_mask)

        @pl.when(qi >= 1)
        def _(qzb=qzb, g=g):
            flash_step(qzb, win_ref, qi - 1, g, None, None)

        flash_step(qzb, win_ref, qi, g, None, diag_mask)
        o_win.append(_flash_result(acc_ref))

    gates = _dot_exact_lhs(small_ref[...], gexp_ref[...])
    d_a = N_HEADS * HEAD_DIM
    oa_ref[...] = (gates[:, 0:d_a] * _assemble_heads(o_cmp, tq, lane)
                   + gates[:, d_a:2 * d_a] * _assemble_heads(o_sel, tq, lane)
                   + gates[:, 2 * d_a:3 * d_a] * _assemble_heads(o_win, tq, lane))


def _nsa_prompt_call(qa, small, cmpkv, slcb, winb, msel, eblk, gexp, n_batch, seq, tq):
    nq = seq // tq
    n_cmp_rows = cmpkv.shape[0] // n_batch
    n_sel = seq // SEL_BLOCK
    qrow = lambda b, q: (b * nq + q, 0)
    per_b = lambda b, q: (b, 0)
    fixed = lambda b, q: (0, 0)
    return pl.pallas_call(
        functools.partial(_nsa_prompt_kernel, tq=tq, n_sel=n_sel),
        grid=(n_batch, nq),
        in_specs=[pl.BlockSpec((tq, qa.shape[1]), qrow), pl.BlockSpec((tq, LANES), qrow),
                  pl.BlockSpec((n_cmp_rows, 2 * LANES), per_b),
                  pl.BlockSpec((seq, 3 * LANES), per_b), pl.BlockSpec((seq, 3 * LANES), per_b),
                  pl.BlockSpec(msel.shape, fixed), pl.BlockSpec(eblk.shape, lambda b, q: (0, 0, 0)),
                  pl.BlockSpec(gexp.shape, fixed)],
        out_specs=pl.BlockSpec((tq, qa.shape[1]), qrow),
        out_shape=jax.ShapeDtypeStruct(qa.shape, F32),
        scratch_shapes=[pltpu.VMEM((R_NSA * tq, LANES), F32), pltpu.VMEM((R_NSA * tq, LANES), F32),
                        pltpu.VMEM((n_sel, tq), F32)],
        compiler_params=_params(("arbitrary", "arbitrary")),
        name="nsa_prompt",
    )(qa, small, cmpkv, slcb, winb, msel, eblk, gexp)


def _fox_prompt_kernel(q_ref, k_ref, v0_ref, v1_ref, cq_ref, ck_ref, o_ref, m_ref, acc_ref, *, tq):
    tk = tq
    qi = pl.program_id(2)
    lane = lax.broadcasted_iota(jnp.int32, (tq, LANES), 1)
    q = q_ref[...]
    qz = jnp.concatenate([jnp.where(lane < HEAD_DIM, q, 0.0), jnp.where(lane >= HEAD_DIM, q, 0.0)], axis=0).astype(BF16)
    cq = cq_ref[...]
    cq_sw = pltpu.roll(cq, HEAD_DIM, 1)
    cq_rep = [jnp.where(lane < HEAD_DIM, cq, cq_sw), jnp.where(lane >= HEAD_DIM, cq, cq_sw)]
    diag_mask = lax.broadcasted_iota(jnp.int32, (tq, tk), 1) <= lax.broadcasted_iota(jnp.int32, (tq, tk), 0)
    v_refs = (v0_ref, v1_ref)
    _flash_init(m_ref, acc_ref)

    def step(kt, mask):
        off = pl.multiple_of(kt * tk, tk)
        k2 = k_ref[pl.ds(off, tk), :]
        ck = [ck_ref[0, e, pl.ds(kt, 1), :] for e in range(2)]

        def fix(e, c, x):
            x = x + (cq_rep[e] - ck[e][:, c * LANES:(c + 1) * LANES])
            if mask is not None:
                x = jnp.where(mask[:, c * LANES:(c + 1) * LANES], x, NEG)
            return x

        _flash_step(_dot_nt(qz, k2), 2, tq, fix, [r[pl.ds(off, tk), :] for r in v_refs], m_ref, acc_ref)

    def body(kt, carry):
        step(kt, None)
        return carry

    lax.fori_loop(0, qi, body, 0)
    step(qi, diag_mask)
    o = _flash_result(acc_ref)
    o_ref[...] = jnp.where(lane < HEAD_DIM, o[0:tq], o[tq:2 * tq])


def _fox_prompt_call(qb, foxb, cexp, ckt, n_batch, seq, tq):
    nq = seq // tq
    n_pairs = N_HEADS // 2
    qmap = lambda b, hp, q: (b * nq + q, hp)
    return pl.pallas_call(
        functools.partial(_fox_prompt_kernel, tq=tq),
        grid=(n_batch, n_pairs, nq),
        in_specs=[pl.BlockSpec((tq, LANES), qmap),
                  pl.BlockSpec((seq, LANES), lambda b, hp, q: (b, hp)),
                  pl.BlockSpec((seq, LANES), lambda b, hp, q: (b, n_pairs + hp)),
                  pl.BlockSpec((seq, LANES), lambda b, hp, q: (b, 2 * n_pairs + hp)),
                  pl.BlockSpec((tq, LANES), qmap),
                  pl.BlockSpec((1, 2, nq, tq), lambda b, hp, q: (b, hp, 0, 0))],
        out_specs=pl.BlockSpec((tq, LANES), qmap),
        out_shape=jax.ShapeDtypeStruct(qb.shape, F32),
        scratch_shapes=[pltpu.VMEM((2 * tq, LANES), F32), pltpu.VMEM((2 * tq, LANES), F32)],
        compiler_params=_params(("arbitrary", "arbitrary", "arbitrary")),
        name="fox_prompt",
    )(qb, foxb, foxb, foxb, cexp, ckt)


def _pad_rows(x, rows):
    return jnp.concatenate([x, jnp.zeros((rows - x.shape[0], x.shape[1]), x.dtype)], axis=0)


def _nsa_sample_kernel(pt_ref, qa_ref, small_ref, cmp_ref, *refs, n_pages, past, n_sel):
    del pt_ref
    page_refs = refs[:n_pages]
    (slcn_ref, winbuf_ref, winn_ref, msel_ref, eblk_ref, gexp_ref, eye_ref, oa_ref, newwin_ref) = refs[n_pages:]
    s_new = qa_ref.shape[0]
    wb = winbuf_ref.shape[1]
    kd = G_NSA * HEAD_DIM
    n_keys = past + PAGE_SIZE
    lane = lax.broadcasted_iota(jnp.int32, (s_new, LANES), 1)
    tpos = past + lax.broadcasted_iota(jnp.int32, (s_new, LANES), 0)
    n_cmp = cmp_ref.shape[0]
    cmask = (lax.broadcasted_iota(jnp.int32, (s_new, n_cmp), 1) * CMP_STRIDE + CMP_LEN - 1
             <= past + lax.broadcasted_iota(jnp.int32, (s_new, n_cmp), 0))
    kpos = lax.broadcasted_iota(jnp.int32, (s_new, n_keys), 1)
    key_ok = kpos <= past + lax.broadcasted_iota(jnp.int32, (s_new, n_keys), 0)
    w_keys = wb + PAGE_SIZE
    wi = lax.broadcasted_iota(jnp.int32, (s_new, w_keys), 1)
    wt = lax.broadcasted_iota(jnp.int32, (s_new, w_keys), 0)
    dist = wb + wt - wi
    win_ok = (dist >= 0) & (dist <= WINDOW) & (past - wb + wi >= 0) & (wi < wb + s_new)

    qa = qa_ref[...]
    ck = cmp_ref[:, 0:LANES]
    cv = cmp_ref[:, LANES:2 * LANES].astype(BF16)
    slcn = slcn_ref[...]
    winn = winn_ref[...]
    qz = [_group_queries(qa, g, lane) for g in range(G_NSA)]
    q_all = jnp.concatenate(qz, axis=0).astype(BF16)

    def attend(kt, vt, k_new, v_new, masks):
        n_past = kt.shape[1]
        s = jnp.concatenate([_dot(q_all, kt), _dot_nt(q_all, k_new)], axis=1)
        s = jnp.concatenate([jnp.where(masks[g], s[(g * R_NSA + r) * s_new:(g * R_NSA + r + 1) * s_new], NEG)
                             for g in range(G_NSA) for r in range(R_NSA)], axis=0)
        mx = jnp.max(s, axis=-1, keepdims=True)
        p = jnp.exp(s - mx)
        den = jnp.sum(p, axis=-1, keepdims=True)
        pb = p.astype(BF16)
        return (_dot_nt(pb[:, :n_past], vt) + _dot(pb[:, n_past:], v_new)) / den

    def new_rows(x):
        return _pad_rows(x, PAGE_SIZE).astype(BF16)

    o_cmp, sel_ok = [], []
    for g in range(G_NSA):
        oc, imp = _compressed_branch(qz[g], ck, cv, cmask, msel_ref[...], R_NSA, s_new)
        o_cmp.append(oc)
        sel = _select_blocks(imp, tpos, lane, n_sel).astype(BF16)
        sel_ok.append((_dot(sel, eblk_ref[...]) > 0.5) & key_ok)

    kt = jnp.concatenate([r[0:kd, :] for r in page_refs], axis=1).astype(BF16)
    vt = jnp.concatenate([r[kd:2 * kd, :] for r in page_refs], axis=1).astype(BF16)
    o_s = attend(kt, vt, new_rows(slcn[:, 0:LANES]), new_rows(slcn[:, LANES:2 * LANES]), sel_ok)
    o_w = attend(winbuf_ref[0:kd, :].astype(BF16), winbuf_ref[kd:2 * kd, :].astype(BF16),
                 new_rows(winn[:, 0:LANES]), new_rows(winn[:, LANES:2 * LANES]), [win_ok] * G_NSA)
    half = R_NSA * s_new
    o_sel = [o_s[0:half], o_s[half:2 * half]]
    o_win = [o_w[0:half], o_w[half:2 * half]]

    gates = _dot_exact_lhs(small_ref[...], gexp_ref[...])
    d_a = N_HEADS * HEAD_DIM
    oa_ref[...] = (gates[:, 0:d_a] * _assemble_heads(o_cmp, s_new, lane)
                   + gates[:, d_a:2 * d_a] * _assemble_heads(o_sel, s_new, lane)
                   + gates[:, 2 * d_a:3 * d_a] * _assemble_heads(o_win, s_new, lane))

    xn = jnp.concatenate([jnp.zeros((LANES - s_new, 2 * LANES), F32), winn], axis=0)
    a, b, c = _split3(xn)
    eye = eye_ref[...]
    new_t = _dot_nt(eye, a) + _dot_nt(eye, b) + _dot_nt(eye, c)
    n_chunks = wb // LANES
    shifted = [pltpu.roll(winbuf_ref[:, c * LANES:(c + 1) * LANES], LANES - s_new, 1) for c in range(n_chunks)]
    shifted.append(new_t)
    lane_w = lax.broadcasted_iota(jnp.int32, (2 * kd, LANES), 1)
    for c in range(n_chunks):
        newwin_ref[:, c * LANES:(c + 1) * LANES] = jnp.where(lane_w < LANES - s_new, shifted[c], shifted[c + 1])


def _nsa_sample_call(pt_flat, qa, small, cmpkv, pool_t, slcn, winbuf_t, winn, msel, eblk, gexp, eye,
                     n_batch, s_new, pages_per_seq):
    past = pages_per_seq * PAGE_SIZE
    n_sel = -(-(past + s_new) // SEL_BLOCK)
    n_cmp_rows = cmpkv.shape[0] // n_batch
    kv_rows = 2 * G_NSA * HEAD_DIM
    wb = winbuf_t.shape[1]
    row = lambda b, pt: (b, 0)
    fixed = lambda b, pt: (0, 0)

    def page_spec(j):
        return pl.BlockSpec((kv_rows, PAGE_SIZE), lambda b, pt: (pt[b * pages_per_seq + j], 0))

    grid_spec = pltpu.PrefetchScalarGridSpec(
        num_scalar_prefetch=1,
        grid=(n_batch,),
        in_specs=[pl.BlockSpec((s_new, qa.shape[1]), row), pl.BlockSpec((s_new, LANES), row),
                  pl.BlockSpec((n_cmp_rows, 2 * LANES), row)]
                 + [page_spec(j) for j in range(pages_per_seq)]
                 + [pl.BlockSpec((s_new, 2 * LANES), row), pl.BlockSpec((kv_rows, wb), row),
                    pl.BlockSpec((s_new, 2 * LANES), row),
                    pl.BlockSpec(msel.shape, fixed), pl.BlockSpec(eblk.shape, fixed), pl.BlockSpec(gexp.shape, fixed),
                    pl.BlockSpec(eye.shape, fixed)],
        out_specs=[pl.BlockSpec((s_new, qa.shape[1]), row), pl.BlockSpec((kv_rows, wb), row)],
    )
    return pl.pallas_call(
        functools.partial(_nsa_sample_kernel, n_pages=pages_per_seq, past=past, n_sel=n_sel),
        grid_spec=grid_spec,
        out_shape=[jax.ShapeDtypeStruct(qa.shape, F32), jax.ShapeDtypeStruct(winbuf_t.shape, F32)],
        compiler_params=_params(("arbitrary",)),
        name="nsa_sample",
    )(pt_flat, qa, small, cmpkv, *([pool_t] * pages_per_seq), slcn, winbuf_t, winn, msel, eblk, gexp, eye)


def _fox_sample_kernel(pt_ref, q_ref, small_ref, foxn_ref, *refs, n_pages, past):
    del pt_ref
    page_refs = refs[:n_pages]
    logf_refs = refs[n_pages:2 * n_pages]
    psel_ref, triu_ref, eye_ref, o_ref = refs[2 * n_pages:]
    s_new = q_ref.shape[0]
    n_keys = past + PAGE_SIZE
    key_ok = (lax.broadcasted_iota(jnp.int32, (s_new, n_keys), 1)
              <= past + lax.broadcasted_iota(jnp.int32, (s_new, n_keys), 0))

    small_pad = _pad_rows(small_ref[...], LANES)
    a, b, c = _split3(small_pad)
    psel = psel_ref[...]
    new_t = _dot_nt(psel, a) + _dot_nt(psel, b) + _dot_nt(psel, c)
    w0 = jnp.concatenate([r[...] for r in logf_refs] + [new_t], axis=0)
    wc = _dot_exact_lhs(w0, triu_ref[...])
    run = jnp.zeros((N_HEADS, 1), F32)
    c_rows = []
    for j in range(n_pages + 1):
        blk = wc[j * N_HEADS:(j + 1) * N_HEADS]
        c_rows.append(blk + run)
        run = run + blk[:, LANES - 1:LANES]
    a, b, c = _split3(_pad_rows(c_rows[n_pages], LANES))
    eye = eye_ref[...]
    cq_t = _dot_nt(eye, a) + _dot_nt(eye, b) + _dot_nt(eye, c)

    q = q_ref[...]
    foxn = foxn_ref[...]
    d_k = N_HEADS * HEAD_DIM
    n_hg = 4
    w_hg = n_hg * HEAD_DIM
    lane4 = lax.broadcasted_iota(jnp.int32, (s_new, w_hg), 1)
    own = [(lane4 >= i * HEAD_DIM) & (lane4 < (i + 1) * HEAD_DIM) for i in range(n_hg)]
    for hg in range(N_HEADS // n_hg):
        cols = slice(hg * w_hg, (hg + 1) * w_hg)
        q4 = q[:, cols]
        qbd = jnp.concatenate([jnp.where(own[i], q4, 0.0) for i in range(n_hg)], axis=0).astype(BF16)
        kt = jnp.concatenate([r[hg * w_hg:(hg + 1) * w_hg, :] for r in page_refs], axis=1).astype(BF16)
        vt = jnp.concatenate([r[d_k + hg * w_hg:d_k + (hg + 1) * w_hg, :] for r in page_refs], axis=1).astype(BF16)
        k_new = _pad_rows(foxn[:, cols], PAGE_SIZE).astype(BF16)
        v_new = _pad_rows(foxn[:, d_k + hg * w_hg:d_k + (hg + 1) * w_hg], PAGE_SIZE).astype(BF16)
        s = jnp.concatenate([_dot(qbd, kt), _dot_nt(qbd, k_new)], axis=1)
        rows = []
        for i in range(n_hg):
            head = hg * n_hg + i
            ck_row = jnp.concatenate([cr[head:head + 1, :] for cr in c_rows], axis=1)
            sh = s[i * s_new:(i + 1) * s_new] + (cq_t[0:s_new, head:head + 1] - ck_row)
            rows.append(jnp.where(key_ok, sh, NEG))
        s = jnp.concatenate(rows, axis=0)
        mx = jnp.max(s, axis=-1, keepdims=True)
        p = jnp.exp(s - mx)
        den = jnp.sum(p, axis=-1, keepdims=True)
        pb = p.astype(BF16)
        o = (_dot_nt(pb[:, :past], vt) + _dot(pb[:, past:], v_new)) / den
        out = jnp.where(own[0], o[0:s_new], 0.0)
        for i in range(1, n_hg):
            out = out + jnp.where(own[i], o[i * s_new:(i + 1) * s_new], 0.0)
        o_ref[:, cols] = out


def _fox_sample_call(pt_flat, qb, small, foxn, pool_t, logf_t, psel, triu, eye, n_batch, s_new, pages_per_seq):
    past = pages_per_seq * PAGE_SIZE
    kv_rows = 2 * N_HEADS * HEAD_DIM
    row = lambda b, pt: (b, 0)
    fixed = lambda b, pt: (0, 0)

    def page_spec(j):
        return pl.BlockSpec((kv_rows, PAGE_SIZE), lambda b, pt: (pt[b * pages_per_seq + j], 0))

    def logf_spec(j):
        return pl.BlockSpec((None,) + logf_t.shape[1:], lambda b, pt: (pt[b * pages_per_seq + j], 0, 0))

    grid_spec = pltpu.PrefetchScalarGridSpec(
        num_scalar_prefetch=1,
        grid=(n_batch,),
        in_specs=[pl.BlockSpec((s_new, qb.shape[1]), row), pl.BlockSpec((s_new, LANES), row),
                  pl.BlockSpec((s_new, foxn.shape[1]), row)]
                 + [page_spec(j) for j in range(pages_per_seq)]
                 + [logf_spec(j) for j in range(pages_per_seq)]
                 + [pl.BlockSpec(psel.shape, fixed), pl.BlockSpec(triu.shape, fixed), pl.BlockSpec(eye.shape, fixed)],
        out_specs=pl.BlockSpec((s_new, qb.shape[1]), row),
    )
    return pl.pallas_call(
        functools.partial(_fox_sample_kernel, n_pages=pages_per_seq, past=past),
        grid_spec=grid_spec,
        out_shape=jax.ShapeDtypeStruct(qb.shape, F32),
        compiler_params=_params(("arbitrary",)),
        name="fox_sample",
    )(pt_flat, qb, small, foxn, *([pool_t] * pages_per_seq), *([logf_t] * pages_per_seq), psel, triu, eye)


def _merge_kernel(x_ref, oa_ref, ob_ref, z_ref, w_ref, y_ref):
    d_a = oa_ref.shape[1]
    z = z_ref[...]
    gate = z * _sigmoid(z)
    u = jnp.concatenate([oa_ref[...] * gate[:, 0:d_a], ob_ref[...] * gate[:, d_a:]], axis=1).astype(BF16)
    y_ref[...] = x_ref[...] + _dot(u, w_ref[...])


def _merge_call(x2d, oa, ob, z, w_out_bf, tm, name):
    n, d = x2d.shape
    row = lambda i: (i, 0)
    return pl.pallas_call(
        _merge_kernel,
        grid=(n // tm,),
        in_specs=[pl.BlockSpec((tm, d), row), pl.BlockSpec((tm, oa.shape[1]), row), pl.BlockSpec((tm, ob.shape[1]), row),
                  pl.BlockSpec((tm, z.shape[1]), row), pl.BlockSpec(w_out_bf.shape, lambda i: (0, 0))],
        out_specs=pl.BlockSpec((tm, d), row),
        out_shape=jax.ShapeDtypeStruct((n, d), F32),
        compiler_params=_params(("arbitrary",)),
        name=name,
    )(x2d, oa, ob, z, w_out_bf)


def _row_tile(n, preferred):
    return preferred if n % preferred == 0 else n


def _rope_tables(pos):
    half = HEAD_DIM // 2
    inv = jnp.power(jnp.float32(ROPE_THETA), -jnp.arange(half, dtype=F32) / half)
    ang = pos.astype(F32)[:, None] * inv[None, :]
    cos = jnp.cos(ang)
    sin = jnp.sin(ang)
    return jnp.tile(cos, (1, 4)), jnp.tile(jnp.concatenate([-sin, sin], axis=1), (1, 2))


def _cmp_to_sel(n_cmp, n_sel, rows, cols):
    cs = np.arange(n_cmp) * CMP_STRIDE
    ss = np.arange(n_sel) * SEL_BLOCK
    ov = np.clip(np.minimum(cs[:, None] + CMP_LEN, ss[None, :] + SEL_BLOCK) - np.maximum(cs[:, None], ss[None, :]), 0, None)
    m = np.zeros((rows, cols), np.float32)
    m[:n_cmp, :n_sel] = ov / CMP_LEN
    return jnp.asarray(m, dtype=BF16)


def _block_expand(n_keys):
    e = np.zeros((LANES, n_keys), np.float32)
    k = np.arange(n_keys)
    e[k // SEL_BLOCK, k] = 1.0
    return jnp.asarray(e, dtype=BF16)


def _gate_expand():
    e = np.zeros((LANES, 3 * N_HEADS * HEAD_DIM), np.float32)
    for i in range(3):
        for h in range(N_HEADS):
            e[i * N_HEADS + h, i * N_HEADS * HEAD_DIM + h * HEAD_DIM:i * N_HEADS * HEAD_DIM + (h + 1) * HEAD_DIM] = 1.0
    return jnp.asarray(e, dtype=BF16)


def _head_expand():
    e = np.zeros((LANES, N_HEADS * HEAD_DIM), np.float32)
    for h in range(N_HEADS):
        e[LOGF_LANE + h, h * HEAD_DIM:(h + 1) * HEAD_DIM] = 1.0
    return jnp.asarray(e, dtype=BF16)


def _compress_weights(w1, w2, pe):
    half = CMP_LEN // 2
    eye = jnp.eye(G_NSA, dtype=F32)
    w = w1.reshape(2, half, HEAD_DIM, CMP_HID)
    w1x = jnp.einsum('pldh,gk->lgdpkh', w, eye).reshape(half * G_NSA * HEAD_DIM, 2 * G_NSA * CMP_HID).astype(BF16)
    w2x = jnp.einsum('hd,gk->ghkd', w2, eye).reshape(G_NSA * CMP_HID, G_NSA * HEAD_DIM).astype(BF16)
    pe2 = jnp.broadcast_to(pe.reshape(2, half, 1, HEAD_DIM), (2, half, G_NSA, HEAD_DIM)).reshape(2, -1)
    pex = jnp.concatenate([pe2, jnp.zeros((14, pe2.shape[1]), F32)], axis=0).astype(BF16)
    return w1x, w2x, pex


def _stored_tiles(cache):
    return jnp.transpose(cache, (0, 2, 3, 4, 1)).reshape(-1, cache.shape[1])


def kernel(x_prompt, x_sample, cache_nsa_cmp_kv, cache_nsa_slc_kv, cache_nsa_win_kv, cache_fox_kv, cache_fox_logf,
           page_table, g_norm, w_in, b_f, gq_a, gk_cmp, gk_slc, gk_win, pe_cmp_k, pe_cmp_v,
           w_cmp1_k, w_cmp2_k, w_cmp1_v, w_cmp2_v, gq_b, gk_b, w_out):
    depth = g_norm.shape[0]
    assert depth == 1, "single-layer stack"
    n_b, seq, d_model = x_prompt.shape
    n_db, s_new, _ = x_sample.shape
    pages_per_seq = page_table.shape[1]
    past = pages_per_seq * PAGE_SIZE
    wb = cache_nsa_win_kv.shape[2]
    n_pool = cache_nsa_cmp_kv.shape[1]
    d_a = N_HEADS * HEAD_DIM
    kv_w = 2 * G_NSA * HEAD_DIM

    w = w_in[0]
    o_kc, o_ga, o_za, o_qb, o_kb, o_fb, o_zb = 512, 1280, 1304, 1816, 2328, 3352, 3360
    gate_cols = o_ga + (np.arange(N_HEADS)[None, :] * 3 + np.arange(3)[:, None]).reshape(-1)
    small_w = jnp.concatenate([w[:, gate_cols], w[:, o_fb:o_fb + N_HEADS],
                               jnp.zeros((d_model, LANES - GATE_LANES - N_HEADS), F32)], axis=1)
    w_bf = jnp.concatenate([w[:, 0:o_ga], w[:, o_za:o_qb], w[:, o_zb:o_zb + d_a], w[:, o_qb:o_fb], small_w],
                           axis=1).astype(BF16)
    assert w_bf.shape[1] == C_END
    gains = jnp.concatenate([jnp.tile(v[0], 2)[None, :] for v in (gq_a, gk_slc, gk_win, gq_b, gk_b)]
                            + [jnp.zeros((3, LANES), F32)], axis=0)
    bf_row = jnp.zeros((1, LANES), F32).at[0, LOGF_LANE:LOGF_LANE + N_HEADS].set(b_f[0])
    lane_head = np.arange(LANES) // HEAD_DIM
    ones_bd = jnp.asarray(lane_head[:, None] == lane_head[None, :], dtype=BF16)
    hexp = _head_expand()
    gexp = _gate_expand()
    consts = lambda tm: (g_norm, w_bf, gains, bf_row, ones_bd,
                         jnp.asarray(np.tril(np.ones((tm, tm), np.float32)), dtype=BF16), hexp)

    tm_p = 256
    cos_p, sin_p = _rope_tables(jnp.arange(seq, dtype=jnp.int32))
    tiles_per_seq = seq // tm_p
    (cmp_p, slc_p, win_p, fox_p, small_p, z_p, qa_p, qb_p, cexp_p, csum_p, cmpd_p, slcb_p, winb_p, foxb_p) = _proj_call(
        x_prompt.reshape(n_b * seq, d_model), (cos_p, sin_p, lambda i: (i % tiles_per_seq, 0)),
        consts(tm_p), tiles_per_seq, tm_p, "proj_prompt", True)
    tm_s = _row_tile(n_db * s_new, 256)
    cos_s, sin_s = _rope_tables(past + jnp.arange(s_new, dtype=jnp.int32))
    cos_s, sin_s = jnp.tile(cos_s, (tm_s // s_new, 1)), jnp.tile(sin_s, (tm_s // s_new, 1))
    (cmp_s, slc_s, win_s, fox_s, small_s, z_s, qa_s, qb_s, _, _, _, _, _, _) = _proj_call(
        x_sample.reshape(n_db * s_new, d_model), (cos_s, sin_s, lambda i: (0, 0)),
        consts(tm_s), 1, tm_s, "proj_sample", False)

    w1k, w2k, pek = _compress_weights(w_cmp1_k[0], w_cmp2_k[0], pe_cmp_k[0])
    w1v, w2v, pev = _compress_weights(w_cmp1_v[0], w_cmp2_v[0], pe_cmp_v[0])
    cw = (w1k, w1v, pek, pev, w2k, w2v, jnp.tile(gk_cmp[0], 2)[None, :], ones_bd)
    chunk_w = CMP_STRIDE * kv_w
    ckv_p = _compress_prompt_call(cmpd_p.reshape(n_b * seq // CMP_STRIDE, chunk_w), cw, seq // CMP_STRIDE)
    pt_flat = page_table.reshape(-1).astype(jnp.int32)
    tok = np.arange(PAGE_SIZE)
    perm = np.zeros((PAGE_SIZE, PAGE_SIZE), np.float32)
    perm[(tok % CMP_STRIDE) * (PAGE_SIZE // CMP_STRIDE) + tok // CMP_STRIDE, tok] = 1.0
    ckv_s = _compress_sample_call(pt_flat, _stored_tiles(cache_nsa_cmp_kv[0]), jnp.asarray(perm, dtype=BF16), cw,
                                  n_db, pages_per_seq, 2)

    n_cmp_p = (seq - CMP_LEN) // CMP_STRIDE + 1
    msel_p = _cmp_to_sel(n_cmp_p, seq // SEL_BLOCK, seq // CMP_STRIDE, LANES)
    tq_a = 256
    eblk_p = jnp.transpose(_block_expand(seq).reshape(LANES, seq // tq_a, tq_a), (1, 0, 2))
    oa_p = _nsa_prompt_call(qa_p, small_p, ckv_p, slcb_p, winb_p, msel_p, eblk_p, gexp, n_b, seq, tq_a)
    n_cmp_s = (past + s_new - CMP_LEN) // CMP_STRIDE + 1
    n_sel_s = -(-(past + s_new) // SEL_BLOCK)
    msel_s = _cmp_to_sel(n_cmp_s, n_sel_s, past // CMP_STRIDE, LANES)
    eye_kv = jnp.asarray(np.eye(kv_w, dtype=np.float32), dtype=BF16)
    oa_s, new_win_t = _nsa_sample_call(
        pt_flat, qa_s, small_s, ckv_s, _stored_tiles(cache_nsa_slc_kv[0]), slc_s,
        _stored_tiles(cache_nsa_win_kv[0]), win_s, msel_s, _block_expand(past + PAGE_SIZE), gexp, eye_kv,
        n_db, s_new, pages_per_seq)
    new_win = jnp.transpose(new_win_t.reshape(n_db, 2, G_NSA, HEAD_DIM, wb), (0, 4, 1, 2, 3))[None]

    tq_f = 512
    c_p = csum_p[:, LOGF_LANE:LOGF_LANE + N_HEADS].reshape(n_b, seq, N_HEADS)
    ckt = jnp.transpose(c_p, (0, 2, 1)).reshape(n_b, N_HEADS, seq // tq_f, tq_f)
    ob_p = _fox_prompt_call(qb_p, foxb_p, cexp_p, ckt, n_b, seq, tq_f)
    psel = jnp.asarray(np.arange(LANES)[None, :] == (LOGF_LANE + np.arange(2 * N_HEADS))[:, None], dtype=BF16)
    psel = psel.at[N_HEADS:].set(0)
    triu = jnp.asarray(np.triu(np.ones((LANES, LANES), np.float32)), dtype=BF16)
    eye = jnp.asarray(np.eye(LANES, dtype=np.float32), dtype=BF16)
    ob_s = _fox_sample_call(
        pt_flat, qb_s, small_s, fox_s, _stored_tiles(cache_fox_kv[0]),
        jnp.transpose(cache_fox_logf[0], (0, 2, 1)), psel, triu, eye, n_db, s_new, pages_per_seq)

    w_out_bf = w_out[0].astype(BF16)
    y_p = _merge_call(x_prompt.reshape(n_b * seq, d_model), oa_p, ob_p, z_p, w_out_bf, 512, "merge_prompt")
    y_s = _merge_call(x_sample.reshape(n_db * s_new, d_model), oa_s, ob_s, z_s, w_out_bf,
                      _row_tile(n_db * s_new, 512), "merge_sample")

    kv5 = lambda a, nb, t, heads: a.reshape(1, nb, t, 2, heads, HEAD_DIM)
    kv5_t = lambda a, heads: jnp.transpose(a.reshape(n_b, 2, heads, HEAD_DIM, -1), (0, 4, 1, 2, 3))[None]
    wbp = min(WINDOW, seq)
    return (y_p.reshape(n_b, seq, d_model), y_s.reshape(n_db, s_new, d_model),
            kv5_t(cmp_p, G_NSA), kv5(cmp_s, n_db, s_new, G_NSA),
            kv5_t(slc_p, G_NSA), kv5(slc_s, n_db, s_new, G_NSA),
            kv5_t(win_p.reshape(n_b, kv_w, seq)[:, :, seq - wbp:], G_NSA), new_win,
            kv5_t(fox_p, N_HEADS), kv5(fox_s, n_db, s_new, N_HEADS),
            small_p[:, LOGF_LANE:LOGF_LANE + N_HEADS].reshape(1, n_b, seq, N_HEADS),
            small_s[:, LOGF_LANE:LOGF_LANE + N_HEADS].reshape(1, n_db, s_new, N_HEADS))
```

```python
import functools

import numpy as np
import jax
import jax.numpy as jnp
from jax import lax
from jax.experimental import pallas as pl
from jax.experimental.pallas import tpu as pltpu

F32 = jnp.float32
BF16 = jnp.bfloat16

HEAD_DIM = 64
LANES = 128
N_HEADS = 8
G_NSA = 2
R_NSA = N_HEADS // G_NSA
CMP_LEN = 32
CMP_STRIDE = 16
CMP_HID = 2 * HEAD_DIM
SEL_BLOCK = 64
SEL_TOP = 16
WINDOW = 512
PAGE_SIZE = 128
ROPE_THETA = 10000.0
EPS = 1e-6
NEG = -1e30
FORCED = 1e4
QK_SCALE = HEAD_DIM ** -0.5

VMEM_LIMIT = 56 * 1024 * 1024

C_QA, C_CMP, C_SLC, C_WIN, C_Z, C_QB, C_FOX, C_SMALL, C_END = 0, 512, 768, 1024, 1280, 2304, 2816, 3840, 3968
GATE_LANES = 3 * N_HEADS
LOGF_LANE = GATE_LANES


def _dot(a, b):
    return jnp.dot(a, b, preferred_element_type=F32)


def _dot_nt(a, b):
    return lax.dot_general(a, b, (((1,), (1,)), ((), ())), preferred_element_type=F32)


def _split2(x):
    hi = x.astype(BF16)
    lo = (x - hi.astype(F32)).astype(BF16)
    return hi, lo


def _split3(x):
    hi = x.astype(BF16)
    r = x - hi.astype(F32)
    mid = r.astype(BF16)
    lo = (r - mid.astype(F32)).astype(BF16)
    return hi, mid, lo


def _dot_exact_lhs(x, m):
    a, b, c = _split3(x)
    return _dot(a, m) + _dot(b, m) + _dot(c, m)


def _dot_exact_rhs(m, x):
    a, b, c = _split3(x)
    return _dot(m, a) + _dot(m, b) + _dot(m, c)


def _sigmoid(x):
    return 1.0 / (1.0 + jnp.exp(-x))


def _params(sem):
    return pltpu.CompilerParams(dimension_semantics=sem, vmem_limit_bytes=VMEM_LIMIT)


def _proj_kernel(x_ref, g_ref, w_ref, cos_ref, sin_ref, gains_ref, bf_ref, ones_ref, tri_ref, hexp_ref,
                 cmp_ref, slc_ref, win_ref, fox_ref, small_ref, z_ref, qa_ref, qb_ref, cexp_ref, csum_ref, cmpd_ref,
                 slcb_ref, winb_ref, foxb_ref, carry_ref, *, tm, tiles_per_seq, kv_t):
    i = pl.program_id(0)
    x = x_ref[...]
    ms = jnp.mean(x * x, axis=-1, keepdims=True)
    h = (x * lax.rsqrt(ms + EPS) * g_ref[...]).astype(BF16)
    cos = cos_ref[...]
    sin = sin_ref[...]
    lane = lax.broadcasted_iota(jnp.int32, (tm, LANES), 1)
    first_half = (lane & (HEAD_DIM // 2)) == 0
    ones = ones_ref[...]

    def proj(c0, c1):
        return _dot(h, w_ref[:, c0:c1])

    def head_norm(y, row):
        hi, lo = _split2(y * y)
        ss = _dot(hi, ones) + _dot(lo, ones)
        return y * lax.rsqrt(ss * (1.0 / HEAD_DIM) + EPS) * gains_ref[row:row + 1, :]

    def rope(y):
        partner = jnp.where(first_half, pltpu.roll(y, LANES - HEAD_DIM // 2, 1), pltpu.roll(y, HEAD_DIM // 2, 1))
        return y * cos + partner * sin

    def chunk(y, c):
        return y[:, c * LANES:(c + 1) * LANES]

    def with_ones(v, e):
        return jnp.where((lane < HEAD_DIM) if e == 0 else (lane >= HEAD_DIM), v, 1.0)

    y = proj(C_QA, C_CMP)
    for c in range(4):
        qa_ref[:, c * LANES:(c + 1) * LANES] = rope(head_norm(chunk(y, c), 0)) * QK_SCALE

    def store_kv(o_ref, t, c):
        if kv_t:
            o_ref[c * LANES:(c + 1) * LANES, :] = t.T
        else:
            o_ref[:, c * LANES:(c + 1) * LANES] = t

    y = proj(C_CMP, C_SLC)
    k = rope(chunk(y, 0))
    v = chunk(y, 1)
    cmpd_ref[:, 0:LANES] = k
    cmpd_ref[:, LANES:2 * LANES] = v
    store_kv(cmp_ref, k, 0)
    store_kv(cmp_ref, v, 1)

    for (c0, c1, row, o_ref, ob_ref) in ((C_SLC, C_WIN, 1, slc_ref, slcb_ref), (C_WIN, C_Z, 2, win_ref, winb_ref)):
        y = proj(c0, c1)
        k = rope(head_norm(chunk(y, 0), row))
        v = chunk(y, 1)
        store_kv(o_ref, k, 0)
        store_kv(o_ref, v, 1)
        ob_ref[:, 0:LANES] = k.astype(BF16)
        for e in range(2):
            ob_ref[:, (1 + e) * LANES:(2 + e) * LANES] = with_ones(v, e).astype(BF16)

    z_ref[...] = proj(C_Z, C_QB)

    y = proj(C_QB, C_FOX)
    for c in range(4):
        qb_ref[:, c * LANES:(c + 1) * LANES] = head_norm(chunk(y, c), 3) * QK_SCALE

    y = proj(C_FOX, C_SMALL)
    for c in range(8):
        t = chunk(y, c)
        if c < 4:
            t = head_norm(t, 4)
        store_kv(fox_ref, t, c)
        if c < 4:
            foxb_ref[:, c * LANES:(c + 1) * LANES] = t.astype(BF16)
        else:
            for e in range(2):
                foxb_ref[:, (c + 4 * e) * LANES:(c + 4 * e + 1) * LANES] = with_ones(t, e).astype(BF16)

    raw = proj(C_SMALL, C_END)
    zf = raw + bf_ref[...]
    logf = jnp.minimum(zf, 0.0) - jnp.log1p(jnp.exp(-jnp.abs(zf)))
    small = jnp.where(lane < GATE_LANES, _sigmoid(raw), jnp.where(lane < GATE_LANES + N_HEADS, logf, 0.0))
    small_ref[...] = small

    @pl.when(i % tiles_per_seq == 0)
    def _():
        carry_ref[...] = jnp.zeros_like(carry_ref)

    c = carry_ref[0:1, :] + _dot_exact_rhs(tri_ref[...], small)
    carry_ref[0:1, :] = c[tm - 1:tm, :]
    csum_ref[...] = c
    cexp_ref[...] = _dot_exact_lhs(c, hexp_ref[...])


def _proj_call(x2d, tables, consts, tiles_per_seq, tm, name, kv_t):
    n = x2d.shape[0]
    cos_t, sin_t, table_map = tables
    g_norm, w_bf, gains, bf_row, ones_bd, tri, hexp = consts
    row = lambda i: (i, 0)
    fixed = lambda i: (0, 0)
    full = lambda a: pl.BlockSpec(a.shape, fixed)
    kv_widths = (256, 256, 256, 1024)
    out_widths = (128, 1024, 512, 512, 512, 128, 256)
    bf_widths = (3 * LANES, 3 * LANES, 3 * N_HEADS * HEAD_DIM)
    if kv_t:
        seq = tiles_per_seq * tm
        out_shape = [jax.ShapeDtypeStruct((n // seq * w, seq), F32) for w in kv_widths]
        out_specs = [pl.BlockSpec((w, tm), lambda i: (i // tiles_per_seq, i % tiles_per_seq)) for w in kv_widths]
    else:
        out_shape = [jax.ShapeDtypeStruct((n, w), F32) for w in kv_widths]
        out_specs = [pl.BlockSpec((tm, w), row) for w in kv_widths]
    out_shape += [jax.ShapeDtypeStruct((n, w), F32) for w in out_widths]
    out_shape += [jax.ShapeDtypeStruct((n, w), BF16) for w in bf_widths]
    out_specs += [pl.BlockSpec((tm, w), row) for w in out_widths + bf_widths]
    return pl.pallas_call(
        functools.partial(_proj_kernel, tm=tm, tiles_per_seq=tiles_per_seq, kv_t=kv_t),
        grid=(n // tm,),
        in_specs=[pl.BlockSpec((tm, x2d.shape[1]), row), full(g_norm), full(w_bf),
                  pl.BlockSpec((tm, LANES), table_map), pl.BlockSpec((tm, LANES), table_map),
                  full(gains), full(bf_row), full(ones_bd), full(tri), full(hexp)],
        out_specs=out_specs,
        out_shape=out_shape,
        scratch_shapes=[pltpu.VMEM((8, LANES), F32)],
        compiler_params=_params(("arbitrary",)),
        name=name,
    )(x2d, g_norm, w_bf, cos_t, sin_t, gains, bf_row, ones_bd, tri, hexp)


def _compress_body(chunk_rows, m, wk_ref, wv_ref, pek_ref, pev_ref, w2k_ref, w2v_ref, gain_ref, ones_ref, out_ref,
                   rows_per_seq):
    hid = G_NSA * CMP_HID

    def branch(kind, w1_ref, pe_ref, w2_ref):
        r = _dot(jnp.concatenate([chunk_rows(kind), pe_ref[...]], axis=0), w1_ref[...])
        hh = r[:m, :hid] + pltpu.roll(r[:m, hid:], m - 1, 0) + r[m:m + 1, :hid] + r[m + 1:m + 2, hid:]
        act = hh * _sigmoid(hh)
        return _dot(act.astype(BF16), w2_ref[...])

    kc = branch(0, wk_ref, pek_ref, w2k_ref)
    hi, lo = _split2(kc * kc)
    ss = _dot(hi, ones_ref[...]) + _dot(lo, ones_ref[...])
    kc = kc * lax.rsqrt(ss * (1.0 / HEAD_DIM) + EPS) * gain_ref[...]
    vc = branch(1, wv_ref, pev_ref, w2v_ref)
    rows = lax.broadcasted_iota(jnp.int32, (m, LANES), 0)
    valid = (rows % rows_per_seq) != rows_per_seq - 1
    out_ref[:, 0:LANES] = jnp.where(valid, kc, 0.0)
    out_ref[:, LANES:2 * LANES] = jnp.where(valid, vc, 0.0)


def _compress_prompt_kernel(y_ref, *rest, rows_per_seq):
    y = y_ref[...].astype(BF16)
    width = 2 * G_NSA * HEAD_DIM

    def chunk_rows(kind):
        return jnp.concatenate([y[:, l * width + kind * LANES:l * width + (kind + 1) * LANES]
                                for l in range(CMP_STRIDE)], axis=1)

    _compress_body(chunk_rows, y.shape[0], *rest, rows_per_seq=rows_per_seq)


def _compress_sample_kernel(pt_ref, *refs, n_pages, rows_per_seq):
    del pt_ref
    page_refs = refs[:n_pages]
    perm_ref = refs[n_pages]
    chunks = PAGE_SIZE // CMP_STRIDE

    def chunk_rows(kind):
        tiles = [_dot_nt(perm_ref[...], r[kind * LANES:(kind + 1) * LANES, :].astype(BF16)) for r in page_refs]
        return jnp.concatenate([jnp.concatenate([t[l * chunks:(l + 1) * chunks] for t in tiles], axis=0)
                                for l in range(CMP_STRIDE)], axis=1).astype(BF16)

    _compress_body(chunk_rows, n_pages * chunks, *refs[n_pages + 1:], rows_per_seq=rows_per_seq)


def _compress_prompt_call(y2d, cw, rows_per_seq):
    n = y2d.shape[0]
    fixed = lambda i: (0, 0)
    return pl.pallas_call(
        functools.partial(_compress_prompt_kernel, rows_per_seq=rows_per_seq),
        grid=(n // rows_per_seq,),
        in_specs=[pl.BlockSpec((rows_per_seq, y2d.shape[1]), lambda i: (i, 0))] + [pl.BlockSpec(a.shape, fixed) for a in cw],
        out_specs=pl.BlockSpec((rows_per_seq, 2 * LANES), lambda i: (i, 0)),
        out_shape=jax.ShapeDtypeStruct((n, 2 * LANES), F32),
        compiler_params=_params(("arbitrary",)),
        name="compress_prompt",
    )(y2d, *cw)


def _compress_sample_call(pt_flat, pool_t, perm, cw, n_batch, pages_per_seq, batch_per_step):
    kv_rows = 2 * G_NSA * HEAD_DIM
    rows_per_seq = pages_per_seq * (PAGE_SIZE // CMP_STRIDE)
    n_pages = batch_per_step * pages_per_seq
    fixed = lambda i, pt: (0, 0)

    def page_spec(k):
        bb, j = divmod(k, pages_per_seq)
        return pl.BlockSpec((kv_rows, PAGE_SIZE), lambda i, pt: (pt[(i * batch_per_step + bb) * pages_per_seq + j], 0))

    m = batch_per_step * rows_per_seq
    grid_spec = pltpu.PrefetchScalarGridSpec(
        num_scalar_prefetch=1,
        grid=(n_batch // batch_per_step,),
        in_specs=[page_spec(k) for k in range(n_pages)] + [pl.BlockSpec(a.shape, fixed) for a in (perm,) + tuple(cw)],
        out_specs=pl.BlockSpec((m, 2 * LANES), lambda i, pt: (i, 0)),
    )
    return pl.pallas_call(
        functools.partial(_compress_sample_kernel, n_pages=n_pages, rows_per_seq=rows_per_seq),
        grid_spec=grid_spec,
        out_shape=jax.ShapeDtypeStruct((n_batch * rows_per_seq, 2 * LANES), F32),
        compiler_params=_params(("arbitrary",)),
        name="compress_sample",
    )(pt_flat, *([pool_t] * n_pages), perm, *cw)


def _group_queries(qa, g, lane):
    in_group = (lane < HEAD_DIM) if g == 0 else (lane >= HEAD_DIM)
    rows = []
    for r in range(R_NSA):
        head = R_NSA * g + r
        y = qa[:, (head // 2) * LANES:(head // 2 + 1) * LANES]
        if head % 2 != g:
            y = pltpu.roll(y, HEAD_DIM, 1)
        rows.append(jnp.where(in_group, y, 0.0))
    return jnp.concatenate(rows, axis=0)


def _assemble_heads(res, tq, lane):
    chunks = []
    for c in range(N_HEADS // 2):
        g = c // 2
        a = res[g][(2 * (c % 2)) * tq:(2 * (c % 2) + 1) * tq]
        b = res[g][(2 * (c % 2) + 1) * tq:(2 * (c % 2) + 2) * tq]
        if g == 0:
            b = pltpu.roll(b, HEAD_DIM, 1)
        else:
            a = pltpu.roll(a, HEAD_DIM, 1)
        chunks.append(jnp.where(lane < HEAD_DIM, a, b))
    return jnp.concatenate(chunks, axis=1)


def _flash_step(s, n_rb, rb, fix, values, m_ref, acc_ref):
    n_c = s.shape[1] // LANES
    cols = [jnp.concatenate([fix(r, c, s[r * rb:(r + 1) * rb, c * LANES:(c + 1) * LANES]) for r in range(n_rb)], axis=0)
            for c in range(n_c)]
    mx = cols[0]
    for c in range(1, n_c):
        mx = jnp.maximum(mx, cols[c])
    m_prev = m_ref[...]
    m_new = jnp.maximum(m_prev, jnp.max(mx, axis=-1, keepdims=True))
    alpha = jnp.exp(m_prev - m_new)
    p = jnp.concatenate([jnp.exp(x - m_new) for x in cols], axis=1).astype(BF16)
    if len(values) == 1:
        pv = _dot(p, values[0])
    else:
        rows_per_v = n_rb * rb // len(values)
        pv = jnp.concatenate([_dot(p[i * rows_per_v:(i + 1) * rows_per_v], v) for i, v in enumerate(values)], axis=0)
    acc_ref[...] = alpha * acc_ref[...] + pv
    m_ref[...] = m_new


def _flash_init(m_ref, acc_ref):
    m_ref[...] = jnp.full(m_ref.shape, NEG, F32)
    acc_ref[...] = jnp.zeros(acc_ref.shape, F32)


def _flash_result(acc_ref):
    acc = acc_ref[...]
    return acc / pltpu.roll(acc, HEAD_DIM, 1)


def _block_ranks(impm, n_blocks, lane):
    cnt = jnp.zeros(impm.shape, F32)
    for j in range(n_blocks):
        col = impm[:, j:j + 1]
        ge = jnp.where(col >= impm, 1.0, 0.0)
        gt = jnp.where(col > impm, 1.0, 0.0)
        cnt = cnt + jnp.where(lane > j, ge, gt)
    return cnt


def _compressed_branch(qz, ck, cv, cmask, msel, n_heads, tq):
    q_hi, q_lo = _split2(qz)
    k_hi, k_lo = _split2(ck)
    s = _dot_nt(q_hi, k_hi) + _dot_nt(q_hi, k_lo) + _dot_nt(q_lo, k_hi)
    ps = []
    psum = None
    for r in range(n_heads):
        sr = jnp.where(cmask, s[r * tq:(r + 1) * tq], NEG)
        mx = jnp.max(sr, axis=-1, keepdims=True)
        e = jnp.where(cmask, jnp.exp(sr - mx), 0.0)
        den = jnp.sum(e, axis=-1, keepdims=True)
        p = e / jnp.where(den > 0.0, den, 1.0)
        ps.append(p)
        psum = p if psum is None else psum + p
    o = _dot(jnp.concatenate(ps, axis=0).astype(BF16), cv)
    return o, _dot_exact_lhs(psum, msel)


def _select_blocks(imp, tpos, lane, n_blocks):
    forced = (lane == 0) | (lane == tpos // SEL_BLOCK)
    causal = lane * SEL_BLOCK <= tpos
    impm = jnp.where(forced, FORCED, jnp.where(causal, imp, -1.0))
    impm = jnp.where(lane < n_blocks, impm, -3e38)
    cnt = _block_ranks(impm, n_blocks, lane)
    n_top = min(SEL_TOP, n_blocks)
    return jnp.where(cnt < n_top, jnp.where(lane < n_blocks, 1.0, 0.0), 0.0)


def _select_blocks_t(imp, s0, n_blocks, cnt_ref):
    tq = imp.shape[0]
    grp = 8
    imp_t = jnp.transpose(imp)[0:n_blocks]
    blk = lax.broadcasted_iota(jnp.int32, (n_blocks, tq), 0)
    tpos = s0 + lax.broadcasted_iota(jnp.int32, (n_blocks, tq), 1)
    forced = (blk == 0) | (blk == tpos // SEL_BLOCK)
    impm = jnp.where(forced, FORCED, jnp.where(blk * SEL_BLOCK <= tpos, imp_t, -1.0))
    cnt_ref[...] = jnp.zeros(cnt_ref.shape, F32)
    sub = lax.broadcasted_iota(jnp.int32, (grp, tq), 0)
    for jb in range(n_blocks // grp):
        @pl.when(jb * grp * SEL_BLOCK <= s0 + tq - 1)
        def _(jb=jb):
            for rg in range(n_blocks // grp):
                x = impm[rg * grp:(rg + 1) * grp]
                cnt = cnt_ref[rg * grp:(rg + 1) * grp, :]
                for j in range(jb * grp, (jb + 1) * grp):
                    row = impm[j:j + 1, :]
                    if rg > jb:
                        beat = row >= x
                    elif rg < jb:
                        beat = row > x
                    else:
                        cnt = cnt + jnp.where(sub > j - jb * grp, jnp.where(row >= x, 1.0, 0.0),
                                              jnp.where(row > x, 1.0, 0.0))
                        continue
                    cnt = cnt + jnp.where(beat, 1.0, 0.0)
                cnt_ref[rg * grp:(rg + 1) * grp, :] = cnt
    sel_t = jnp.where(cnt_ref[...] < min(SEL_TOP, n_blocks), 1.0, 0.0)
    return jnp.transpose(jnp.concatenate([sel_t, jnp.zeros((LANES - n_blocks, tq), F32)], axis=0))


def _nsa_prompt_kernel(qa_ref, small_ref, cmp_ref, slc_ref, win_ref, msel_ref, eblk_ref, gexp_ref,
                       oa_ref, m_ref, acc_ref, cnt_ref, *, tq, n_sel):
    tk = tq
    qi = pl.program_id(1)
    s0 = qi * tq
    lane = lax.broadcasted_iota(jnp.int32, (tq, LANES), 1)
    trow = lax.broadcasted_iota(jnp.int32, (tq, LANES), 0)
    tpos = s0 + trow
    row_t = lax.broadcasted_iota(jnp.int32, (tq, tk), 0)
    col_t = lax.broadcasted_iota(jnp.int32, (tq, tk), 1)
    diag_mask = col_t <= row_t
    far_mask = (2 * tk + row_t - col_t) <= WINDOW
    n_cmp = cmp_ref.shape[0]
    cmask = (lax.broadcasted_iota(jnp.int32, (tq, n_cmp), 1) * CMP_STRIDE + CMP_LEN - 1
             <= s0 + lax.broadcasted_iota(jnp.int32, (tq, n_cmp), 0))
    qa = qa_ref[...]
    ck = cmp_ref[:, 0:LANES]
    cv = cmp_ref[:, LANES:2 * LANES].astype(BF16)

    def chunk_of(mask, c):
        return mask[:, c * LANES:(c + 1) * LANES]

    o_cmp, sel = [], []
    qz = [_group_queries(qa, g, lane) for g in range(G_NSA)]
    for g in range(G_NSA):
        oc, imp = _compressed_branch(qz[g], ck, cv, cmask, msel_ref[...], R_NSA, tq)
        o_cmp.append(oc)
        sel.append(_select_blocks_t(imp, s0, n_sel, cnt_ref).astype(BF16))
    q_all = jnp.concatenate(qz, axis=0).astype(BF16)

    def flash_step(kv_ref, kt, biases, mask):
        off = pl.multiple_of(kt * tk, tk)
        k2 = kv_ref[pl.ds(off, tk), 0:LANES]
        values = [kv_ref[pl.ds(off, tk), (1 + g) * LANES:(2 + g) * LANES] for g in range(G_NSA)]

        def fix(r, c, x):
            if biases is not None:
                x = x + chunk_of(biases[r // R_NSA], c)
            if mask is not None:
                x = jnp.where(chunk_of(mask, c), x, NEG)
            return x

        _flash_step(_dot_nt(q_all, k2), G_NSA * R_NSA, tq, fix, values, m_ref, acc_ref)

    def sel_biases(kt):
        return [(_dot(sel[g], eblk_ref[kt]) - 1.0) * (-NEG) for g in range(G_NSA)]

    def per_group(o):
        return [o[g * R_NSA * tq:(g + 1) * R_NSA * tq] for g in range(G_NSA)]

    _flash_init(m_ref, acc_ref)

    def sel_body(kt, carry):
        flash_step(slc_ref, kt, sel_biases(kt), None)
        return carry

    lax.fori_loop(0, qi, sel_body, 0)
    flash_step(slc_ref, qi, sel_biases(qi), diag_mask)
    o_sel = per_group(_flash_result(acc_ref))

    _flash_init(m_ref, acc_ref)

    @pl.when(qi >= 2)
    def _():
        flash_step(win_ref, qi - 2, None, far_mask)

    @pl.when(qi >= 1)
    def _():
        flash_step(win_ref, qi - 1, None, None)

    flash_step(win_ref, qi, None, diag_mask)
    o_win = per_group(_flash_result(acc_ref))

    gates = _dot_exact_lhs(small_ref[...], gexp_ref[...])
    d_a = N_HEADS * HEAD_DIM
    oa_ref[...] = (gates[:, 0:d_a] * _assemble_heads(o_cmp, tq, lane)
                   + gates[:, d_a:2 * d_a] * _assemble_heads(o_sel, tq, lane)
                   + gates[:, 2 * d_a:3 * d_a] * _assemble_heads(o_win, tq, lane))


def _nsa_prompt_call(qa, small, cmpkv, slcb, winb, msel, eblk, gexp, n_batch, seq, tq):
    nq = seq // tq
    n_cmp_rows = cmpkv.shape[0] // n_batch
    n_sel = seq // SEL_BLOCK
    qrow = lambda b, q: (b * nq + q, 0)
    per_b = lambda b, q: (b, 0)
    fixed = lambda b, q: (0, 0)
    return pl.pallas_call(
        functools.partial(_nsa_prompt_kernel, tq=tq, n_sel=n_sel),
        grid=(n_batch, nq),
        in_specs=[pl.BlockSpec((tq, qa.shape[1]), qrow), pl.BlockSpec((tq, LANES), qrow),
                  pl.BlockSpec((n_cmp_rows, 2 * LANES), per_b),
                  pl.BlockSpec((seq, 3 * LANES), per_b), pl.BlockSpec((seq, 3 * LANES), per_b),
                  pl.BlockSpec(msel.shape, fixed), pl.BlockSpec(eblk.shape, lambda b, q: (0, 0, 0)),
                  pl.BlockSpec(gexp.shape, fixed)],
        out_specs=pl.BlockSpec((tq, qa.shape[1]), qrow),
        out_shape=jax.ShapeDtypeStruct(qa.shape, F32),
        scratch_shapes=[pltpu.VMEM((G_NSA * R_NSA * tq, LANES), F32), pltpu.VMEM((G_NSA * R_NSA * tq, LANES), F32),
                        pltpu.VMEM((n_sel, tq), F32)],
        compiler_params=_params(("arbitrary", "arbitrary")),
        name="nsa_prompt",
    )(qa, small, cmpkv, slcb, winb, msel, eblk, gexp)


def _fox_prompt_kernel(q_ref, k_ref, v0_ref, v1_ref, cq_ref, ck_ref, o_ref, m_ref, acc_ref, *, tq):
    tk = tq
    qi = pl.program_id(2)
    lane = lax.broadcasted_iota(jnp.int32, (tq, LANES), 1)
    q = q_ref[...]
    qz = jnp.concatenate([jnp.where(lane < HEAD_DIM, q, 0.0), jnp.where(lane >= HEAD_DIM, q, 0.0)], axis=0).astype(BF16)
    cq = cq_ref[...]
    cq_sw = pltpu.roll(cq, HEAD_DIM, 1)
    cq_rep = [jnp.where(lane < HEAD_DIM, cq, cq_sw), jnp.where(lane >= HEAD_DIM, cq, cq_sw)]
    diag_mask = lax.broadcasted_iota(jnp.int32, (tq, tk), 1) <= lax.broadcasted_iota(jnp.int32, (tq, tk), 0)
    v_refs = (v0_ref, v1_ref)
    _flash_init(m_ref, acc_ref)

    def step(kt, mask):
        off = pl.multiple_of(kt * tk, tk)
        k2 = k_ref[pl.ds(off, tk), :]
        ck = [ck_ref[0, e, pl.ds(kt, 1), :] for e in range(2)]

        def fix(e, c, x):
            x = x + (cq_rep[e] - ck[e][:, c * LANES:(c + 1) * LANES])
            if mask is not None:
                x = jnp.where(mask[:, c * LANES:(c + 1) * LANES], x, NEG)
            return x

        _flash_step(_dot_nt(qz, k2), 2, tq, fix, [r[pl.ds(off, tk), :] for r in v_refs], m_ref, acc_ref)

    def body(kt, carry):
        step(kt, None)
        return carry

    lax.fori_loop(0, qi, body, 0)
    step(qi, diag_mask)
    o = _flash_result(acc_ref)
    o_ref[...] = jnp.where(lane < HEAD_DIM, o[0:tq], o[tq:2 * tq])


def _fox_prompt_call(qb, foxb, cexp, ckt, n_batch, seq, tq):
    nq = seq // tq
    n_pairs = N_HEADS // 2
    qmap = lambda b, hp, q: (b * nq + q, hp)
    return pl.pallas_call(
        functools.partial(_fox_prompt_kernel, tq=tq),
        grid=(n_batch, n_pairs, nq),
        in_specs=[pl.BlockSpec((tq, LANES), qmap),
                  pl.BlockSpec((seq, LANES), lambda b, hp, q: (b, hp)),
                  pl.BlockSpec((seq, LANES), lambda b, hp, q: (b, n_pairs + hp)),
                  pl.BlockSpec((seq, LANES), lambda b, hp, q: (b, 2 * n_pairs + hp)),
                  pl.BlockSpec((tq, LANES), qmap),
                  pl.BlockSpec((1, 2, nq, tq), lambda b, hp, q: (b, hp, 0, 0))],
        out_specs=pl.BlockSpec((tq, LANES), qmap),
        out_shape=jax.ShapeDtypeStruct(qb.shape, F32),
        scratch_shapes=[pltpu.VMEM((2 * tq, LANES), F32), pltpu.VMEM((2 * tq, LANES), F32)],
        compiler_params=_params(("arbitrary", "arbitrary", "arbitrary")),
        name="fox_prompt",
    )(qb, foxb, foxb, foxb, cexp, ckt)


def _pad_rows(x, rows):
    return jnp.concatenate([x, jnp.zeros((rows - x.shape[0], x.shape[1]), x.dtype)], axis=0)


def _nsa_sample_kernel(pt_ref, qa_ref, small_ref, cmp_ref, *refs, n_pages, n_seq, past, n_sel):
    del pt_ref
    all_pages = refs[:n_pages * n_seq]
    (slcn_ref, winbuf_ref, winn_ref, msel_ref, eblk_ref, gexp_ref, eye_ref, oa_ref, newwin_ref) = refs[n_pages * n_seq:]
    s_new = qa_ref.shape[0] // n_seq
    wb = winbuf_ref.shape[1]
    kd = G_NSA * HEAD_DIM
    n_keys = past + PAGE_SIZE
    lane = lax.broadcasted_iota(jnp.int32, (s_new, LANES), 1)
    tpos = past + lax.broadcasted_iota(jnp.int32, (s_new, LANES), 0)
    n_cmp = cmp_ref.shape[0] // n_seq
    cmask = (lax.broadcasted_iota(jnp.int32, (s_new, n_cmp), 1) * CMP_STRIDE + CMP_LEN - 1
             <= past + lax.broadcasted_iota(jnp.int32, (s_new, n_cmp), 0))
    kpos = lax.broadcasted_iota(jnp.int32, (s_new, n_keys), 1)
    key_ok = kpos <= past + lax.broadcasted_iota(jnp.int32, (s_new, n_keys), 0)
    w_keys = wb + PAGE_SIZE
    wi = lax.broadcasted_iota(jnp.int32, (s_new, w_keys), 1)
    wt = lax.broadcasted_iota(jnp.int32, (s_new, w_keys), 0)
    dist = wb + wt - wi
    win_ok = (dist >= 0) & (dist <= WINDOW) & (past - wb + wi >= 0) & (wi < wb + s_new)
    lane_w = lax.broadcasted_iota(jnp.int32, (2 * kd, LANES), 1)
    for seq_i in range(n_seq):
        _nsa_sample_one(seq_i, all_pages[seq_i * n_pages:(seq_i + 1) * n_pages], qa_ref, small_ref, cmp_ref, slcn_ref,
                        winbuf_ref, winn_ref, msel_ref, eblk_ref, gexp_ref, eye_ref, oa_ref, newwin_ref,
                        s_new, n_cmp, n_sel, lane, tpos, cmask, key_ok, win_ok, lane_w)


def _nsa_sample_one(seq_i, page_refs, qa_ref, small_ref, cmp_ref, slcn_ref, winbuf_ref, winn_ref, msel_ref, eblk_ref,
                    gexp_ref, eye_ref, oa_ref, newwin_ref, s_new, n_cmp, n_sel, lane, tpos, cmask, key_ok, win_ok, lane_w):
    kd = G_NSA * HEAD_DIM
    wb = winbuf_ref.shape[1]
    tok = slice(seq_i * s_new, (seq_i + 1) * s_new)
    w0 = seq_i * 2 * kd
    qa = qa_ref[tok, :]
    ck = cmp_ref[seq_i * n_cmp:(seq_i + 1) * n_cmp, 0:LANES]
    cv = cmp_ref[seq_i * n_cmp:(seq_i + 1) * n_cmp, LANES:2 * LANES].astype(BF16)
    slcn = slcn_ref[tok, :]
    winn = winn_ref[tok, :]
    qz = [_group_queries(qa, g, lane) for g in range(G_NSA)]
    q_all = jnp.concatenate(qz, axis=0).astype(BF16)

    def attend(kt, vt, k_new, v_new, masks):
        n_past = kt.shape[1]
        s = jnp.concatenate([_dot(q_all, kt), _dot_nt(q_all, k_new)], axis=1)
        s = jnp.concatenate([jnp.where(masks[g], s[(g * R_NSA + r) * s_new:(g * R_NSA + r + 1) * s_new], NEG)
                             for g in range(G_NSA) for r in range(R_NSA)], axis=0)
        mx = jnp.max(s, axis=-1, keepdims=True)
        p = jnp.exp(s - mx)
        den = jnp.sum(p, axis=-1, keepdims=True)
        pb = p.astype(BF16)
        return (_dot_nt(pb[:, :n_past], vt) + _dot(pb[:, n_past:], v_new)) / den

    def new_rows(x):
        return _pad_rows(x, PAGE_SIZE).astype(BF16)

    o_cmp, sel_ok = [], []
    for g in range(G_NSA):
        oc, imp = _compressed_branch(qz[g], ck, cv, cmask, msel_ref[...], R_NSA, s_new)
        o_cmp.append(oc)
        sel = _select_blocks(imp, tpos, lane, n_sel).astype(BF16)
        sel_ok.append((_dot(sel, eblk_ref[...]) > 0.5) & key_ok)

    kt = jnp.concatenate([r[0:kd, :] for r in page_refs], axis=1).astype(BF16)
    vt = jnp.concatenate([r[kd:2 * kd, :] for r in page_refs], axis=1).astype(BF16)
    o_s = attend(kt, vt, new_rows(slcn[:, 0:LANES]), new_rows(slcn[:, LANES:2 * LANES]), sel_ok)
    o_w = attend(winbuf_ref[w0:w0 + kd, :].astype(BF16), winbuf_ref[w0 + kd:w0 + 2 * kd, :].astype(BF16),
                 new_rows(winn[:, 0:LANES]), new_rows(winn[:, LANES:2 * LANES]), [win_ok] * G_NSA)
    half = R_NSA * s_new
    o_sel = [o_s[0:half], o_s[half:2 * half]]
    o_win = [o_w[0:half], o_w[half:2 * half]]

    gates = _dot_exact_lhs(small_ref[tok, :], gexp_ref[...])
    d_a = N_HEADS * HEAD_DIM
    oa_ref[tok, :] = (gates[:, 0:d_a] * _assemble_heads(o_cmp, s_new, lane)
                      + gates[:, d_a:2 * d_a] * _assemble_heads(o_sel, s_new, lane)
                      + gates[:, 2 * d_a:3 * d_a] * _assemble_heads(o_win, s_new, lane))

    xn = jnp.concatenate([jnp.zeros((LANES - s_new, 2 * LANES), F32), winn], axis=0)
    a, b, c = _split3(xn)
    eye = eye_ref[...]
    new_t = _dot_nt(eye, a) + _dot_nt(eye, b) + _dot_nt(eye, c)
    n_chunks = wb // LANES
    shifted = [pltpu.roll(winbuf_ref[w0:w0 + 2 * kd, c * LANES:(c + 1) * LANES], LANES - s_new, 1)
               for c in range(n_chunks)]
    shifted.append(new_t)
    for c in range(n_chunks):
        newwin_ref[w0:w0 + 2 * kd, c * LANES:(c + 1) * LANES] = jnp.where(lane_w < LANES - s_new, shifted[c], shifted[c + 1])


def _nsa_sample_call(pt_flat, qa, small, cmpkv, pool_t, slcn, winbuf_t, winn, msel, eblk, gexp, eye,
                     n_batch, s_new, pages_per_seq):
    past = pages_per_seq * PAGE_SIZE
    n_sel = -(-(past + s_new) // SEL_BLOCK)
    n_cmp_rows = cmpkv.shape[0] // n_batch
    kv_rows = 2 * G_NSA * HEAD_DIM
    wb = winbuf_t.shape[1]
    n_seq = 2 if n_batch % 2 == 0 else 1
    row = lambda b, pt: (b, 0)
    fixed = lambda b, pt: (0, 0)

    def page_spec(k):
        i, j = divmod(k, pages_per_seq)
        return pl.BlockSpec((kv_rows, PAGE_SIZE), lambda b, pt: (pt[(b * n_seq + i) * pages_per_seq + j], 0))

    grid_spec = pltpu.PrefetchScalarGridSpec(
        num_scalar_prefetch=1,
        grid=(n_batch // n_seq,),
        in_specs=[pl.BlockSpec((n_seq * s_new, qa.shape[1]), row), pl.BlockSpec((n_seq * s_new, LANES), row),
                  pl.BlockSpec((n_seq * n_cmp_rows, 2 * LANES), row)]
                 + [page_spec(k) for k in range(n_seq * pages_per_seq)]
                 + [pl.BlockSpec((n_seq * s_new, 2 * LANES), row), pl.BlockSpec((n_seq * kv_rows, wb), row),
                    pl.BlockSpec((n_seq * s_new, 2 * LANES), row),
                    pl.BlockSpec(msel.shape, fixed), pl.BlockSpec(eblk.shape, fixed), pl.BlockSpec(gexp.shape, fixed),
                    pl.BlockSpec(eye.shape, fixed)],
        out_specs=[pl.BlockSpec((n_seq * s_new, qa.shape[1]), row), pl.BlockSpec((n_seq * kv_rows, wb), row)],
    )
    return pl.pallas_call(
        functools.partial(_nsa_sample_kernel, n_pages=pages_per_seq, n_seq=n_seq, past=past, n_sel=n_sel),
        grid_spec=grid_spec,
        out_shape=[jax.ShapeDtypeStruct(qa.shape, F32), jax.ShapeDtypeStruct(winbuf_t.shape, F32)],
        compiler_params=_params(("arbitrary",)),
        name="nsa_sample",
    )(pt_flat, qa, small, cmpkv, *([pool_t] * (n_seq * pages_per_seq)), slcn, winbuf_t, winn, msel, eblk, gexp, eye)


def _fox_sample_kernel(pt_ref, q_ref, small_ref, foxn_ref, *refs, n_pages, n_seq, past):
    del pt_ref
    total = n_pages * n_seq
    psel_ref, triu_ref, eye_ref, o_ref = refs[2 * total:]
    s_new = q_ref.shape[0] // n_seq
    n_keys = past + PAGE_SIZE
    key_ok = (lax.broadcasted_iota(jnp.int32, (s_new, n_keys), 1)
              <= past + lax.broadcasted_iota(jnp.int32, (s_new, n_keys), 0))
    for seq_i in range(n_seq):
        _fox_sample_one(seq_i, refs[seq_i * n_pages:(seq_i + 1) * n_pages],
                        refs[total + seq_i * n_pages:total + (seq_i + 1) * n_pages],
                        q_ref, small_ref, foxn_ref, psel_ref, triu_ref, eye_ref, o_ref, s_new, past, key_ok)


def _fox_sample_one(seq_i, page_refs, logf_refs, q_ref, small_ref, foxn_ref, psel_ref, triu_ref, eye_ref, o_ref,
                    s_new, past, key_ok):
    n_pages = len(page_refs)
    tok = slice(seq_i * s_new, (seq_i + 1) * s_new)
    small_pad = _pad_rows(small_ref[tok, :], LANES)
    a, b, c = _split3(small_pad)
    psel = psel_ref[...]
    new_t = _dot_nt(psel, a) + _dot_nt(psel, b) + _dot_nt(psel, c)
    w0 = jnp.concatenate([r[...] for r in logf_refs] + [new_t], axis=0)
    wc = _dot_exact_lhs(w0, triu_ref[...])
    run = jnp.zeros((N_HEADS, 1), F32)
    c_rows = []
    for j in range(n_pages + 1):
        blk = wc[j * N_HEADS:(j + 1) * N_HEADS]
        c_rows.append(blk + run)
        run = run + blk[:, LANES - 1:LANES]
    a, b, c = _split3(_pad_rows(c_rows[n_pages], LANES))
    eye = eye_ref[...]
    cq_t = _dot_nt(eye, a) + _dot_nt(eye, b) + _dot_nt(eye, c)

    q = q_ref[tok, :]
    foxn = foxn_ref[tok, :]
    d_k = N_HEADS * HEAD_DIM
    n_hg = 4
    w_hg = n_hg * HEAD_DIM
    lane4 = lax.broadcasted_iota(jnp.int32, (s_new, w_hg), 1)
    own = [(lane4 >= i * HEAD_DIM) & (lane4 < (i + 1) * HEAD_DIM) for i in range(n_hg)]
    for hg in range(N_HEADS // n_hg):
        cols = slice(hg * w_hg, (hg + 1) * w_hg)
        q4 = q[:, cols]
        qbd = jnp.concatenate([jnp.where(own[i], q4, 0.0) for i in range(n_hg)], axis=0).astype(BF16)
        kt = jnp.concatenate([r[hg * w_hg:(hg + 1) * w_hg, :] for r in page_refs], axis=1).astype(BF16)
        vt = jnp.concatenate([r[d_k + hg * w_hg:d_k + (hg + 1) * w_hg, :] for r in page_refs], axis=1).astype(BF16)
        k_new = _pad_rows(foxn[:, cols], PAGE_SIZE).astype(BF16)
        v_new = _pad_rows(foxn[:, d_k + hg * w_hg:d_k + (hg + 1) * w_hg], PAGE_SIZE).astype(BF16)
        s = jnp.concatenate([_dot(qbd, kt), _dot_nt(qbd, k_new)], axis=1)
        rows = []
        for i in range(n_hg):
            head = hg * n_hg + i
            ck_row = jnp.concatenate([cr[head:head + 1, :] for cr in c_rows], axis=1)
            sh = s[i * s_new:(i + 1) * s_new] + (cq_t[0:s_new, head:head + 1] - ck_row)
            rows.append(jnp.where(key_ok, sh, NEG))
        s = jnp.concatenate(rows, axis=0)
        mx = jnp.max(s, axis=-1, keepdims=True)
        p = jnp.exp(s - mx)
        den = jnp.sum(p, axis=-1, keepdims=True)
        pb = p.astype(BF16)
        o = (_dot_nt(pb[:, :past], vt) + _dot(pb[:, past:], v_new)) / den
        out = jnp.where(own[0], o[0:s_new], 0.0)
        for i in range(1, n_hg):
            out = out + jnp.where(own[i], o[i * s_new:(i + 1) * s_new], 0.0)
        o_ref[tok, cols] = out


def _fox_sample_call(pt_flat, qb, small, foxn, pool_t, logf_t, psel, triu, eye, n_batch, s_new, pages_per_seq):
    past = pages_per_seq * PAGE_SIZE
    kv_rows = 2 * N_HEADS * HEAD_DIM
    n_seq = 2 if n_batch % 2 == 0 else 1
    row = lambda b, pt: (b, 0)
    fixed = lambda b, pt: (0, 0)

    def page_of(k):
        i, j = divmod(k, pages_per_seq)
        return lambda b, pt: pt[(b * n_seq + i) * pages_per_seq + j]

    def page_spec(k):
        page = page_of(k)
        return pl.BlockSpec((kv_rows, PAGE_SIZE), lambda b, pt: (page(b, pt), 0))

    def logf_spec(k):
        page = page_of(k)
        return pl.BlockSpec((None,) + logf_t.shape[1:], lambda b, pt: (page(b, pt), 0, 0))

    total = n_seq * pages_per_seq
    grid_spec = pltpu.PrefetchScalarGridSpec(
        num_scalar_prefetch=1,
        grid=(n_batch // n_seq,),
        in_specs=[pl.BlockSpec((n_seq * s_new, qb.shape[1]), row), pl.BlockSpec((n_seq * s_new, LANES), row),
                  pl.BlockSpec((n_seq * s_new, foxn.shape[1]), row)]
                 + [page_spec(k) for k in range(total)]
                 + [logf_spec(k) for k in range(total)]
                 + [pl.BlockSpec(psel.shape, fixed), pl.BlockSpec(triu.shape, fixed), pl.BlockSpec(eye.shape, fixed)],
        out_specs=pl.BlockSpec((n_seq * s_new, qb.shape[1]), row),
    )
    return pl.pallas_call(
        functools.partial(_fox_sample_kernel, n_pages=pages_per_seq, n_seq=n_seq, past=past),
        grid_spec=grid_spec,
        out_shape=jax.ShapeDtypeStruct(qb.shape, F32),
        compiler_params=_params(("arbitrary",)),
        name="fox_sample",
    )(pt_flat, qb, small, foxn, *([pool_t] * total), *([logf_t] * total), psel, triu, eye)


def _merge_kernel(x_ref, oa_ref, ob_ref, z_ref, w_ref, y_ref):
    d_a = oa_ref.shape[1]
    z = z_ref[...]
    gate = z * _sigmoid(z)
    u = jnp.concatenate([oa_ref[...] * gate[:, 0:d_a], ob_ref[...] * gate[:, d_a:]], axis=1).astype(BF16)
    y_ref[...] = x_ref[...] + _dot(u, w_ref[...])


def _merge_call(x2d, oa, ob, z, w_out_bf, tm, name):
    n, d = x2d.shape
    row = lambda i: (i, 0)
    return pl.pallas_call(
        _merge_kernel,
        grid=(n // tm,),
        in_specs=[pl.BlockSpec((tm, d), row), pl.BlockSpec((tm, oa.shape[1]), row), pl.BlockSpec((tm, ob.shape[1]), row),
                  pl.BlockSpec((tm, z.shape[1]), row), pl.BlockSpec(w_out_bf.shape, lambda i: (0, 0))],
        out_specs=pl.BlockSpec((tm, d), row),
        out_shape=jax.ShapeDtypeStruct((n, d), F32),
        compiler_params=_params(("arbitrary",)),
        name=name,
    )(x2d, oa, ob, z, w_out_bf)


def _row_tile(n, preferred):
    return preferred if n % preferred == 0 else n


def _rope_tables(pos):
    half = HEAD_DIM // 2
    inv = jnp.power(jnp.float32(ROPE_THETA), -jnp.arange(half, dtype=F32) / half)
    ang = pos.astype(F32)[:, None] * inv[None, :]
    cos = jnp.cos(ang)
    sin = jnp.sin(ang)
    return jnp.tile(cos, (1, 4)), jnp.tile(jnp.concatenate([-sin, sin], axis=1), (1, 2))


def _cmp_to_sel(n_cmp, n_sel, rows, cols):
    cs = np.arange(n_cmp) * CMP_STRIDE
    ss = np.arange(n_sel) * SEL_BLOCK
    ov = np.clip(np.minimum(cs[:, None] + CMP_LEN, ss[None, :] + SEL_BLOCK) - np.maximum(cs[:, None], ss[None, :]), 0, None)
    m = np.zeros((rows, cols), np.float32)
    m[:n_cmp, :n_sel] = ov / CMP_LEN
    return jnp.asarray(m, dtype=BF16)


def _block_expand(n_keys):
    e = np.zeros((LANES, n_keys), np.float32)
    k = np.arange(n_keys)
    e[k // SEL_BLOCK, k] = 1.0
    return jnp.asarray(e, dtype=BF16)


def _gate_expand():
    e = np.zeros((LANES, 3 * N_HEADS * HEAD_DIM), np.float32)
    for i in range(3):
        for h in range(N_HEADS):
            e[i * N_HEADS + h, i * N_HEADS * HEAD_DIM + h * HEAD_DIM:i * N_HEADS * HEAD_DIM + (h + 1) * HEAD_DIM] = 1.0
    return jnp.asarray(e, dtype=BF16)


def _head_expand():
    e = np.zeros((LANES, N_HEADS * HEAD_DIM), np.float32)
    for h in range(N_HEADS):
        e[LOGF_LANE + h, h * HEAD_DIM:(h + 1) * HEAD_DIM] = 1.0
    return jnp.asarray(e, dtype=BF16)


def _compress_weights(w1, w2, pe):
    half = CMP_LEN // 2
    eye = jnp.eye(G_NSA, dtype=F32)
    w = w1.reshape(2, half, HEAD_DIM, CMP_HID)
    w1x = jnp.einsum('pldh,gk->lgdpkh', w, eye).reshape(half * G_NSA * HEAD_DIM, 2 * G_NSA * CMP_HID).astype(BF16)
    w2x = jnp.einsum('hd,gk->ghkd', w2, eye).reshape(G_NSA * CMP_HID, G_NSA * HEAD_DIM).astype(BF16)
    pe2 = jnp.broadcast_to(pe.reshape(2, half, 1, HEAD_DIM), (2, half, G_NSA, HEAD_DIM)).reshape(2, -1)
    pex = jnp.concatenate([pe2, jnp.zeros((14, pe2.shape[1]), F32)], axis=0).astype(BF16)
    return w1x, w2x, pex


def _stored_tiles(cache):
    return jnp.transpose(cache, (0, 2, 3, 4, 1)).reshape(-1, cache.shape[1])


def kernel(x_prompt, x_sample, cache_nsa_cmp_kv, cache_nsa_slc_kv, cache_nsa_win_kv, cache_fox_kv, cache_fox_logf,
           page_table, g_norm, w_in, b_f, gq_a, gk_cmp, gk_slc, gk_win, pe_cmp_k, pe_cmp_v,
           w_cmp1_k, w_cmp2_k, w_cmp1_v, w_cmp2_v, gq_b, gk_b, w_out):
    depth = g_norm.shape[0]
    assert depth == 1, "single-layer stack"
    n_b, seq, d_model = x_prompt.shape
    n_db, s_new, _ = x_sample.shape
    pages_per_seq = page_table.shape[1]
    past = pages_per_seq * PAGE_SIZE
    wb = cache_nsa_win_kv.shape[2]
    n_pool = cache_nsa_cmp_kv.shape[1]
    d_a = N_HEADS * HEAD_DIM
    kv_w = 2 * G_NSA * HEAD_DIM

    w = w_in[0]
    o_kc, o_ga, o_za, o_qb, o_kb, o_fb, o_zb = 512, 1280, 1304, 1816, 2328, 3352, 3360
    gate_cols = o_ga + (np.arange(N_HEADS)[None, :] * 3 + np.arange(3)[:, None]).reshape(-1)
    small_w = jnp.concatenate([w[:, gate_cols], w[:, o_fb:o_fb + N_HEADS],
                               jnp.zeros((d_model, LANES - GATE_LANES - N_HEADS), F32)], axis=1)
    w_bf = jnp.concatenate([w[:, 0:o_ga], w[:, o_za:o_qb], w[:, o_zb:o_zb + d_a], w[:, o_qb:o_fb], small_w],
                           axis=1).astype(BF16)
    assert w_bf.shape[1] == C_END
    gains = jnp.concatenate([jnp.tile(v[0], 2)[None, :] for v in (gq_a, gk_slc, gk_win, gq_b, gk_b)]
                            + [jnp.zeros((3, LANES), F32)], axis=0)
    bf_row = jnp.zeros((1, LANES), F32).at[0, LOGF_LANE:LOGF_LANE + N_HEADS].set(b_f[0])
    lane_head = np.arange(LANES) // HEAD_DIM
    ones_bd = jnp.asarray(lane_head[:, None] == lane_head[None, :], dtype=BF16)
    hexp = _head_expand()
    gexp = _gate_expand()
    consts = lambda tm: (g_norm, w_bf, gains, bf_row, ones_bd,
                         jnp.asarray(np.tril(np.ones((tm, tm), np.float32)), dtype=BF16), hexp)

    tm_p = 256
    cos_p, sin_p = _rope_tables(jnp.arange(seq, dtype=jnp.int32))
    tiles_per_seq = seq // tm_p
    (cmp_p, slc_p, win_p, fox_p, small_p, z_p, qa_p, qb_p, cexp_p, csum_p, cmpd_p, slcb_p, winb_p, foxb_p) = _proj_call(
        x_prompt.reshape(n_b * seq, d_model), (cos_p, sin_p, lambda i: (i % tiles_per_seq, 0)),
        consts(tm_p), tiles_per_seq, tm_p, "proj_prompt", True)
    tm_s = _row_tile(n_db * s_new, 256)
    cos_s, sin_s = _rope_tables(past + jnp.arange(s_new, dtype=jnp.int32))
    cos_s, sin_s = jnp.tile(cos_s, (tm_s // s_new, 1)), jnp.tile(sin_s, (tm_s // s_new, 1))
    (cmp_s, slc_s, win_s, fox_s, small_s, z_s, qa_s, qb_s, _, _, _, _, _, _) = _proj_call(
        x_sample.reshape(n_db * s_new, d_model), (cos_s, sin_s, lambda i: (0, 0)),
        consts(tm_s), 1, tm_s, "proj_sample", False)

    w1k, w2k, pek = _compress_weights(w_cmp1_k[0], w_cmp2_k[0], pe_cmp_k[0])
    w1v, w2v, pev = _compress_weights(w_cmp1_v[0], w_cmp2_v[0], pe_cmp_v[0])
    cw = (w1k, w1v, pek, pev, w2k, w2v, jnp.tile(gk_cmp[0], 2)[None, :], ones_bd)
    chunk_w = CMP_STRIDE * kv_w
    ckv_p = _compress_prompt_call(cmpd_p.reshape(n_b * seq // CMP_STRIDE, chunk_w), cw, seq // CMP_STRIDE)
    pt_flat = page_table.reshape(-1).astype(jnp.int32)
    tok = np.arange(PAGE_SIZE)
    perm = np.zeros((PAGE_SIZE, PAGE_SIZE), np.float32)
    perm[(tok % CMP_STRIDE) * (PAGE_SIZE // CMP_STRIDE) + tok // CMP_STRIDE, tok] = 1.0
    ckv_s = _compress_sample_call(pt_flat, _stored_tiles(cache_nsa_cmp_kv[0]), jnp.asarray(perm, dtype=BF16), cw,
                                  n_db, pages_per_seq, 2)

    n_cmp_p = (seq - CMP_LEN) // CMP_STRIDE + 1
    msel_p = _cmp_to_sel(n_cmp_p, seq // SEL_BLOCK, seq // CMP_STRIDE, LANES)
    tq_a = 256
    eblk_p = jnp.transpose(_block_expand(seq).reshape(LANES, seq // tq_a, tq_a), (1, 0, 2))
    oa_p = _nsa_prompt_call(qa_p, small_p, ckv_p, slcb_p, winb_p, msel_p, eblk_p, gexp, n_b, seq, tq_a)
    n_cmp_s = (past + s_new - CMP_LEN) // CMP_STRIDE + 1
    n_sel_s = -(-(past + s_new) // SEL_BLOCK)
    msel_s = _cmp_to_sel(n_cmp_s, n_sel_s, past // CMP_STRIDE, LANES)
    eye_kv = jnp.asarray(np.eye(kv_w, dtype=np.float32), dtype=BF16)
    oa_s, new_win_t = _nsa_sample_call(
        pt_flat, qa_s, small_s, ckv_s, _stored_tiles(cache_nsa_slc_kv[0]), slc_s,
        _stored_tiles(cache_nsa_win_kv[0]), win_s, msel_s, _block_expand(past + PAGE_SIZE), gexp, eye_kv,
        n_db, s_new, pages_per_seq)
    new_win = jnp.transpose(new_win_t.reshape(n_db, 2, G_NSA, HEAD_DIM, wb), (0, 4, 1, 2, 3))[None]

    tq_f = 512
    c_p = csum_p[:, LOGF_LANE:LOGF_LANE + N_HEADS].reshape(n_b, seq, N_HEADS)
    ckt = jnp.transpose(c_p, (0, 2, 1)).reshape(n_b, N_HEADS, seq // tq_f, tq_f)
    ob_p = _fox_prompt_call(qb_p, foxb_p, cexp_p, ckt, n_b, seq, tq_f)
    psel = jnp.asarray(np.arange(LANES)[None, :] == (LOGF_LANE + np.arange(2 * N_HEADS))[:, None], dtype=BF16)
    psel = psel.at[N_HEADS:].set(0)
    triu = jnp.asarray(np.triu(np.ones((LANES, LANES), np.float32)), dtype=BF16)
    eye = jnp.asarray(np.eye(LANES, dtype=np.float32), dtype=BF16)
    ob_s = _fox_sample_call(
        pt_flat, qb_s, small_s, fox_s, _stored_tiles(cache_fox_kv[0]),
        jnp.transpose(cache_fox_logf[0], (0, 2, 1)), psel, triu, eye, n_db, s_new, pages_per_seq)

    w_out_bf = w_out[0].astype(BF16)
    y_p = _merge_call(x_prompt.reshape(n_b * seq, d_model), oa_p, ob_p, z_p, w_out_bf, 512, "merge_prompt")
    y_s = _merge_call(x_sample.reshape(n_db * s_new, d_model), oa_s, ob_s, z_s, w_out_bf,
                      _row_tile(n_db * s_new, 512), "merge_sample")

    kv5 = lambda a, nb, t, heads: a.reshape(1, nb, t, 2, heads, HEAD_DIM)
    kv5_t = lambda a, heads: jnp.transpose(a.reshape(n_b, 2, heads, HEAD_DIM, -1), (0, 4, 1, 2, 3))[None]
    wbp = min(WINDOW, seq)
    return (y_p.reshape(n_b, seq, d_model), y_s.reshape(n_db, s_new, d_model),
            kv5_t(cmp_p, G_NSA), kv5(cmp_s, n_db, s_new, G_NSA),
            kv5_t(slc_p, G_NSA), kv5(slc_s, n_db, s_new, G_NSA),
            kv5_t(win_p.reshape(n_b, kv_w, seq)[:, :, seq - wbp:], G_NSA), new_win,
            kv5_t(fox_p, N_HEADS), kv5(fox_s, n_db, s_new, N_HEADS),
            small_p[:, LOGF_LANE:LOGF_LANE + N_HEADS].reshape(1, n_b, seq, N_HEADS),
            small_s[:, LOGF_LANE:LOGF_LANE + N_HEADS].reshape(1, n_db, s_new, N_HEADS))
```

```python
import functools

import numpy as np
import jax
import jax.numpy as jnp
from jax import lax
from jax.experimental import pallas as pl
from jax.experimental.pallas import tpu as pltpu

F32 = jnp.float32
BF16 = jnp.bfloat16

HEAD_DIM = 64
LANES = 128
N_HEADS = 8
G_NSA = 2
R_NSA = N_HEADS // G_NSA
CMP_LEN = 32
CMP_STRIDE = 16
CMP_HID = 2 * HEAD_DIM
SEL_BLOCK = 64
SEL_TOP = 16
WINDOW = 512
PAGE_SIZE = 128
ROPE_THETA = 10000.0
EPS = 1e-6
NEG = -1e30
FORCED = 1e4
QK_SCALE = HEAD_DIM ** -0.5
LOG2_E = 1.4426950408889634

VMEM_LIMIT = 56 * 1024 * 1024

C_QA, C_CMP, C_SLC, C_WIN, C_Z, C_QB, C_FOX, C_SMALL, C_END = 0, 512, 768, 1024, 1280, 2304, 2816, 3840, 3968
GATE_LANES = 3 * N_HEADS
LOGF_LANE = GATE_LANES


def _dot(a, b):
    return jnp.dot(a, b, preferred_element_type=F32)


def _dot_nt(a, b):
    return lax.dot_general(a, b, (((1,), (1,)), ((), ())), preferred_element_type=F32)


def _split2(x):
    hi = x.astype(BF16)
    lo = (x - hi.astype(F32)).astype(BF16)
    return hi, lo


def _split3(x):
    hi = x.astype(BF16)
    r = x - hi.astype(F32)
    mid = r.astype(BF16)
    lo = (r - mid.astype(F32)).astype(BF16)
    return hi, mid, lo


def _dot_exact_lhs(x, m):
    a, b, c = _split3(x)
    return _dot(a, m) + _dot(b, m) + _dot(c, m)


def _dot_exact_rhs(m, x):
    a, b, c = _split3(x)
    return _dot(m, a) + _dot(m, b) + _dot(m, c)


def _sigmoid(x):
    return 1.0 / (1.0 + jnp.exp(-x))


def _params(sem):
    return pltpu.CompilerParams(dimension_semantics=sem, vmem_limit_bytes=VMEM_LIMIT)


def _proj_kernel(x_ref, g_ref, w_ref, cos_ref, sin_ref, gains_ref, bf_ref, ones_ref, tri_ref, hexp_ref,
                 cmp_ref, slc_ref, win_ref, fox_ref, small_ref, z_ref, qa_ref, qb_ref, cexp_ref, csum_ref, cmpd_ref,
                 slcb_ref, winb_ref, foxb_ref, carry_ref, *, tm, tiles_per_seq, kv_t, logit_scale):
    i = pl.program_id(0)
    qb_scale = QK_SCALE * logit_scale
    x = x_ref[...]
    ms = jnp.mean(x * x, axis=-1, keepdims=True)
    h = (x * lax.rsqrt(ms + EPS) * g_ref[...]).astype(BF16)
    cos = cos_ref[...]
    sin = sin_ref[...]
    lane = lax.broadcasted_iota(jnp.int32, (tm, LANES), 1)
    first_half = (lane & (HEAD_DIM // 2)) == 0
    ones = ones_ref[...]

    def proj(c0, c1):
        return _dot(h, w_ref[:, c0:c1])

    def sumsq_by_head(y):
        hi, lo = _split2(y * y)
        m = ones if y.shape[1] == 2 * LANES else ones[0:LANES, 0:LANES]
        return _dot(hi, m) + _dot(lo, m)

    def head_norm(y, row, ss=None):
        if ss is None:
            ss = sumsq_by_head(y)
        return y * lax.rsqrt(ss * (1.0 / HEAD_DIM) + EPS) * gains_ref[row:row + 1, :]

    def head_norm_chunks(y, row):
        out = []
        for s in range(2):
            ss2 = sumsq_by_head(y[:, 2 * s * LANES:2 * (s + 1) * LANES])
            for c in (2 * s, 2 * s + 1):
                out.append(head_norm(chunk(y, c), row, ss2[:, (c - 2 * s) * LANES:(c - 2 * s + 1) * LANES]))
        return out

    def rope(y):
        partner = jnp.where(first_half, pltpu.roll(y, LANES - HEAD_DIM // 2, 1), pltpu.roll(y, HEAD_DIM // 2, 1))
        return y * cos + partner * sin

    def chunk(y, c):
        return y[:, c * LANES:(c + 1) * LANES]

    def with_ones(v, e):
        return jnp.where((lane < HEAD_DIM) if e == 0 else (lane >= HEAD_DIM), v, 1.0)

    y = proj(C_QA, C_CMP)
    for c, t in enumerate(head_norm_chunks(y, 0)):
        qa_ref[:, c * LANES:(c + 1) * LANES] = rope(t) * QK_SCALE

    def store_kv(o_ref, t, c):
        if kv_t:
            o_ref[c * LANES:(c + 1) * LANES, :] = t.T
        else:
            o_ref[:, c * LANES:(c + 1) * LANES] = t

    y = proj(C_CMP, C_SLC)
    k = rope(chunk(y, 0))
    v = chunk(y, 1)
    cmpd_ref[:, 0:LANES] = k
    cmpd_ref[:, LANES:2 * LANES] = v
    store_kv(cmp_ref, k, 0)
    store_kv(cmp_ref, v, 1)

    for (c0, c1, row, o_ref, ob_ref) in ((C_SLC, C_WIN, 1, slc_ref, slcb_ref), (C_WIN, C_Z, 2, win_ref, winb_ref)):
        y = proj(c0, c1)
        k = rope(head_norm(chunk(y, 0), row))
        v = chunk(y, 1)
        store_kv(o_ref, k, 0)
        store_kv(o_ref, v, 1)
        ob_ref[:, 0:LANES] = k.astype(BF16)
        for e in range(2):
            ob_ref[:, (1 + e) * LANES:(2 + e) * LANES] = with_ones(v, e).astype(BF16)

    z_ref[...] = proj(C_Z, C_QB)

    y = proj(C_QB, C_FOX)
    for c, t in enumerate(head_norm_chunks(y, 3)):
        qb_ref[:, c * LANES:(c + 1) * LANES] = t * qb_scale

    y = proj(C_FOX, C_SMALL)
    kb = head_norm_chunks(y[:, 0:N_HEADS * HEAD_DIM], 4)
    for c in range(8):
        t = kb[c] if c < 4 else chunk(y, c)
        store_kv(fox_ref, t, c)
        if c < 4:
            foxb_ref[:, c * LANES:(c + 1) * LANES] = t.astype(BF16)
        else:
            for e in range(2):
                foxb_ref[:, (c + 4 * e) * LANES:(c + 4 * e + 1) * LANES] = with_ones(t, e).astype(BF16)

    raw = proj(C_SMALL, C_END)
    zf = raw + bf_ref[...]
    logf = jnp.minimum(zf, 0.0) - jnp.log1p(jnp.exp(-jnp.abs(zf)))
    small = jnp.where(lane < GATE_LANES, _sigmoid(raw), jnp.where(lane < GATE_LANES + N_HEADS, logf, 0.0))
    small_ref[...] = small

    @pl.when(i % tiles_per_seq == 0)
    def _():
        carry_ref[...] = jnp.zeros_like(carry_ref)

    c = carry_ref[0:1, :] + _dot_exact_rhs(tri_ref[...], small)
    carry_ref[0:1, :] = c[tm - 1:tm, :]
    c = c * logit_scale
    csum_ref[...] = c
    cexp_ref[...] = _dot_exact_lhs(c, hexp_ref[...])


def _proj_call(x2d, tables, consts, tiles_per_seq, tm, name, kv_t, logit_scale):
    n = x2d.shape[0]
    cos_t, sin_t, table_map = tables
    g_norm, w_bf, gains, bf_row, ones_bd, tri, hexp = consts
    row = lambda i: (i, 0)
    fixed = lambda i: (0, 0)
    full = lambda a: pl.BlockSpec(a.shape, fixed)
    kv_widths = (256, 256, 256, 1024)
    out_widths = (128, 1024, 512, 512, 512, 128, 256)
    bf_widths = (3 * LANES, 3 * LANES, 3 * N_HEADS * HEAD_DIM)
    if kv_t:
        seq = tiles_per_seq * tm
        out_shape = [jax.ShapeDtypeStruct((n // seq * w, seq), F32) for w in kv_widths]
        out_specs = [pl.BlockSpec((w, tm), lambda i: (i // tiles_per_seq, i % tiles_per_seq)) for w in kv_widths]
    else:
        out_shape = [jax.ShapeDtypeStruct((n, w), F32) for w in kv_widths]
        out_specs = [pl.BlockSpec((tm, w), row) for w in kv_widths]
    out_shape += [jax.ShapeDtypeStruct((n, w), F32) for w in out_widths]
    out_shape += [jax.ShapeDtypeStruct((n, w), BF16) for w in bf_widths]
    out_specs += [pl.BlockSpec((tm, w), row) for w in out_widths + bf_widths]
    return pl.pallas_call(
        functools.partial(_proj_kernel, tm=tm, tiles_per_seq=tiles_per_seq, kv_t=kv_t, logit_scale=logit_scale),
        grid=(n // tm,),
        in_specs=[pl.BlockSpec((tm, x2d.shape[1]), row), full(g_norm), full(w_bf),
                  pl.BlockSpec((tm, LANES), table_map), pl.BlockSpec((tm, LANES), table_map),
                  full(gains), full(bf_row), full(ones_bd), full(tri), full(hexp)],
        out_specs=out_specs,
        out_shape=out_shape,
        scratch_shapes=[pltpu.VMEM((8, LANES), F32)],
        compiler_params=_params(("arbitrary",)),
        name=name,
    )(x2d, g_norm, w_bf, cos_t, sin_t, gains, bf_row, ones_bd, tri, hexp)


def _compress_body(chunk_rows, m, wk_ref, wv_ref, pek_ref, pev_ref, w2k_ref, w2v_ref, gain_ref, ones_ref, out_ref,
                   rows_per_seq):
    hid = G_NSA * CMP_HID

    def branch(kind, w1_ref, pe_ref, w2_ref):
        r = _dot(jnp.concatenate([chunk_rows(kind), pe_ref[...]], axis=0), w1_ref[...])
        hh = r[:m, :hid] + pltpu.roll(r[:m, hid:], m - 1, 0) + r[m:m + 1, :hid] + r[m + 1:m + 2, hid:]
        act = hh * _sigmoid(hh)
        return _dot(act.astype(BF16), w2_ref[...])

    kc = branch(0, wk_ref, pek_ref, w2k_ref)
    hi, lo = _split2(kc * kc)
    ss = _dot(hi, ones_ref[...]) + _dot(lo, ones_ref[...])
    kc = kc * lax.rsqrt(ss * (1.0 / HEAD_DIM) + EPS) * gain_ref[...]
    vc = branch(1, wv_ref, pev_ref, w2v_ref)
    rows = lax.broadcasted_iota(jnp.int32, (m, LANES), 0)
    valid = (rows % rows_per_seq) != rows_per_seq - 1
    out_ref[:, 0:LANES] = jnp.where(valid, kc, 0.0)
    out_ref[:, LANES:2 * LANES] = jnp.where(valid, vc, 0.0)


def _compress_prompt_kernel(y_ref, *rest, rows_per_seq):
    y = y_ref[...].astype(BF16)
    width = 2 * G_NSA * HEAD_DIM

    def chunk_rows(kind):
        return jnp.concatenate([y[:, l * width + kind * LANES:l * width + (kind + 1) * LANES]
                                for l in range(CMP_STRIDE)], axis=1)

    _compress_body(chunk_rows, y.shape[0], *rest, rows_per_seq=rows_per_seq)


def _compress_sample_kernel(pt_ref, *refs, n_pages, rows_per_seq):
    del pt_ref
    page_refs = refs[:n_pages]
    perm_ref = refs[n_pages]
    chunks = PAGE_SIZE // CMP_STRIDE

    def chunk_rows(kind):
        tiles = [_dot_nt(perm_ref[...], r[kind * LANES:(kind + 1) * LANES, :].astype(BF16)) for r in page_refs]
        return jnp.concatenate([jnp.concatenate([t[l * chunks:(l + 1) * chunks] for t in tiles], axis=0)
                                for l in range(CMP_STRIDE)], axis=1).astype(BF16)

    _compress_body(chunk_rows, n_pages * chunks, *refs[n_pages + 1:], rows_per_seq=rows_per_seq)


def _compress_prompt_call(y2d, cw, rows_per_seq):
    n = y2d.shape[0]
    fixed = lambda i: (0, 0)
    return pl.pallas_call(
        functools.partial(_compress_prompt_kernel, rows_per_seq=rows_per_seq),
        grid=(n // rows_per_seq,),
        in_specs=[pl.BlockSpec((rows_per_seq, y2d.shape[1]), lambda i: (i, 0))] + [pl.BlockSpec(a.shape, fixed) for a in cw],
        out_specs=pl.BlockSpec((rows_per_seq, 2 * LANES), lambda i: (i, 0)),
        out_shape=jax.ShapeDtypeStruct((n, 2 * LANES), F32),
        compiler_params=_params(("arbitrary",)),
        name="compress_prompt",
    )(y2d, *cw)


def _compress_sample_call(pt_flat, pool_t, perm, cw, n_batch, pages_per_seq, batch_per_step):
    kv_rows = 2 * G_NSA * HEAD_DIM
    rows_per_seq = pages_per_seq * (PAGE_SIZE // CMP_STRIDE)
    n_pages = batch_per_step * pages_per_seq
    fixed = lambda i, pt: (0, 0)

    def page_spec(k):
        bb, j = divmod(k, pages_per_seq)
        return pl.BlockSpec((kv_rows, PAGE_SIZE), lambda i, pt: (pt[(i * batch_per_step + bb) * pages_per_seq + j], 0))

    m = batch_per_step * rows_per_seq
    grid_spec = pltpu.PrefetchScalarGridSpec(
        num_scalar_prefetch=1,
        grid=(n_batch // batch_per_step,),
        in_specs=[page_spec(k) for k in range(n_pages)] + [pl.BlockSpec(a.shape, fixed) for a in (perm,) + tuple(cw)],
        out_specs=pl.BlockSpec((m, 2 * LANES), lambda i, pt: (i, 0)),
    )
    return pl.pallas_call(
        functools.partial(_compress_sample_kernel, n_pages=n_pages, rows_per_seq=rows_per_seq),
        grid_spec=grid_spec,
        out_shape=jax.ShapeDtypeStruct((n_batch * rows_per_seq, 2 * LANES), F32),
        compiler_params=_params(("arbitrary",)),
        name="compress_sample",
    )(pt_flat, *([pool_t] * n_pages), perm, *cw)


def _group_queries(qa, g, lane):
    in_group = (lane < HEAD_DIM) if g == 0 else (lane >= HEAD_DIM)
    rows = []
    for r in range(R_NSA):
        head = R_NSA * g + r
        y = qa[:, (head // 2) * LANES:(head // 2 + 1) * LANES]
        if head % 2 != g:
            y = pltpu.roll(y, HEAD_DIM, 1)
        rows.append(jnp.where(in_group, y, 0.0))
    return jnp.concatenate(rows, axis=0)


def _assemble_heads(res, tq, lane):
    chunks = []
    for c in range(N_HEADS // 2):
        g = c // 2
        a = res[g][(2 * (c % 2)) * tq:(2 * (c % 2) + 1) * tq]
        b = res[g][(2 * (c % 2) + 1) * tq:(2 * (c % 2) + 2) * tq]
        if g == 0:
            b = pltpu.roll(b, HEAD_DIM, 1)
        else:
            a = pltpu.roll(a, HEAD_DIM, 1)
        chunks.append(jnp.where(lane < HEAD_DIM, a, b))
    return jnp.concatenate(chunks, axis=1)


def _flash_step(s, n_rb, rb, fix, values, m_ref, acc_ref, base2=False):
    exp = jnp.exp2 if base2 else jnp.exp
    n_c = s.shape[1] // LANES
    cols = [jnp.concatenate([fix(r, c, s[r * rb:(r + 1) * rb, c * LANES:(c + 1) * LANES]) for r in range(n_rb)], axis=0)
            for c in range(n_c)]
    mx = cols[0]
    for c in range(1, n_c):
        mx = jnp.maximum(mx, cols[c])
    m_prev = m_ref[...]
    m_new = jnp.maximum(m_prev, jnp.max(mx, axis=-1, keepdims=True))
    alpha = exp(m_prev - m_new)
    p = jnp.concatenate([exp(x - m_new) for x in cols], axis=1).astype(BF16)
    if len(values) == 1:
        pv = _dot(p, values[0])
    else:
        rows_per_v = n_rb * rb // len(values)
        pv = jnp.concatenate([_dot(p[i * rows_per_v:(i + 1) * rows_per_v], v) for i, v in enumerate(values)], axis=0)
    acc_ref[...] = alpha * acc_ref[...] + pv
    m_ref[...] = m_new


def _flash_init(m_ref, acc_ref):
    m_ref[...] = jnp.full(m_ref.shape, NEG, F32)
    acc_ref[...] = jnp.zeros(acc_ref.shape, F32)


def _flash_result(acc_ref):
    acc = acc_ref[...]
    return acc / pltpu.roll(acc, HEAD_DIM, 1)


def _block_ranks(impm, n_blocks, lane):
    cnt = jnp.zeros(impm.shape, F32)
    for j in range(n_blocks):
        col = impm[:, j:j + 1]
        ge = jnp.where(col >= impm, 1.0, 0.0)
        gt = jnp.where(col > impm, 1.0, 0.0)
        cnt = cnt + jnp.where(lane > j, ge, gt)
    return cnt


def _compressed_branch(qz, ck, cv, cmask, msel, n_heads, tq):
    q_hi, q_lo = _split2(qz)
    k_hi, k_lo = _split2(ck)
    s = _dot_nt(q_hi, k_hi) + _dot_nt(q_hi, k_lo) + _dot_nt(q_lo, k_hi)
    ps = []
    psum = None
    for r in range(n_heads):
        sr = jnp.where(cmask, s[r * tq:(r + 1) * tq], NEG)
        mx = jnp.max(sr, axis=-1, keepdims=True)
        e = jnp.where(cmask, jnp.exp(sr - mx), 0.0)
        den = jnp.sum(e, axis=-1, keepdims=True)
        p = e / jnp.where(den > 0.0, den, 1.0)
        ps.append(p)
        psum = p if psum is None else psum + p
    o = _dot(jnp.concatenate(ps, axis=0).astype(BF16), cv)
    return o, _dot_exact_lhs(psum, msel)


def _select_blocks(imp, tpos, lane, n_blocks):
    forced = (lane == 0) | (lane == tpos // SEL_BLOCK)
    causal = lane * SEL_BLOCK <= tpos
    impm = jnp.where(forced, FORCED, jnp.where(causal, imp, -1.0))
    impm = jnp.where(lane < n_blocks, impm, -3e38)
    cnt = _block_ranks(impm, n_blocks, lane)
    n_top = min(SEL_TOP, n_blocks)
    return jnp.where(cnt < n_top, jnp.where(lane < n_blocks, 1.0, 0.0), 0.0)


def _select_blocks_t(imp, s0, n_blocks, cnt_ref):
    tq = imp.shape[0]
    grp = 8
    imp_t = jnp.transpose(imp)[0:n_blocks]
    blk = lax.broadcasted_iota(jnp.int32, (n_blocks, tq), 0)
    tpos = s0 + lax.broadcasted_iota(jnp.int32, (n_blocks, tq), 1)
    forced = (blk == 0) | (blk == tpos // SEL_BLOCK)
    impm = jnp.where(forced, FORCED, jnp.where(blk * SEL_BLOCK <= tpos, imp_t, -1.0))
    cnt_ref[...] = jnp.zeros(cnt_ref.shape, F32)
    sub = lax.broadcasted_iota(jnp.int32, (grp, tq), 0)
    for jb in range(n_blocks // grp):
        @pl.when(jb * grp * SEL_BLOCK <= s0 + tq - 1)
        def _(jb=jb):
            for rg in range(n_blocks // grp):
                x = impm[rg * grp:(rg + 1) * grp]
                cnt = cnt_ref[rg * grp:(rg + 1) * grp, :]
                for j in range(jb * grp, (jb + 1) * grp):
                    row = impm[j:j + 1, :]
                    if rg > jb:
                        beat = row >= x
                    elif rg < jb:
                        beat = row > x
                    else:
                        cnt = cnt + jnp.where(sub > j - jb * grp, jnp.where(row >= x, 1.0, 0.0),
                                              jnp.where(row > x, 1.0, 0.0))
                        continue
                    cnt = cnt + jnp.where(beat, 1.0, 0.0)
                cnt_ref[rg * grp:(rg + 1) * grp, :] = cnt
    sel_t = jnp.where(cnt_ref[...] < min(SEL_TOP, n_blocks), 1.0, 0.0)
    return jnp.transpose(jnp.concatenate([sel_t, jnp.zeros((LANES - n_blocks, tq), F32)], axis=0))


def _nsa_prompt_kernel(qa_ref, small_ref, cmp_ref, slc_ref, win_ref, msel_ref, eblk_ref, gexp_ref,
                       oa_ref, m_ref, acc_ref, cnt_ref, *, tq, n_sel):
    tk = tq
    qi = pl.program_id(1)
    s0 = qi * tq
    lane = lax.broadcasted_iota(jnp.int32, (tq, LANES), 1)
    trow = lax.broadcasted_iota(jnp.int32, (tq, LANES), 0)
    tpos = s0 + trow
    row_t = lax.broadcasted_iota(jnp.int32, (tq, tk), 0)
    col_t = lax.broadcasted_iota(jnp.int32, (tq, tk), 1)
    diag_mask = col_t <= row_t
    n_back = WINDOW // tk
    assert n_back * tk == WINDOW
    far_mask = (n_back * tk + row_t - col_t) <= WINDOW
    n_cmp = cmp_ref.shape[0]
    cmask = (lax.broadcasted_iota(jnp.int32, (tq, n_cmp), 1) * CMP_STRIDE + CMP_LEN - 1
             <= s0 + lax.broadcasted_iota(jnp.int32, (tq, n_cmp), 0))
    qa = qa_ref[...]
    ck = cmp_ref[:, 0:LANES]
    cv = cmp_ref[:, LANES:2 * LANES].astype(BF16)

    def chunk_of(mask, c):
        return mask[:, c * LANES:(c + 1) * LANES]

    o_cmp, sel = [], []
    qz = [_group_queries(qa, g, lane) for g in range(G_NSA)]
    for g in range(G_NSA):
        oc, imp = _compressed_branch(qz[g], ck, cv, cmask, msel_ref[...], R_NSA, tq)
        o_cmp.append(oc)
        sel.append(_select_blocks_t(imp, s0, n_sel, cnt_ref).astype(BF16))
    q_all = jnp.concatenate(qz, axis=0).astype(BF16)

    def flash_step(kv_ref, kt, biases, mask):
        off = pl.multiple_of(kt * tk, tk)
        k2 = kv_ref[pl.ds(off, tk), 0:LANES]
        values = [kv_ref[pl.ds(off, tk), (1 + g) * LANES:(2 + g) * LANES] for g in range(G_NSA)]

        def fix(r, c, x):
            if biases is not None:
                x = x + chunk_of(biases[r // R_NSA], c)
            if mask is not None:
                x = jnp.where(chunk_of(mask, c), x, NEG)
            return x

        _flash_step(_dot_nt(q_all, k2), G_NSA * R_NSA, tq, fix, values, m_ref, acc_ref)

    def sel_biases(kt):
        return [(_dot(sel[g], eblk_ref[kt]) - 1.0) * (-NEG) for g in range(G_NSA)]

    def per_group(o):
        return [o[g * R_NSA * tq:(g + 1) * R_NSA * tq] for g in range(G_NSA)]

    _flash_init(m_ref, acc_ref)

    def sel_body(kt, carry):
        flash_step(slc_ref, kt, sel_biases(kt), None)
        return carry

    lax.fori_loop(0, qi, sel_body, 0)
    flash_step(slc_ref, qi, sel_biases(qi), diag_mask)
    o_sel = per_group(_flash_result(acc_ref))

    _flash_init(m_ref, acc_ref)

    for back in range(n_back, 0, -1):
        @pl.when(qi >= back)
        def _(back=back):
            flash_step(win_ref, qi - back, None, far_mask if back == n_back else None)

    flash_step(win_ref, qi, None, diag_mask)
    o_win = per_group(_flash_result(acc_ref))

    gates = _dot_exact_lhs(small_ref[...], gexp_ref[...])
    d_a = N_HEADS * HEAD_DIM
    oa_ref[...] = (gates[:, 0:d_a] * _assemble_heads(o_cmp, tq, lane)
                   + gates[:, d_a:2 * d_a] * _assemble_heads(o_sel, tq, lane)
                   + gates[:, 2 * d_a:3 * d_a] * _assemble_heads(o_win, tq, lane))


def _nsa_prompt_call(qa, small, cmpkv, slcb, winb, msel, eblk, gexp, n_batch, seq, tq):
    nq = seq // tq
    n_cmp_rows = cmpkv.shape[0] // n_batch
    n_sel = seq // SEL_BLOCK
    qrow = lambda b, q: (b * nq + q, 0)
    per_b = lambda b, q: (b, 0)
    fixed = lambda b, q: (0, 0)
    return pl.pallas_call(
        functools.partial(_nsa_prompt_kernel, tq=tq, n_sel=n_sel),
        grid=(n_batch, nq),
        in_specs=[pl.BlockSpec((tq, qa.shape[1]), qrow), pl.BlockSpec((tq, LANES), qrow),
                  pl.BlockSpec((n_cmp_rows, 2 * LANES), per_b),
                  pl.BlockSpec((seq, 3 * LANES), per_b), pl.BlockSpec((seq, 3 * LANES), per_b),
                  pl.BlockSpec(msel.shape, fixed), pl.BlockSpec(eblk.shape, lambda b, q: (0, 0, 0)),
                  pl.BlockSpec(gexp.shape, fixed)],
        out_specs=pl.BlockSpec((tq, qa.shape[1]), qrow),
        out_shape=jax.ShapeDtypeStruct(qa.shape, F32),
        scratch_shapes=[pltpu.VMEM((G_NSA * R_NSA * tq, LANES), F32), pltpu.VMEM((G_NSA * R_NSA * tq, LANES), F32),
                        pltpu.VMEM((n_sel, tq), F32)],
        compiler_params=_params(("arbitrary", "arbitrary")),
        name="nsa_prompt",
    )(qa, small, cmpkv, slcb, winb, msel, eblk, gexp)


def _fox_prompt_kernel(q_ref, k_ref, v0_ref, v1_ref, cq_ref, ck_ref, o_ref, m_ref, acc_ref, *, tq):
    tk = tq
    qi = pl.program_id(2)
    lane = lax.broadcasted_iota(jnp.int32, (tq, LANES), 1)
    q = q_ref[...]
    qz = jnp.concatenate([jnp.where(lane < HEAD_DIM, q, 0.0), jnp.where(lane >= HEAD_DIM, q, 0.0)], axis=0).astype(BF16)
    cq = cq_ref[...]
    cq_sw = pltpu.roll(cq, HEAD_DIM, 1)
    cq_rep = [jnp.where(lane < HEAD_DIM, cq, cq_sw), jnp.where(lane >= HEAD_DIM, cq, cq_sw)]
    diag_mask = lax.broadcasted_iota(jnp.int32, (tq, tk), 1) <= lax.broadcasted_iota(jnp.int32, (tq, tk), 0)
    v_refs = (v0_ref, v1_ref)
    _flash_init(m_ref, acc_ref)

    def step(kt, mask):
        off = pl.multiple_of(kt * tk, tk)
        k2 = k_ref[pl.ds(off, tk), :]
        ck = [ck_ref[0, e, pl.ds(kt, 1), :] for e in range(2)]

        def fix(e, c, x):
            x = x + (cq_rep[e] - ck[e][:, c * LANES:(c + 1) * LANES])
            if mask is not None:
                x = jnp.where(mask[:, c * LANES:(c + 1) * LANES], x, NEG)
            return x

        _flash_step(_dot_nt(qz, k2), 2, tq, fix, [r[pl.ds(off, tk), :] for r in v_refs], m_ref, acc_ref, base2=True)

    def body(kt, carry):
        step(kt, None)
        return carry

    lax.fori_loop(0, qi, body, 0)
    step(qi, diag_mask)
    o = _flash_result(acc_ref)
    o_ref[...] = jnp.where(lane < HEAD_DIM, o[0:tq], o[tq:2 * tq])


def _fox_prompt_call(qb, foxb, cexp, ckt, n_batch, seq, tq):
    nq = seq // tq
    n_pairs = N_HEADS // 2
    qmap = lambda b, hp, q: (b * nq + q, hp)
    return pl.pallas_call(
        functools.partial(_fox_prompt_kernel, tq=tq),
        grid=(n_batch, n_pairs, nq),
        in_specs=[pl.BlockSpec((tq, LANES), qmap),
                  pl.BlockSpec((seq, LANES), lambda b, hp, q: (b, hp)),
                  pl.BlockSpec((seq, LANES), lambda b, hp, q: (b, n_pairs + hp)),
                  pl.BlockSpec((seq, LANES), lambda b, hp, q: (b, 2 * n_pairs + hp)),
                  pl.BlockSpec((tq, LANES), qmap),
                  pl.BlockSpec((1, 2, nq, tq), lambda b, hp, q: (b, hp, 0, 0))],
        out_specs=pl.BlockSpec((tq, LANES), qmap),
        out_shape=jax.ShapeDtypeStruct(qb.shape, F32),
        scratch_shapes=[pltpu.VMEM((2 * tq, LANES), F32), pltpu.VMEM((2 * tq, LANES), F32)],
        compiler_params=_params(("arbitrary", "arbitrary", "arbitrary")),
        name="fox_prompt",
    )(qb, foxb, foxb, foxb, cexp, ckt)


def _pad_rows(x, rows):
    return jnp.concatenate([x, jnp.zeros((rows - x.shape[0], x.shape[1]), x.dtype)], axis=0)


def _nsa_sample_kernel(pt_ref, qa_ref, small_ref, cmp_ref, *refs, n_pages, n_seq, past, n_sel):
    del pt_ref
    all_pages = refs[:n_pages * n_seq]
    (slcn_ref, winbuf_ref, winn_ref, msel_ref, eblk_ref, gexp_ref, eye_ref, oa_ref, newwin_ref) = refs[n_pages * n_seq:]
    s_new = qa_ref.shape[0] // n_seq
    wb = winbuf_ref.shape[1]
    kd = G_NSA * HEAD_DIM
    n_keys = past + PAGE_SIZE
    lane = lax.broadcasted_iota(jnp.int32, (s_new, LANES), 1)
    tpos = past + lax.broadcasted_iota(jnp.int32, (s_new, LANES), 0)
    n_cmp = cmp_ref.shape[0] // n_seq
    cmask = (lax.broadcasted_iota(jnp.int32, (s_new, n_cmp), 1) * CMP_STRIDE + CMP_LEN - 1
             <= past + lax.broadcasted_iota(jnp.int32, (s_new, n_cmp), 0))
    kpos = lax.broadcasted_iota(jnp.int32, (s_new, n_keys), 1)
    key_ok = kpos <= past + lax.broadcasted_iota(jnp.int32, (s_new, n_keys), 0)
    w_keys = wb + PAGE_SIZE
    wi = lax.broadcasted_iota(jnp.int32, (s_new, w_keys), 1)
    wt = lax.broadcasted_iota(jnp.int32, (s_new, w_keys), 0)
    dist = wb + wt - wi
    win_ok = (dist >= 0) & (dist <= WINDOW) & (past - wb + wi >= 0) & (wi < wb + s_new)
    lane_w = lax.broadcasted_iota(jnp.int32, (2 * kd, LANES), 1)
    for seq_i in range(n_seq):
        _nsa_sample_one(seq_i, all_pages[seq_i * n_pages:(seq_i + 1) * n_pages], qa_ref, small_ref, cmp_ref, slcn_ref,
                        winbuf_ref, winn_ref, msel_ref, eblk_ref, gexp_ref, eye_ref, oa_ref, newwin_ref,
                        s_new, n_cmp, n_sel, lane, tpos, cmask, key_ok, win_ok, lane_w)


def _nsa_sample_one(seq_i, page_refs, qa_ref, small_ref, cmp_ref, slcn_ref, winbuf_ref, winn_ref, msel_ref, eblk_ref,
                    gexp_ref, eye_ref, oa_ref, newwin_ref, s_new, n_cmp, n_sel, lane, tpos, cmask, key_ok, win_ok, lane_w):
    kd = G_NSA * HEAD_DIM
    wb = winbuf_ref.shape[1]
    tok = slice(seq_i * s_new, (seq_i + 1) * s_new)
    w0 = seq_i * 2 * kd
    qa = qa_ref[tok, :]
    ck = cmp_ref[seq_i * n_cmp:(seq_i + 1) * n_cmp, 0:LANES]
    cv = cmp_ref[seq_i * n_cmp:(seq_i + 1) * n_cmp, LANES:2 * LANES].astype(BF16)
    slcn = slcn_ref[tok, :]
    winn = winn_ref[tok, :]
    qz = [_group_queries(qa, g, lane) for g in range(G_NSA)]
    q_all = jnp.concatenate(qz, axis=0).astype(BF16)

    def attend(kt, vt, k_new, v_new, masks):
        n_past = kt.shape[1]
        s = jnp.concatenate([_dot(q_all, kt), _dot_nt(q_all, k_new)], axis=1)
        s = jnp.concatenate([jnp.where(masks[g], s[(g * R_NSA + r) * s_new:(g * R_NSA + r + 1) * s_new], NEG)
                             for g in range(G_NSA) for r in range(R_NSA)], axis=0)
        mx = jnp.max(s, axis=-1, keepdims=True)
        p = jnp.exp(s - mx)
        den = jnp.sum(p, axis=-1, keepdims=True)
        pb = p.astype(BF16)
        return (_dot_nt(pb[:, :n_past], vt) + _dot(pb[:, n_past:], v_new)) / den

    def new_rows(x):
        return _pad_rows(x, PAGE_SIZE).astype(BF16)

    o_cmp, sel_ok = [], []
    for g in range(G_NSA):
        oc, imp = _compressed_branch(qz[g], ck, cv, cmask, msel_ref[...], R_NSA, s_new)
        o_cmp.append(oc)
        sel = _select_blocks(imp, tpos, lane, n_sel).astype(BF16)
        sel_ok.append((_dot(sel, eblk_ref[...]) > 0.5) & key_ok)

    kt = jnp.concatenate([r[0:kd, :] for r in page_refs], axis=1).astype(BF16)
    vt = jnp.concatenate([r[kd:2 * kd, :] for r in page_refs], axis=1).astype(BF16)
    o_s = attend(kt, vt, new_rows(slcn[:, 0:LANES]), new_rows(slcn[:, LANES:2 * LANES]), sel_ok)
    o_w = attend(winbuf_ref[w0:w0 + kd, :].astype(BF16), winbuf_ref[w0 + kd:w0 + 2 * kd, :].astype(BF16),
                 new_rows(winn[:, 0:LANES]), new_rows(winn[:, LANES:2 * LANES]), [win_ok] * G_NSA)
    half = R_NSA * s_new
    o_sel = [o_s[0:half], o_s[half:2 * half]]
    o_win = [o_w[0:half], o_w[half:2 * half]]

    gates = _dot_exact_lhs(small_ref[tok, :], gexp_ref[...])
    d_a = N_HEADS * HEAD_DIM
    oa_ref[tok, :] = (gates[:, 0:d_a] * _assemble_heads(o_cmp, s_new, lane)
                      + gates[:, d_a:2 * d_a] * _assemble_heads(o_sel, s_new, lane)
                      + gates[:, 2 * d_a:3 * d_a] * _assemble_heads(o_win, s_new, lane))

    xn = jnp.concatenate([jnp.zeros((LANES - s_new, 2 * LANES), F32), winn], axis=0)
    a, b, c = _split3(xn)
    eye = eye_ref[...]
    new_t = _dot_nt(eye, a) + _dot_nt(eye, b) + _dot_nt(eye, c)
    n_chunks = wb // LANES
    shifted = [pltpu.roll(winbuf_ref[w0:w0 + 2 * kd, c * LANES:(c + 1) * LANES], LANES - s_new, 1)
               for c in range(n_chunks)]
    shifted.append(new_t)
    for c in range(n_chunks):
        newwin_ref[w0:w0 + 2 * kd, c * LANES:(c + 1) * LANES] = jnp.where(lane_w < LANES - s_new, shifted[c], shifted[c + 1])


def _nsa_sample_call(pt_flat, qa, small, cmpkv, pool_t, slcn, winbuf_t, winn, msel, eblk, gexp, eye,
                     n_batch, s_new, pages_per_seq):
    past = pages_per_seq * PAGE_SIZE
    n_sel = -(-(past + s_new) // SEL_BLOCK)
    n_cmp_rows = cmpkv.shape[0] // n_batch
    kv_rows = 2 * G_NSA * HEAD_DIM
    wb = winbuf_t.shape[1]
    n_seq = 2 if n_batch % 2 == 0 else 1
    row = lambda b, pt: (b, 0)
    fixed = lambda b, pt: (0, 0)

    def page_spec(k):
        i, j = divmod(k, pages_per_seq)
        return pl.BlockSpec((kv_rows, PAGE_SIZE), lambda b, pt: (pt[(b * n_seq + i) * pages_per_seq + j], 0))

    grid_spec = pltpu.PrefetchScalarGridSpec(
        num_scalar_prefetch=1,
        grid=(n_batch // n_seq,),
        in_specs=[pl.BlockSpec((n_seq * s_new, qa.shape[1]), row), pl.BlockSpec((n_seq * s_new, LANES), row),
                  pl.BlockSpec((n_seq * n_cmp_rows, 2 * LANES), row)]
                 + [page_spec(k) for k in range(n_seq * pages_per_seq)]
                 + [pl.BlockSpec((n_seq * s_new, 2 * LANES), row), pl.BlockSpec((n_seq * kv_rows, wb), row),
                    pl.BlockSpec((n_seq * s_new, 2 * LANES), row),
                    pl.BlockSpec(msel.shape, fixed), pl.BlockSpec(eblk.shape, fixed), pl.BlockSpec(gexp.shape, fixed),
                    pl.BlockSpec(eye.shape, fixed)],
        out_specs=[pl.BlockSpec((n_seq * s_new, qa.shape[1]), row), pl.BlockSpec((n_seq * kv_rows, wb), row)],
    )
    return pl.pallas_call(
        functools.partial(_nsa_sample_kernel, n_pages=pages_per_seq, n_seq=n_seq, past=past, n_sel=n_sel),
        grid_spec=grid_spec,
        out_shape=[jax.ShapeDtypeStruct(qa.shape, F32), jax.ShapeDtypeStruct(winbuf_t.shape, F32)],
        compiler_params=_params(("arbitrary",)),
        name="nsa_sample",
    )(pt_flat, qa, small, cmpkv, *([pool_t] * (n_seq * pages_per_seq)), slcn, winbuf_t, winn, msel, eblk, gexp, eye)


def _fox_sample_kernel(pt_ref, q_ref, small_ref, foxn_ref, *refs, n_pages, n_seq, past):
    del pt_ref
    total = n_pages * n_seq
    psel_ref, triu_ref, eye_ref, o_ref = refs[2 * total:]
    s_new = q_ref.shape[0] // n_seq
    n_keys = past + PAGE_SIZE
    key_ok = (lax.broadcasted_iota(jnp.int32, (s_new, n_keys), 1)
              <= past + lax.broadcasted_iota(jnp.int32, (s_new, n_keys), 0))
    for seq_i in range(n_seq):
        _fox_sample_one(seq_i, refs[seq_i * n_pages:(seq_i + 1) * n_pages],
                        refs[total + seq_i * n_pages:total + (seq_i + 1) * n_pages],
                        q_ref, small_ref, foxn_ref, psel_ref, triu_ref, eye_ref, o_ref, s_new, past, key_ok)


def _fox_sample_one(seq_i, page_refs, logf_refs, q_ref, small_ref, foxn_ref, psel_ref, triu_ref, eye_ref, o_ref,
                    s_new, past, key_ok):
    n_pages = len(page_refs)
    tok = slice(seq_i * s_new, (seq_i + 1) * s_new)
    small_pad = _pad_rows(small_ref[tok, :], LANES)
    a, b, c = _split3(small_pad)
    psel = psel_ref[...]
    new_t = _dot_nt(psel, a) + _dot_nt(psel, b) + _dot_nt(psel, c)
    w0 = jnp.concatenate([r[...] for r in logf_refs] + [new_t], axis=0)
    wc = _dot_exact_lhs(w0, triu_ref[...])
    run = jnp.zeros((N_HEADS, 1), F32)
    c_rows = []
    for j in range(n_pages + 1):
        blk = wc[j * N_HEADS:(j + 1) * N_HEADS]
        c_rows.append(blk + run)
        run = run + blk[:, LANES - 1:LANES]
    a, b, c = _split3(_pad_rows(c_rows[n_pages], LANES))
    eye = eye_ref[...]
    cq_t = _dot_nt(eye, a) + _dot_nt(eye, b) + _dot_nt(eye, c)

    q = q_ref[tok, :]
    foxn = foxn_ref[tok, :]
    d_k = N_HEADS * HEAD_DIM
    n_hg = 4
    w_hg = n_hg * HEAD_DIM
    lane4 = lax.broadcasted_iota(jnp.int32, (s_new, w_hg), 1)
    own = [(lane4 >= i * HEAD_DIM) & (lane4 < (i + 1) * HEAD_DIM) for i in range(n_hg)]
    for hg in range(N_HEADS // n_hg):
        cols = slice(hg * w_hg, (hg + 1) * w_hg)
        q4 = q[:, cols]
        qbd = jnp.concatenate([jnp.where(own[i], q4, 0.0) for i in range(n_hg)], axis=0).astype(BF16)
        kt = jnp.concatenate([r[hg * w_hg:(hg + 1) * w_hg, :] for r in page_refs], axis=1).astype(BF16)
        vt = jnp.concatenate([r[d_k + hg * w_hg:d_k + (hg + 1) * w_hg, :] for r in page_refs], axis=1).astype(BF16)
        k_new = _pad_rows(foxn[:, cols], PAGE_SIZE).astype(BF16)
        v_new = _pad_rows(foxn[:, d_k + hg * w_hg:d_k + (hg + 1) * w_hg], PAGE_SIZE).astype(BF16)
        s = jnp.concatenate([_dot(qbd, kt), _dot_nt(qbd, k_new)], axis=1)
        rows = []
        for i in range(n_hg):
            head = hg * n_hg + i
            ck_row = jnp.concatenate([cr[head:head + 1, :] for cr in c_rows], axis=1)
            sh = s[i * s_new:(i + 1) * s_new] + (cq_t[0:s_new, head:head + 1] - ck_row)
            rows.append(jnp.where(key_ok, sh, NEG))
        s = jnp.concatenate(rows, axis=0)
        mx = jnp.max(s, axis=-1, keepdims=True)
        p = jnp.exp(s - mx)
        den = jnp.sum(p, axis=-1, keepdims=True)
        pb = p.astype(BF16)
        o = (_dot_nt(pb[:, :past], vt) + _dot(pb[:, past:], v_new)) / den
        out = jnp.where(own[0], o[0:s_new], 0.0)
        for i in range(1, n_hg):
            out = out + jnp.where(own[i], o[i * s_new:(i + 1) * s_new], 0.0)
        o_ref[tok, cols] = out


def _fox_sample_call(pt_flat, qb, small, foxn, pool_t, logf_t, psel, triu, eye, n_batch, s_new, pages_per_seq):
    past = pages_per_seq * PAGE_SIZE
    kv_rows = 2 * N_HEADS * HEAD_DIM
    n_seq = 2 if n_batch % 2 == 0 else 1
    row = lambda b, pt: (b, 0)
    fixed = lambda b, pt: (0, 0)

    def page_of(k):
        i, j = divmod(k, pages_per_seq)
        return lambda b, pt: pt[(b * n_seq + i) * pages_per_seq + j]

    def page_spec(k):
        page = page_of(k)
        return pl.BlockSpec((kv_rows, PAGE_SIZE), lambda b, pt: (page(b, pt), 0))

    def logf_spec(k):
        page = page_of(k)
        return pl.BlockSpec((None,) + logf_t.shape[1:], lambda b, pt: (page(b, pt), 0, 0))

    total = n_seq * pages_per_seq
    grid_spec = pltpu.PrefetchScalarGridSpec(
        num_scalar_prefetch=1,
        grid=(n_batch // n_seq,),
        in_specs=[pl.BlockSpec((n_seq * s_new, qb.shape[1]), row), pl.BlockSpec((n_seq * s_new, LANES), row),
                  pl.BlockSpec((n_seq * s_new, foxn.shape[1]), row)]
                 + [page_spec(k) for k in range(total)]
                 + [logf_spec(k) for k in range(total)]
                 + [pl.BlockSpec(psel.shape, fixed), pl.BlockSpec(triu.shape, fixed), pl.BlockSpec(eye.shape, fixed)],
        out_specs=pl.BlockSpec((n_seq * s_new, qb.shape[1]), row),
    )
    return pl.pallas_call(
        functools.partial(_fox_sample_kernel, n_pages=pages_per_seq, n_seq=n_seq, past=past),
        grid_spec=grid_spec,
        out_shape=jax.ShapeDtypeStruct(qb.shape, F32),
        compiler_params=_params(("arbitrary",)),
        name="fox_sample",
    )(pt_flat, qb, small, foxn, *([pool_t] * total), *([logf_t] * total), psel, triu, eye)


def _merge_kernel(x_ref, oa_ref, ob_ref, z_ref, w_ref, y_ref):
    d_a = oa_ref.shape[1]
    z = z_ref[...]
    gate = z * _sigmoid(z)
    u = jnp.concatenate([oa_ref[...] * gate[:, 0:d_a], ob_ref[...] * gate[:, d_a:]], axis=1).astype(BF16)
    y_ref[...] = x_ref[...] + _dot(u, w_ref[...])


def _merge_call(x2d, oa, ob, z, w_out_bf, tm, name):
    n, d = x2d.shape
    row = lambda i: (i, 0)
    return pl.pallas_call(
        _merge_kernel,
        grid=(n // tm,),
        in_specs=[pl.BlockSpec((tm, d), row), pl.BlockSpec((tm, oa.shape[1]), row), pl.BlockSpec((tm, ob.shape[1]), row),
                  pl.BlockSpec((tm, z.shape[1]), row), pl.BlockSpec(w_out_bf.shape, lambda i: (0, 0))],
        out_specs=pl.BlockSpec((tm, d), row),
        out_shape=jax.ShapeDtypeStruct((n, d), F32),
        compiler_params=_params(("arbitrary",)),
        name=name,
    )(x2d, oa, ob, z, w_out_bf)


def _row_tile(n, preferred):
    return preferred if n % preferred == 0 else n


def _rope_tables(pos):
    half = HEAD_DIM // 2
    inv = jnp.power(jnp.float32(ROPE_THETA), -jnp.arange(half, dtype=F32) / half)
    ang = pos.astype(F32)[:, None] * inv[None, :]
    cos = jnp.cos(ang)
    sin = jnp.sin(ang)
    return jnp.tile(cos, (1, 4)), jnp.tile(jnp.concatenate([-sin, sin], axis=1), (1, 2))


def _cmp_to_sel(n_cmp, n_sel, rows, cols):
    cs = np.arange(n_cmp) * CMP_STRIDE
    ss = np.arange(n_sel) * SEL_BLOCK
    ov = np.clip(np.minimum(cs[:, None] + CMP_LEN, ss[None, :] + SEL_BLOCK) - np.maximum(cs[:, None], ss[None, :]), 0, None)
    m = np.zeros((rows, cols), np.float32)
    m[:n_cmp, :n_sel] = ov / CMP_LEN
    return jnp.asarray(m, dtype=BF16)


def _block_expand(n_keys):
    e = np.zeros((LANES, n_keys), np.float32)
    k = np.arange(n_keys)
    e[k // SEL_BLOCK, k] = 1.0
    return jnp.asarray(e, dtype=BF16)


def _gate_expand():
    e = np.zeros((LANES, 3 * N_HEADS * HEAD_DIM), np.float32)
    for i in range(3):
        for h in range(N_HEADS):
            e[i * N_HEADS + h, i * N_HEADS * HEAD_DIM + h * HEAD_DIM:i * N_HEADS * HEAD_DIM + (h + 1) * HEAD_DIM] = 1.0
    return jnp.asarray(e, dtype=BF16)


def _head_expand():
    e = np.zeros((LANES, N_HEADS * HEAD_DIM), np.float32)
    for h in range(N_HEADS):
        e[LOGF_LANE + h, h * HEAD_DIM:(h + 1) * HEAD_DIM] = 1.0
    return jnp.asarray(e, dtype=BF16)


def _compress_weights(w1, w2, pe):
    half = CMP_LEN // 2
    eye = jnp.eye(G_NSA, dtype=F32)
    w = w1.reshape(2, half, HEAD_DIM, CMP_HID)
    w1x = jnp.einsum('pldh,gk->lgdpkh', w, eye).reshape(half * G_NSA * HEAD_DIM, 2 * G_NSA * CMP_HID).astype(BF16)
    w2x = jnp.einsum('hd,gk->ghkd', w2, eye).reshape(G_NSA * CMP_HID, G_NSA * HEAD_DIM).astype(BF16)
    pe2 = jnp.broadcast_to(pe.reshape(2, half, 1, HEAD_DIM), (2, half, G_NSA, HEAD_DIM)).reshape(2, -1)
    pex = jnp.concatenate([pe2, jnp.zeros((14, pe2.shape[1]), F32)], axis=0).astype(BF16)
    return w1x, w2x, pex


def _stored_tiles(cache):
    return jnp.transpose(cache, (0, 2, 3, 4, 1)).reshape(-1, cache.shape[1])


def kernel(x_prompt, x_sample, cache_nsa_cmp_kv, cache_nsa_slc_kv, cache_nsa_win_kv, cache_fox_kv, cache_fox_logf,
           page_table, g_norm, w_in, b_f, gq_a, gk_cmp, gk_slc, gk_win, pe_cmp_k, pe_cmp_v,
           w_cmp1_k, w_cmp2_k, w_cmp1_v, w_cmp2_v, gq_b, gk_b, w_out):
    depth = g_norm.shape[0]
    assert depth == 1, "single-layer stack"
    n_b, seq, d_model = x_prompt.shape
    n_db, s_new, _ = x_sample.shape
    pages_per_seq = page_table.shape[1]
    past = pages_per_seq * PAGE_SIZE
    wb = cache_nsa_win_kv.shape[2]
    n_pool = cache_nsa_cmp_kv.shape[1]
    d_a = N_HEADS * HEAD_DIM
    kv_w = 2 * G_NSA * HEAD_DIM

    w = w_in[0]
    o_kc, o_ga, o_za, o_qb, o_kb, o_fb, o_zb = 512, 1280, 1304, 1816, 2328, 3352, 3360
    gate_cols = o_ga + (np.arange(N_HEADS)[None, :] * 3 + np.arange(3)[:, None]).reshape(-1)
    small_w = jnp.concatenate([w[:, gate_cols], w[:, o_fb:o_fb + N_HEADS],
                               jnp.zeros((d_model, LANES - GATE_LANES - N_HEADS), F32)], axis=1)
    w_bf = jnp.concatenate([w[:, 0:o_ga], w[:, o_za:o_qb], w[:, o_zb:o_zb + d_a], w[:, o_qb:o_fb], small_w],
                           axis=1).astype(BF16)
    assert w_bf.shape[1] == C_END
    gains = jnp.concatenate([jnp.tile(v[0], 2)[None, :] for v in (gq_a, gk_slc, gk_win, gq_b, gk_b)]
                            + [jnp.zeros((3, LANES), F32)], axis=0)
    bf_row = jnp.zeros((1, LANES), F32).at[0, LOGF_LANE:LOGF_LANE + N_HEADS].set(b_f[0])
    lane_head = np.arange(2 * LANES) // HEAD_DIM
    ones_bd2 = jnp.asarray(lane_head[:, None] == lane_head[None, :], dtype=BF16)
    ones_bd = ones_bd2[0:LANES, 0:LANES]
    hexp = _head_expand()
    gexp = _gate_expand()
    consts = lambda tm: (g_norm, w_bf, gains, bf_row, ones_bd2,
                         jnp.asarray(np.tril(np.ones((tm, tm), np.float32)), dtype=BF16), hexp)

    tm_p = 256
    cos_p, sin_p = _rope_tables(jnp.arange(seq, dtype=jnp.int32))
    tiles_per_seq = seq // tm_p
    (cmp_p, slc_p, win_p, fox_p, small_p, z_p, qa_p, qb_p, cexp_p, csum_p, cmpd_p, slcb_p, winb_p, foxb_p) = _proj_call(
        x_prompt.reshape(n_b * seq, d_model), (cos_p, sin_p, lambda i: (i % tiles_per_seq, 0)),
        consts(tm_p), tiles_per_seq, tm_p, "proj_prompt", True, LOG2_E)
    tm_s = _row_tile(n_db * s_new, 256)
    cos_s, sin_s = _rope_tables(past + jnp.arange(s_new, dtype=jnp.int32))
    cos_s, sin_s = jnp.tile(cos_s, (tm_s // s_new, 1)), jnp.tile(sin_s, (tm_s // s_new, 1))
    (cmp_s, slc_s, win_s, fox_s, small_s, z_s, qa_s, qb_s, _, _, _, _, _, _) = _proj_call(
        x_sample.reshape(n_db * s_new, d_model), (cos_s, sin_s, lambda i: (0, 0)),
        consts(tm_s), 1, tm_s, "proj_sample", False, 1.0)

    w1k, w2k, pek = _compress_weights(w_cmp1_k[0], w_cmp2_k[0], pe_cmp_k[0])
    w1v, w2v, pev = _compress_weights(w_cmp1_v[0], w_cmp2_v[0], pe_cmp_v[0])
    cw = (w1k, w1v, pek, pev, w2k, w2v, jnp.tile(gk_cmp[0], 2)[None, :], ones_bd)
    chunk_w = CMP_STRIDE * kv_w
    ckv_p = _compress_prompt_call(cmpd_p.reshape(n_b * seq // CMP_STRIDE, chunk_w), cw, seq // CMP_STRIDE)
    pt_flat = page_table.reshape(-1).astype(jnp.int32)
    tok = np.arange(PAGE_SIZE)
    perm = np.zeros((PAGE_SIZE, PAGE_SIZE), np.float32)
    perm[(tok % CMP_STRIDE) * (PAGE_SIZE // CMP_STRIDE) + tok // CMP_STRIDE, tok] = 1.0
    ckv_s = _compress_sample_call(pt_flat, _stored_tiles(cache_nsa_cmp_kv[0]), jnp.asarray(perm, dtype=BF16), cw,
                                  n_db, pages_per_seq, 2)

    n_cmp_p = (seq - CMP_LEN) // CMP_STRIDE + 1
    msel_p = _cmp_to_sel(n_cmp_p, seq // SEL_BLOCK, seq // CMP_STRIDE, LANES)
    tq_a = 512
    eblk_p = jnp.transpose(_block_expand(seq).reshape(LANES, seq // tq_a, tq_a), (1, 0, 2))
    oa_p = _nsa_prompt_call(qa_p, small_p, ckv_p, slcb_p, winb_p, msel_p, eblk_p, gexp, n_b, seq, tq_a)
    n_cmp_s = (past + s_new - CMP_LEN) // CMP_STRIDE + 1
    n_sel_s = -(-(past + s_new) // SEL_BLOCK)
    msel_s = _cmp_to_sel(n_cmp_s, n_sel_s, past // CMP_STRIDE, LANES)
    eye_kv = jnp.asarray(np.eye(kv_w, dtype=np.float32), dtype=BF16)
    oa_s, new_win_t = _nsa_sample_call(
        pt_flat, qa_s, small_s, ckv_s, _stored_tiles(cache_nsa_slc_kv[0]), slc_s,
        _stored_tiles(cache_nsa_win_kv[0]), win_s, msel_s, _block_expand(past + PAGE_SIZE), gexp, eye_kv,
        n_db, s_new, pages_per_seq)
    new_win = jnp.transpose(new_win_t.reshape(n_db, 2, G_NSA, HEAD_DIM, wb), (0, 4, 1, 2, 3))[None]

    tq_f = 512
    c_p = csum_p[:, LOGF_LANE:LOGF_LANE + N_HEADS].reshape(n_b, seq, N_HEADS)
    ckt = jnp.transpose(c_p, (0, 2, 1)).reshape(n_b, N_HEADS, seq // tq_f, tq_f)
    ob_p = _fox_prompt_call(qb_p, foxb_p, cexp_p, ckt, n_b, seq, tq_f)
    psel = jnp.asarray(np.arange(LANES)[None, :] == (LOGF_LANE + np.arange(2 * N_HEADS))[:, None], dtype=BF16)
    psel = psel.at[N_HEADS:].set(0)
    triu = jnp.asarray(np.triu(np.ones((LANES, LANES), np.float32)), dtype=BF16)
    eye = jnp.asarray(np.eye(LANES, dtype=np.float32), dtype=BF16)
    ob_s = _fox_sample_call(
        pt_flat, qb_s, small_s, fox_s, _stored_tiles(cache_fox_kv[0]),
        jnp.transpose(cache_fox_logf[0], (0, 2, 1)), psel, triu, eye, n_db, s_new, pages_per_seq)

    w_out_bf = w_out[0].astype(BF16)
    y_p = _merge_call(x_prompt.reshape(n_b * seq, d_model), oa_p, ob_p, z_p, w_out_bf, 512, "merge_prompt")
    y_s = _merge_call(x_sample.reshape(n_db * s_new, d_model), oa_s, ob_s, z_s, w_out_bf,
                      _row_tile(n_db * s_new, 512), "merge_sample")

    kv5 = lambda a, nb, t, heads: a.reshape(1, nb, t, 2, heads, HEAD_DIM)
    kv5_t = lambda a, heads: jnp.transpose(a.reshape(n_b, 2, heads, HEAD_DIM, -1), (0, 4, 1, 2, 3))[None]
    wbp = min(WINDOW, seq)
    return (y_p.reshape(n_b, seq, d_model), y_s.reshape(n_db, s_new, d_model),
            kv5_t(cmp_p, G_NSA), kv5(cmp_s, n_db, s_new, G_NSA),
            kv5_t(slc_p, G_NSA), kv5(slc_s, n_db, s_new, G_NSA),
            kv5_t(win_p.reshape(n_b, kv_w, seq)[:, :, seq - wbp:], G_NSA), new_win,
            kv5_t(fox_p, N_HEADS), kv5(fox_s, n_db, s_new, N_HEADS),
            small_p[:, LOGF_LANE:LOGF_LANE + N_HEADS].reshape(1, n_b, seq, N_HEADS),
            small_s[:, LOGF_LANE:LOGF_LANE + N_HEADS].reshape(1, n_db, s_new, N_HEADS))
```

```python
import functools

import numpy as np
import jax
import jax.numpy as jnp
from jax import lax
from jax.experimental import pallas as pl
from jax.experimental.pallas import tpu as pltpu

F32 = jnp.float32
BF16 = jnp.bfloat16

HEAD_DIM = 64
LANES = 128
N_HEADS = 8
G_NSA = 2
R_NSA = N_HEADS // G_NSA
CMP_LEN = 32
CMP_STRIDE = 16
CMP_HID = 2 * HEAD_DIM
SEL_BLOCK = 64
SEL_TOP = 16
WINDOW = 512
PAGE_SIZE = 128
ROPE_THETA = 10000.0
EPS = 1e-6
NEG = -1e30
FORCED = 1e4
QK_SCALE = HEAD_DIM ** -0.5
LOG2_E = 1.4426950408889634

VMEM_LIMIT = 56 * 1024 * 1024

C_QA, C_CMP, C_SLC, C_WIN, C_Z, C_QB, C_FOX, C_SMALL, C_END = 0, 512, 768, 1024, 1280, 2304, 2816, 3840, 3968
GATE_LANES = 3 * N_HEADS
LOGF_LANE = GATE_LANES


def _dot(a, b):
    return jnp.dot(a, b, preferred_element_type=F32)


def _dot_nt(a, b):
    return lax.dot_general(a, b, (((1,), (1,)), ((), ())), preferred_element_type=F32)


def _split2(x):
    hi = x.astype(BF16)
    lo = (x - hi.astype(F32)).astype(BF16)
    return hi, lo


def _split3(x):
    hi = x.astype(BF16)
    r = x - hi.astype(F32)
    mid = r.astype(BF16)
    lo = (r - mid.astype(F32)).astype(BF16)
    return hi, mid, lo


def _dot_exact_lhs(x, m):
    a, b, c = _split3(x)
    return _dot(a, m) + _dot(b, m) + _dot(c, m)


def _dot_exact_rhs(m, x):
    a, b, c = _split3(x)
    return _dot(m, a) + _dot(m, b) + _dot(m, c)


def _sigmoid(x):
    return 1.0 / (1.0 + jnp.exp(-x))


def _params(sem):
    return pltpu.CompilerParams(dimension_semantics=sem, vmem_limit_bytes=VMEM_LIMIT)


def _proj_kernel(x_ref, g_ref, w_ref, cos_ref, sin_ref, gains_ref, bf_ref, ones_ref, tri_ref, hexp_ref,
                 cmp_ref, slc_ref, win_ref, fox_ref, small_ref, z_ref, qa_ref, qb_ref, cexp_ref, csum_ref, cmpd_ref,
                 slcb_ref, winb_ref, foxb_ref, carry_ref, *, tm, tiles_per_seq, kv_t, logit_scale):
    i = pl.program_id(0)
    qb_scale = QK_SCALE * logit_scale
    x = x_ref[...]
    ms = jnp.mean(x * x, axis=-1, keepdims=True)
    h = (x * lax.rsqrt(ms + EPS) * g_ref[...]).astype(BF16)
    cos = cos_ref[...]
    sin = sin_ref[...]
    lane = lax.broadcasted_iota(jnp.int32, (tm, LANES), 1)
    first_half = (lane & (HEAD_DIM // 2)) == 0
    ones = ones_ref[...]

    def proj(c0, c1):
        return _dot(h, w_ref[:, c0:c1])

    def sumsq_by_head(y):
        hi, lo = _split2(y * y)
        m = ones if y.shape[1] == 2 * LANES else ones[0:LANES, 0:LANES]
        return _dot(hi, m) + _dot(lo, m)

    def head_norm(y, row, ss=None):
        if ss is None:
            ss = sumsq_by_head(y)
        return y * lax.rsqrt(ss * (1.0 / HEAD_DIM) + EPS) * gains_ref[row:row + 1, :]

    def head_norm_chunks(y, row):
        out = []
        for s in range(2):
            ss2 = sumsq_by_head(y[:, 2 * s * LANES:2 * (s + 1) * LANES])
            for c in (2 * s, 2 * s + 1):
                out.append(head_norm(chunk(y, c), row, ss2[:, (c - 2 * s) * LANES:(c - 2 * s + 1) * LANES]))
        return out

    def rope(y):
        partner = jnp.where(first_half, pltpu.roll(y, LANES - HEAD_DIM // 2, 1), pltpu.roll(y, HEAD_DIM // 2, 1))
        return y * cos + partner * sin

    def chunk(y, c):
        return y[:, c * LANES:(c + 1) * LANES]

    def with_ones(v, e):
        return jnp.where((lane < HEAD_DIM) if e == 0 else (lane >= HEAD_DIM), v, 1.0)

    y = proj(C_QA, C_CMP)
    for c, t in enumerate(head_norm_chunks(y, 0)):
        qa_ref[:, c * LANES:(c + 1) * LANES] = rope(t) * QK_SCALE

    def store_kv(o_ref, t, c):
        if kv_t:
            o_ref[c * LANES:(c + 1) * LANES, :] = t.T
        else:
            o_ref[:, c * LANES:(c + 1) * LANES] = t

    y = proj(C_CMP, C_SLC)
    k = rope(chunk(y, 0))
    v = chunk(y, 1)
    cmpd_ref[:, 0:LANES] = k
    cmpd_ref[:, LANES:2 * LANES] = v
    store_kv(cmp_ref, k, 0)
    store_kv(cmp_ref, v, 1)

    for (c0, c1, row, o_ref, ob_ref) in ((C_SLC, C_WIN, 1, slc_ref, slcb_ref), (C_WIN, C_Z, 2, win_ref, winb_ref)):
        y = proj(c0, c1)
        k = rope(head_norm(chunk(y, 0), row))
        v = chunk(y, 1)
        store_kv(o_ref, k, 0)
        store_kv(o_ref, v, 1)
        ob_ref[:, 0:LANES] = k.astype(BF16)
        for e in range(2):
            ob_ref[:, (1 + e) * LANES:(2 + e) * LANES] = with_ones(v, e).astype(BF16)

    z_ref[...] = proj(C_Z, C_QB)

    y = proj(C_QB, C_FOX)
    for c, t in enumerate(head_norm_chunks(y, 3)):
        qb_ref[:, c * LANES:(c + 1) * LANES] = t * qb_scale

    y = proj(C_FOX, C_SMALL)
    kb = head_norm_chunks(y[:, 0:N_HEADS * HEAD_DIM], 4)
    for c in range(8):
        t = kb[c] if c < 4 else chunk(y, c)
        store_kv(fox_ref, t, c)
        if c < 4:
            foxb_ref[:, c * LANES:(c + 1) * LANES] = t.astype(BF16)
        else:
            for e in range(2):
                foxb_ref[:, (c + 4 * e) * LANES:(c + 4 * e + 1) * LANES] = with_ones(t, e).astype(BF16)

    raw = proj(C_SMALL, C_END)
    zf = raw + bf_ref[...]
    logf = jnp.minimum(zf, 0.0) - jnp.log1p(jnp.exp(-jnp.abs(zf)))
    small = jnp.where(lane < GATE_LANES, _sigmoid(raw), jnp.where(lane < GATE_LANES + N_HEADS, logf, 0.0))
    small_ref[...] = small

    @pl.when(i % tiles_per_seq == 0)
    def _():
        carry_ref[...] = jnp.zeros_like(carry_ref)

    c = carry_ref[0:1, :] + _dot_exact_rhs(tri_ref[...], small)
    carry_ref[0:1, :] = c[tm - 1:tm, :]
    c = c * logit_scale
    csum_ref[...] = c
    cexp_ref[...] = _dot_exact_lhs(c, hexp_ref[...])


def _proj_call(x2d, tables, consts, tiles_per_seq, tm, name, kv_t, logit_scale):
    n = x2d.shape[0]
    cos_t, sin_t, table_map = tables
    g_norm, w_bf, gains, bf_row, ones_bd, tri, hexp = consts
    row = lambda i: (i, 0)
    fixed = lambda i: (0, 0)
    full = lambda a: pl.BlockSpec(a.shape, fixed)
    kv_widths = (256, 256, 256, 1024)
    out_widths = (128, 1024, 512, 512, 512, 128, 256)
    bf_widths = (3 * LANES, 3 * LANES, 3 * N_HEADS * HEAD_DIM)
    if kv_t:
        seq = tiles_per_seq * tm
        out_shape = [jax.ShapeDtypeStruct((n // seq * w, seq), F32) for w in kv_widths]
        out_specs = [pl.BlockSpec((w, tm), lambda i: (i // tiles_per_seq, i % tiles_per_seq)) for w in kv_widths]
    else:
        out_shape = [jax.ShapeDtypeStruct((n, w), F32) for w in kv_widths]
        out_specs = [pl.BlockSpec((tm, w), row) for w in kv_widths]
    out_shape += [jax.ShapeDtypeStruct((n, w), F32) for w in out_widths]
    out_shape += [jax.ShapeDtypeStruct((n, w), BF16) for w in bf_widths]
    out_specs += [pl.BlockSpec((tm, w), row) for w in out_widths + bf_widths]
    return pl.pallas_call(
        functools.partial(_proj_kernel, tm=tm, tiles_per_seq=tiles_per_seq, kv_t=kv_t, logit_scale=logit_scale),
        grid=(n // tm,),
        in_specs=[pl.BlockSpec((tm, x2d.shape[1]), row), full(g_norm), full(w_bf),
                  pl.BlockSpec((tm, LANES), table_map), pl.BlockSpec((tm, LANES), table_map),
                  full(gains), full(bf_row), full(ones_bd), full(tri), full(hexp)],
        out_specs=out_specs,
        out_shape=out_shape,
        scratch_shapes=[pltpu.VMEM((8, LANES), F32)],
        compiler_params=_params(("arbitrary",)),
        name=name,
    )(x2d, g_norm, w_bf, cos_t, sin_t, gains, bf_row, ones_bd, tri, hexp)


def _compress_body(chunk_rows, m, wk_ref, wv_ref, pek_ref, pev_ref, w2k_ref, w2v_ref, gain_ref, ones_ref, out_ref,
                   rows_per_seq):
    hid = G_NSA * CMP_HID

    def branch(kind, w1_ref, pe_ref, w2_ref):
        r = _dot(jnp.concatenate([chunk_rows(kind), pe_ref[...]], axis=0), w1_ref[...])
        hh = r[:m, :hid] + pltpu.roll(r[:m, hid:], m - 1, 0) + r[m:m + 1, :hid] + r[m + 1:m + 2, hid:]
        act = hh * _sigmoid(hh)
        return _dot(act.astype(BF16), w2_ref[...])

    kc = branch(0, wk_ref, pek_ref, w2k_ref)
    hi, lo = _split2(kc * kc)
    ss = _dot(hi, ones_ref[...]) + _dot(lo, ones_ref[...])
    kc = kc * lax.rsqrt(ss * (1.0 / HEAD_DIM) + EPS) * gain_ref[...]
    vc = branch(1, wv_ref, pev_ref, w2v_ref)
    rows = lax.broadcasted_iota(jnp.int32, (m, LANES), 0)
    valid = (rows % rows_per_seq) != rows_per_seq - 1
    out_ref[:, 0:LANES] = jnp.where(valid, kc, 0.0)
    out_ref[:, LANES:2 * LANES] = jnp.where(valid, vc, 0.0)


def _compress_prompt_kernel(y_ref, *rest, rows_per_seq):
    y = y_ref[...].astype(BF16)
    width = 2 * G_NSA * HEAD_DIM

    def chunk_rows(kind):
        return jnp.concatenate([y[:, l * width + kind * LANES:l * width + (kind + 1) * LANES]
                                for l in range(CMP_STRIDE)], axis=1)

    _compress_body(chunk_rows, y.shape[0], *rest, rows_per_seq=rows_per_seq)


def _compress_sample_kernel(pt_ref, *refs, n_pages, rows_per_seq):
    del pt_ref
    page_refs = refs[:n_pages]
    perm_ref = refs[n_pages]
    chunks = PAGE_SIZE // CMP_STRIDE

    def chunk_rows(kind):
        tiles = [_dot_nt(perm_ref[...], r[kind * LANES:(kind + 1) * LANES, :].astype(BF16)) for r in page_refs]
        return jnp.concatenate([jnp.concatenate([t[l * chunks:(l + 1) * chunks] for t in tiles], axis=0)
                                for l in range(CMP_STRIDE)], axis=1).astype(BF16)

    _compress_body(chunk_rows, n_pages * chunks, *refs[n_pages + 1:], rows_per_seq=rows_per_seq)


def _compress_prompt_call(y2d, cw, rows_per_seq):
    n = y2d.shape[0]
    fixed = lambda i: (0, 0)
    return pl.pallas_call(
        functools.partial(_compress_prompt_kernel, rows_per_seq=rows_per_seq),
        grid=(n // rows_per_seq,),
        in_specs=[pl.BlockSpec((rows_per_seq, y2d.shape[1]), lambda i: (i, 0))] + [pl.BlockSpec(a.shape, fixed) for a in cw],
        out_specs=pl.BlockSpec((rows_per_seq, 2 * LANES), lambda i: (i, 0)),
        out_shape=jax.ShapeDtypeStruct((n, 2 * LANES), F32),
        compiler_params=_params(("arbitrary",)),
        name="compress_prompt",
    )(y2d, *cw)


def _compress_sample_call(pt_flat, pool_t, perm, cw, n_batch, pages_per_seq, batch_per_step):
    kv_rows = 2 * G_NSA * HEAD_DIM
    rows_per_seq = pages_per_seq * (PAGE_SIZE // CMP_STRIDE)
    n_pages = batch_per_step * pages_per_seq
    fixed = lambda i, pt: (0, 0)

    def page_spec(k):
        bb, j = divmod(k, pages_per_seq)
        return pl.BlockSpec((kv_rows, PAGE_SIZE), lambda i, pt: (pt[(i * batch_per_step + bb) * pages_per_seq + j], 0))

    m = batch_per_step * rows_per_seq
    grid_spec = pltpu.PrefetchScalarGridSpec(
        num_scalar_prefetch=1,
        grid=(n_batch // batch_per_step,),
        in_specs=[page_spec(k) for k in range(n_pages)] + [pl.BlockSpec(a.shape, fixed) for a in (perm,) + tuple(cw)],
        out_specs=pl.BlockSpec((m, 2 * LANES), lambda i, pt: (i, 0)),
    )
    return pl.pallas_call(
        functools.partial(_compress_sample_kernel, n_pages=n_pages, rows_per_seq=rows_per_seq),
        grid_spec=grid_spec,
        out_shape=jax.ShapeDtypeStruct((n_batch * rows_per_seq, 2 * LANES), F32),
        compiler_params=_params(("arbitrary",)),
        name="compress_sample",
    )(pt_flat, *([pool_t] * n_pages), perm, *cw)


def _group_queries(qa, g, lane):
    in_group = (lane < HEAD_DIM) if g == 0 else (lane >= HEAD_DIM)
    rows = []
    for r in range(R_NSA):
        head = R_NSA * g + r
        y = qa[:, (head // 2) * LANES:(head // 2 + 1) * LANES]
        if head % 2 != g:
            y = pltpu.roll(y, HEAD_DIM, 1)
        rows.append(jnp.where(in_group, y, 0.0))
    return jnp.concatenate(rows, axis=0)


def _assemble_heads(res, tq, lane):
    chunks = []
    for c in range(N_HEADS // 2):
        g = c // 2
        a = res[g][(2 * (c % 2)) * tq:(2 * (c % 2) + 1) * tq]
        b = res[g][(2 * (c % 2) + 1) * tq:(2 * (c % 2) + 2) * tq]
        if g == 0:
            b = pltpu.roll(b, HEAD_DIM, 1)
        else:
            a = pltpu.roll(a, HEAD_DIM, 1)
        chunks.append(jnp.where(lane < HEAD_DIM, a, b))
    return jnp.concatenate(chunks, axis=1)


def _flash_step(s, n_rb, rb, fix, values, m_ref, acc_ref, base2=False):
    exp = jnp.exp2 if base2 else jnp.exp
    n_c = s.shape[1] // LANES
    cols = [jnp.concatenate([fix(r, c, s[r * rb:(r + 1) * rb, c * LANES:(c + 1) * LANES]) for r in range(n_rb)], axis=0)
            for c in range(n_c)]
    mx = cols[0]
    for c in range(1, n_c):
        mx = jnp.maximum(mx, cols[c])
    m_prev = m_ref[...]
    m_new = jnp.maximum(m_prev, jnp.max(mx, axis=-1, keepdims=True))
    alpha = exp(m_prev - m_new)
    p = jnp.concatenate([exp(x - m_new) for x in cols], axis=1).astype(BF16)
    if len(values) == 1:
        pv = _dot(p, values[0])
    else:
        rows_per_v = n_rb * rb // len(values)
        pv = jnp.concatenate([_dot(p[i * rows_per_v:(i + 1) * rows_per_v], v) for i, v in enumerate(values)], axis=0)
    acc_ref[...] = alpha * acc_ref[...] + pv
    m_ref[...] = m_new


def _flash_init(m_ref, acc_ref):
    m_ref[...] = jnp.full(m_ref.shape, NEG, F32)
    acc_ref[...] = jnp.zeros(acc_ref.shape, F32)


def _flash_result(acc_ref):
    acc = acc_ref[...]
    return acc / pltpu.roll(acc, HEAD_DIM, 1)


def _block_ranks(impm, n_blocks, lane):
    cnt = jnp.zeros(impm.shape, F32)
    for j in range(n_blocks):
        col = impm[:, j:j + 1]
        ge = jnp.where(col >= impm, 1.0, 0.0)
        gt = jnp.where(col > impm, 1.0, 0.0)
        cnt = cnt + jnp.where(lane > j, ge, gt)
    return cnt


def _compressed_branch(qz, ck, cv, cmask, msel, n_heads, tq):
    q_hi, q_lo = _split2(qz)
    k_hi, k_lo = _split2(ck)
    s = _dot_nt(q_hi, k_hi) + _dot_nt(q_hi, k_lo) + _dot_nt(q_lo, k_hi)
    ps = []
    psum = None
    for r in range(n_heads):
        sr = jnp.where(cmask, s[r * tq:(r + 1) * tq], NEG)
        mx = jnp.max(sr, axis=-1, keepdims=True)
        e = jnp.where(cmask, jnp.exp(sr - mx), 0.0)
        den = jnp.sum(e, axis=-1, keepdims=True)
        p = e / jnp.where(den > 0.0, den, 1.0)
        ps.append(p)
        psum = p if psum is None else psum + p
    o = _dot(jnp.concatenate(ps, axis=0).astype(BF16), cv)
    return o, _dot_exact_lhs(psum, msel)


def _select_blocks(imp, tpos, lane, n_blocks):
    forced = (lane == 0) | (lane == tpos // SEL_BLOCK)
    causal = lane * SEL_BLOCK <= tpos
    impm = jnp.where(forced, FORCED, jnp.where(causal, imp, -1.0))
    impm = jnp.where(lane < n_blocks, impm, -3e38)
    cnt = _block_ranks(impm, n_blocks, lane)
    n_top = min(SEL_TOP, n_blocks)
    return jnp.where(cnt < n_top, jnp.where(lane < n_blocks, 1.0, 0.0), 0.0)


def _select_blocks_t(imp, s0, n_blocks, cnt_ref):
    tq = imp.shape[0]
    grp = 8
    imp_t = jnp.transpose(imp)[0:n_blocks]
    blk = lax.broadcasted_iota(jnp.int32, (n_blocks, tq), 0)
    tpos = s0 + lax.broadcasted_iota(jnp.int32, (n_blocks, tq), 1)
    forced = (blk == 0) | (blk == tpos // SEL_BLOCK)
    impm = jnp.where(forced, FORCED, jnp.where(blk * SEL_BLOCK <= tpos, imp_t, -1.0))
    cnt_ref[...] = jnp.zeros(cnt_ref.shape, F32)
    sub = lax.broadcasted_iota(jnp.int32, (grp, tq), 0)
    for jb in range(n_blocks // grp):
        @pl.when(jb * grp * SEL_BLOCK <= s0 + tq - 1)
        def _(jb=jb):
            for rg in range(n_blocks // grp):
                x = impm[rg * grp:(rg + 1) * grp]
                cnt = cnt_ref[rg * grp:(rg + 1) * grp, :]
                for j in range(jb * grp, (jb + 1) * grp):
                    row = impm[j:j + 1, :]
                    if rg > jb:
                        beat = row >= x
                    elif rg < jb:
                        beat = row > x
                    else:
                        cnt = cnt + jnp.where(sub > j - jb * grp, jnp.where(row >= x, 1.0, 0.0),
                                              jnp.where(row > x, 1.0, 0.0))
                        continue
                    cnt = cnt + jnp.where(beat, 1.0, 0.0)
                cnt_ref[rg * grp:(rg + 1) * grp, :] = cnt
    sel_t = jnp.where(cnt_ref[...] < min(SEL_TOP, n_blocks), 1.0, 0.0)
    return jnp.transpose(jnp.concatenate([sel_t, jnp.zeros((LANES - n_blocks, tq), F32)], axis=0))


def _nsa_prompt_kernel(qa_ref, small_ref, cmp_ref, slc_ref, win_ref, msel_ref, eblk_ref, gexp_ref,
                       oa_ref, m_ref, acc_ref, cnt_ref, *, tq, n_sel):
    tk = tq
    qi = pl.program_id(1)
    s0 = qi * tq
    lane = lax.broadcasted_iota(jnp.int32, (tq, LANES), 1)
    trow = lax.broadcasted_iota(jnp.int32, (tq, LANES), 0)
    tpos = s0 + trow
    row_t = lax.broadcasted_iota(jnp.int32, (tq, tk), 0)
    col_t = lax.broadcasted_iota(jnp.int32, (tq, tk), 1)
    diag_mask = col_t <= row_t
    n_back = WINDOW // tk
    assert n_back * tk == WINDOW
    far_mask = (n_back * tk + row_t - col_t) <= WINDOW
    n_cmp = cmp_ref.shape[0]
    cmask = (lax.broadcasted_iota(jnp.int32, (tq, n_cmp), 1) * CMP_STRIDE + CMP_LEN - 1
             <= s0 + lax.broadcasted_iota(jnp.int32, (tq, n_cmp), 0))
    qa = qa_ref[...]
    ck = cmp_ref[:, 0:LANES]
    cv = cmp_ref[:, LANES:2 * LANES].astype(BF16)

    def chunk_of(mask, c):
        return mask[:, c * LANES:(c + 1) * LANES]

    o_cmp, sel = [], []
    qz = [_group_queries(qa, g, lane) for g in range(G_NSA)]
    for g in range(G_NSA):
        oc, imp = _compressed_branch(qz[g], ck, cv, cmask, msel_ref[...], R_NSA, tq)
        o_cmp.append(oc)
        sel.append(_select_blocks_t(imp, s0, n_sel, cnt_ref).astype(BF16))
    q_all = jnp.concatenate(qz, axis=0).astype(BF16)

    def flash_step(kv_ref, kt, biases, mask):
        off = pl.multiple_of(kt * tk, tk)
        k2 = kv_ref[pl.ds(off, tk), 0:LANES]
        values = [kv_ref[pl.ds(off, tk), (1 + g) * LANES:(2 + g) * LANES] for g in range(G_NSA)]

        def fix(r, c, x):
            if biases is not None:
                x = x + chunk_of(biases[r // R_NSA], c)
            if mask is not None:
                x = jnp.where(chunk_of(mask, c), x, NEG)
            return x

        _flash_step(_dot_nt(q_all, k2), G_NSA * R_NSA, tq, fix, values, m_ref, acc_ref)

    def sel_biases(kt):
        return [(_dot(sel[g], eblk_ref[kt]) - 1.0) * (-NEG) for g in range(G_NSA)]

    def per_group(o):
        return [o[g * R_NSA * tq:(g + 1) * R_NSA * tq] for g in range(G_NSA)]

    _flash_init(m_ref, acc_ref)

    def sel_body(kt, carry):
        flash_step(slc_ref, kt, sel_biases(kt), None)
        return carry

    lax.fori_loop(0, qi, sel_body, 0)
    flash_step(slc_ref, qi, sel_biases(qi), diag_mask)
    o_sel = per_group(_flash_result(acc_ref))

    _flash_init(m_ref, acc_ref)

    for back in range(n_back, 0, -1):
        @pl.when(qi >= back)
        def _(back=back):
            flash_step(win_ref, qi - back, None, far_mask if back == n_back else None)

    flash_step(win_ref, qi, None, diag_mask)
    o_win = per_group(_flash_result(acc_ref))

    gates = _dot_exact_lhs(small_ref[...], gexp_ref[...])
    d_a = N_HEADS * HEAD_DIM
    oa_ref[...] = (gates[:, 0:d_a] * _assemble_heads(o_cmp, tq, lane)
                   + gates[:, d_a:2 * d_a] * _assemble_heads(o_sel, tq, lane)
                   + gates[:, 2 * d_a:3 * d_a] * _assemble_heads(o_win, tq, lane))


def _nsa_prompt_call(qa, small, cmpkv, slcb, winb, msel, eblk, gexp, n_batch, seq, tq):
    nq = seq // tq
    n_cmp_rows = cmpkv.shape[0] // n_batch
    n_sel = seq // SEL_BLOCK
    qrow = lambda b, q: (b * nq + q, 0)
    per_b = lambda b, q: (b, 0)
    fixed = lambda b, q: (0, 0)
    return pl.pallas_call(
        functools.partial(_nsa_prompt_kernel, tq=tq, n_sel=n_sel),
        grid=(n_batch, nq),
        in_specs=[pl.BlockSpec((tq, qa.shape[1]), qrow), pl.BlockSpec((tq, LANES), qrow),
                  pl.BlockSpec((n_cmp_rows, 2 * LANES), per_b),
                  pl.BlockSpec((seq, 3 * LANES), per_b), pl.BlockSpec((seq, 3 * LANES), per_b),
                  pl.BlockSpec(msel.shape, fixed), pl.BlockSpec(eblk.shape, lambda b, q: (0, 0, 0)),
                  pl.BlockSpec(gexp.shape, fixed)],
        out_specs=pl.BlockSpec((tq, qa.shape[1]), qrow),
        out_shape=jax.ShapeDtypeStruct(qa.shape, F32),
        scratch_shapes=[pltpu.VMEM((G_NSA * R_NSA * tq, LANES), F32), pltpu.VMEM((G_NSA * R_NSA * tq, LANES), F32),
                        pltpu.VMEM((n_sel, tq), F32)],
        compiler_params=_params(("arbitrary", "arbitrary")),
        name="nsa_prompt",
    )(qa, small, cmpkv, slcb, winb, msel, eblk, gexp)


def _fox_prompt_kernel(q_ref, k_ref, v0_ref, v1_ref, cq_ref, ck_ref, o_ref, m_ref, acc_ref, *, tq):
    tk = tq
    qi = pl.program_id(2)
    lane = lax.broadcasted_iota(jnp.int32, (tq, LANES), 1)
    q = q_ref[...]
    qz = jnp.concatenate([jnp.where(lane < HEAD_DIM, q, 0.0), jnp.where(lane >= HEAD_DIM, q, 0.0)], axis=0).astype(BF16)
    cq = cq_ref[...]
    cq_sw = pltpu.roll(cq, HEAD_DIM, 1)
    cq_rep = [jnp.where(lane < HEAD_DIM, cq, cq_sw), jnp.where(lane >= HEAD_DIM, cq, cq_sw)]
    diag_mask = lax.broadcasted_iota(jnp.int32, (tq, tk), 1) <= lax.broadcasted_iota(jnp.int32, (tq, tk), 0)
    v_refs = (v0_ref, v1_ref)
    _flash_init(m_ref, acc_ref)

    def step(kt, mask):
        off = pl.multiple_of(kt * tk, tk)
        k2 = k_ref[pl.ds(off, tk), :]
        ck = [ck_ref[0, e, pl.ds(kt, 1), :] for e in range(2)]

        def fix(e, c, x):
            x = x + (cq_rep[e] - ck[e][:, c * LANES:(c + 1) * LANES])
            if mask is not None:
                x = jnp.where(mask[:, c * LANES:(c + 1) * LANES], x, NEG)
            return x

        _flash_step(_dot_nt(qz, k2), 2, tq, fix, [r[pl.ds(off, tk), :] for r in v_refs], m_ref, acc_ref, base2=True)

    def body(kt, carry):
        step(kt, None)
        return carry

    lax.fori_loop(0, qi, body, 0)
    step(qi, diag_mask)
    o = _flash_result(acc_ref)
    o_ref[...] = jnp.where(lane < HEAD_DIM, o[0:tq], o[tq:2 * tq])


def _fox_prompt_call(qb, foxb, cexp, ckt, n_batch, seq, tq):
    nq = seq // tq
    n_pairs = N_HEADS // 2
    qmap = lambda b, hp, q: (b * nq + q, hp)
    return pl.pallas_call(
        functools.partial(_fox_prompt_kernel, tq=tq),
        grid=(n_batch, n_pairs, nq),
        in_specs=[pl.BlockSpec((tq, LANES), qmap),
                  pl.BlockSpec((seq, LANES), lambda b, hp, q: (b, hp)),
                  pl.BlockSpec((seq, LANES), lambda b, hp, q: (b, n_pairs + hp)),
                  pl.BlockSpec((seq, LANES), lambda b, hp, q: (b, 2 * n_pairs + hp)),
                  pl.BlockSpec((tq, LANES), qmap),
                  pl.BlockSpec((1, 2, nq, tq), lambda b, hp, q: (b, hp, 0, 0))],
        out_specs=pl.BlockSpec((tq, LANES), qmap),
        out_shape=jax.ShapeDtypeStruct(qb.shape, F32),
        scratch_shapes=[pltpu.VMEM((2 * tq, LANES), F32), pltpu.VMEM((2 * tq, LANES), F32)],
        compiler_params=_params(("arbitrary", "arbitrary", "arbitrary")),
        name="fox_prompt",
    )(qb, foxb, foxb, foxb, cexp, ckt)


def _round_robin(stages):
    results = [None] * len(stages)
    pending = list(range(len(stages)))
    while pending:
        for i in list(pending):
            out = next(stages[i])
            if out is not None:
                results[i] = out
                pending.remove(i)
    return results


def _pad_rows(x, rows):
    return jnp.concatenate([x, jnp.zeros((rows - x.shape[0], x.shape[1]), x.dtype)], axis=0)


def _nsa_sample_kernel(pt_ref, qa_ref, small_ref, cmp_ref, *refs, n_pages, n_seq, past, n_sel):
    del pt_ref
    all_pages = refs[:n_pages * n_seq]
    (slcn_ref, winbuf_ref, winn_ref, msel_ref, eblk_ref, gexp_ref, eye_ref, oa_ref, newwin_ref) = refs[n_pages * n_seq:]
    s_new = qa_ref.shape[0] // n_seq
    wb = winbuf_ref.shape[1]
    kd = G_NSA * HEAD_DIM
    n_keys = past + PAGE_SIZE
    lane = lax.broadcasted_iota(jnp.int32, (s_new, LANES), 1)
    tpos = past + lax.broadcasted_iota(jnp.int32, (s_new, LANES), 0)
    n_cmp = cmp_ref.shape[0] // n_seq
    cmask = (lax.broadcasted_iota(jnp.int32, (s_new, n_cmp), 1) * CMP_STRIDE + CMP_LEN - 1
             <= past + lax.broadcasted_iota(jnp.int32, (s_new, n_cmp), 0))
    kpos = lax.broadcasted_iota(jnp.int32, (s_new, n_keys), 1)
    key_ok = kpos <= past + lax.broadcasted_iota(jnp.int32, (s_new, n_keys), 0)
    w_keys = wb + PAGE_SIZE
    wi = lax.broadcasted_iota(jnp.int32, (s_new, w_keys), 1)
    wt = lax.broadcasted_iota(jnp.int32, (s_new, w_keys), 0)
    dist = wb + wt - wi
    win_ok = (dist >= 0) & (dist <= WINDOW) & (past - wb + wi >= 0) & (wi < wb + s_new)
    lane_w = lax.broadcasted_iota(jnp.int32, (2 * kd, LANES), 1)
    stages = [_nsa_sample_one(seq_i, all_pages[seq_i * n_pages:(seq_i + 1) * n_pages], qa_ref, small_ref, cmp_ref,
                              slcn_ref, winbuf_ref, winn_ref, msel_ref, eblk_ref, gexp_ref, eye_ref,
                              s_new, n_cmp, n_sel, lane, tpos, cmask, key_ok, win_ok, lane_w)
              for seq_i in range(n_seq)]
    for seq_i, (oa, new_win) in enumerate(_round_robin(stages)):
        oa_ref[seq_i * s_new:(seq_i + 1) * s_new, :] = oa
        for c, chunk in enumerate(new_win):
            newwin_ref[seq_i * 2 * kd:(seq_i + 1) * 2 * kd, c * LANES:(c + 1) * LANES] = chunk


def _nsa_sample_one(seq_i, page_refs, qa_ref, small_ref, cmp_ref, slcn_ref, winbuf_ref, winn_ref, msel_ref, eblk_ref,
                    gexp_ref, eye_ref, s_new, n_cmp, n_sel, lane, tpos, cmask, key_ok, win_ok, lane_w):
    kd = G_NSA * HEAD_DIM
    wb = winbuf_ref.shape[1]
    tok = slice(seq_i * s_new, (seq_i + 1) * s_new)
    w0 = seq_i * 2 * kd
    qa = qa_ref[tok, :]
    ck = cmp_ref[seq_i * n_cmp:(seq_i + 1) * n_cmp, 0:LANES]
    cv = cmp_ref[seq_i * n_cmp:(seq_i + 1) * n_cmp, LANES:2 * LANES].astype(BF16)
    slcn = slcn_ref[tok, :]
    winn = winn_ref[tok, :]
    qz = [_group_queries(qa, g, lane) for g in range(G_NSA)]
    q_all = jnp.concatenate(qz, axis=0).astype(BF16)

    def attend(kt, vt, k_new, v_new, masks):
        n_past = kt.shape[1]
        s = jnp.concatenate([_dot(q_all, kt), _dot_nt(q_all, k_new)], axis=1)
        yield None
        s = jnp.concatenate([jnp.where(masks[g], s[(g * R_NSA + r) * s_new:(g * R_NSA + r + 1) * s_new], NEG)
                             for g in range(G_NSA) for r in range(R_NSA)], axis=0)
        mx = jnp.max(s, axis=-1, keepdims=True)
        p = jnp.exp(s - mx)
        den = jnp.sum(p, axis=-1, keepdims=True)
        pb = p.astype(BF16)
        yield None
        return (_dot_nt(pb[:, :n_past], vt) + _dot(pb[:, n_past:], v_new)) / den

    def new_rows(x):
        return _pad_rows(x, PAGE_SIZE).astype(BF16)

    o_cmp, sel_ok = [], []
    for g in range(G_NSA):
        oc, imp = _compressed_branch(qz[g], ck, cv, cmask, msel_ref[...], R_NSA, s_new)
        o_cmp.append(oc)
        yield None
        sel = _select_blocks(imp, tpos, lane, n_sel).astype(BF16)
        sel_ok.append((_dot(sel, eblk_ref[...]) > 0.5) & key_ok)
        yield None

    o_w = yield from attend(winbuf_ref[w0:w0 + kd, :].astype(BF16), winbuf_ref[w0 + kd:w0 + 2 * kd, :].astype(BF16),
                            new_rows(winn[:, 0:LANES]), new_rows(winn[:, LANES:2 * LANES]), [win_ok] * G_NSA)
    yield None
    kt = jnp.concatenate([r[0:kd, :] for r in page_refs], axis=1).astype(BF16)
    vt = jnp.concatenate([r[kd:2 * kd, :] for r in page_refs], axis=1).astype(BF16)
    o_s = yield from attend(kt, vt, new_rows(slcn[:, 0:LANES]), new_rows(slcn[:, LANES:2 * LANES]), sel_ok)
    yield None
    half = R_NSA * s_new
    o_sel = [o_s[0:half], o_s[half:2 * half]]
    o_win = [o_w[0:half], o_w[half:2 * half]]

    gates = _dot_exact_lhs(small_ref[tok, :], gexp_ref[...])
    d_a = N_HEADS * HEAD_DIM
    oa = (gates[:, 0:d_a] * _assemble_heads(o_cmp, s_new, lane)
          + gates[:, d_a:2 * d_a] * _assemble_heads(o_sel, s_new, lane)
          + gates[:, 2 * d_a:3 * d_a] * _assemble_heads(o_win, s_new, lane))

    xn = jnp.concatenate([jnp.zeros((LANES - s_new, 2 * LANES), F32), winn], axis=0)
    a, b, c = _split3(xn)
    eye = eye_ref[...]
    new_t = _dot_nt(eye, a) + _dot_nt(eye, b) + _dot_nt(eye, c)
    n_chunks = wb // LANES
    shifted = [pltpu.roll(winbuf_ref[w0:w0 + 2 * kd, c * LANES:(c + 1) * LANES], LANES - s_new, 1)
               for c in range(n_chunks)]
    shifted.append(new_t)
    new_win = [jnp.where(lane_w < LANES - s_new, shifted[c], shifted[c + 1]) for c in range(n_chunks)]
    yield oa, new_win


def _nsa_sample_call(pt_flat, qa, small, cmpkv, pool_t, slcn, winbuf_t, winn, msel, eblk, gexp, eye,
                     n_batch, s_new, pages_per_seq):
    past = pages_per_seq * PAGE_SIZE
    n_sel = -(-(past + s_new) // SEL_BLOCK)
    n_cmp_rows = cmpkv.shape[0] // n_batch
    kv_rows = 2 * G_NSA * HEAD_DIM
    wb = winbuf_t.shape[1]
    n_seq = next(n for n in (4, 2, 1) if n_batch % n == 0)
    row = lambda b, pt: (b, 0)
    fixed = lambda b, pt: (0, 0)

    def page_spec(k):
        i, j = divmod(k, pages_per_seq)
        return pl.BlockSpec((kv_rows, PAGE_SIZE), lambda b, pt: (pt[(b * n_seq + i) * pages_per_seq + j], 0))

    grid_spec = pltpu.PrefetchScalarGridSpec(
        num_scalar_prefetch=1,
        grid=(n_batch // n_seq,),
        in_specs=[pl.BlockSpec((n_seq * s_new, qa.shape[1]), row), pl.BlockSpec((n_seq * s_new, LANES), row),
                  pl.BlockSpec((n_seq * n_cmp_rows, 2 * LANES), row)]
                 + [page_spec(k) for k in range(n_seq * pages_per_seq)]
                 + [pl.BlockSpec((n_seq * s_new, 2 * LANES), row), pl.BlockSpec((n_seq * kv_rows, wb), row),
                    pl.BlockSpec((n_seq * s_new, 2 * LANES), row),
                    pl.BlockSpec(msel.shape, fixed), pl.BlockSpec(eblk.shape, fixed), pl.BlockSpec(gexp.shape, fixed),
                    pl.BlockSpec(eye.shape, fixed)],
        out_specs=[pl.BlockSpec((n_seq * s_new, qa.shape[1]), row), pl.BlockSpec((n_seq * kv_rows, wb), row)],
    )
    return pl.pallas_call(
        functools.partial(_nsa_sample_kernel, n_pages=pages_per_seq, n_seq=n_seq, past=past, n_sel=n_sel),
        grid_spec=grid_spec,
        out_shape=[jax.ShapeDtypeStruct(qa.shape, F32), jax.ShapeDtypeStruct(winbuf_t.shape, F32)],
        compiler_params=_params(("arbitrary",)),
        name="nsa_sample",
    )(pt_flat, qa, small, cmpkv, *([pool_t] * (n_seq * pages_per_seq)), slcn, winbuf_t, winn, msel, eblk, gexp, eye)


def _fox_sample_kernel(pt_ref, q_ref, small_ref, foxn_ref, *refs, n_pages, n_seq, past):
    del pt_ref
    total = n_pages * n_seq
    psel_ref, triu_ref, eye_ref, o_ref = refs[2 * total:]
    s_new = q_ref.shape[0] // n_seq
    n_keys = past + PAGE_SIZE
    key_ok = (lax.broadcasted_iota(jnp.int32, (s_new, n_keys), 1)
              <= past + lax.broadcasted_iota(jnp.int32, (s_new, n_keys), 0))
    _round_robin([_fox_sample_one(seq_i, refs[seq_i * n_pages:(seq_i + 1) * n_pages],
                                  refs[total + seq_i * n_pages:total + (seq_i + 1) * n_pages],
                                  q_ref, small_ref, foxn_ref, psel_ref, triu_ref, eye_ref, o_ref, s_new, past, key_ok)
                  for seq_i in range(n_seq)])


def _fox_sample_one(seq_i, page_refs, logf_refs, q_ref, small_ref, foxn_ref, psel_ref, triu_ref, eye_ref, o_ref,
                    s_new, past, key_ok):
    n_pages = len(page_refs)
    tok = slice(seq_i * s_new, (seq_i + 1) * s_new)
    small_pad = _pad_rows(small_ref[tok, :], LANES)
    a, b, c = _split3(small_pad)
    psel = psel_ref[...]
    new_t = _dot_nt(psel, a) + _dot_nt(psel, b) + _dot_nt(psel, c)
    w0 = jnp.concatenate([r[...] for r in logf_refs] + [new_t], axis=0)
    wc = _dot_exact_lhs(w0, triu_ref[...])
    yield None
    run = jnp.zeros((N_HEADS, 1), F32)
    c_rows = []
    for j in range(n_pages + 1):
        blk = wc[j * N_HEADS:(j + 1) * N_HEADS]
        c_rows.append(blk + run)
        run = run + blk[:, LANES - 1:LANES]
    a, b, c = _split3(_pad_rows(c_rows[n_pages], LANES))
    eye = eye_ref[...]
    cq_t = _dot_nt(eye, a) + _dot_nt(eye, b) + _dot_nt(eye, c)
    yield None

    q = q_ref[tok, :]
    foxn = foxn_ref[tok, :]
    d_k = N_HEADS * HEAD_DIM
    n_hg = 4
    w_hg = n_hg * HEAD_DIM
    lane4 = lax.broadcasted_iota(jnp.int32, (s_new, w_hg), 1)
    own = [(lane4 >= i * HEAD_DIM) & (lane4 < (i + 1) * HEAD_DIM) for i in range(n_hg)]
    for hg in range(N_HEADS // n_hg):
        cols = slice(hg * w_hg, (hg + 1) * w_hg)
        q4 = q[:, cols]
        qbd = jnp.concatenate([jnp.where(own[i], q4, 0.0) for i in range(n_hg)], axis=0).astype(BF16)
        kt = jnp.concatenate([r[hg * w_hg:(hg + 1) * w_hg, :] for r in page_refs], axis=1).astype(BF16)
        vt = jnp.concatenate([r[d_k + hg * w_hg:d_k + (hg + 1) * w_hg, :] for r in page_refs], axis=1).astype(BF16)
        k_new = _pad_rows(foxn[:, cols], PAGE_SIZE).astype(BF16)
        v_new = _pad_rows(foxn[:, d_k + hg * w_hg:d_k + (hg + 1) * w_hg], PAGE_SIZE).astype(BF16)
        s = jnp.concatenate([_dot(qbd, kt), _dot_nt(qbd, k_new)], axis=1)
        yield None
        rows = []
        for i in range(n_hg):
            head = hg * n_hg + i
            ck_row = jnp.concatenate([cr[head:head + 1, :] for cr in c_rows], axis=1)
            sh = s[i * s_new:(i + 1) * s_new] + (cq_t[0:s_new, head:head + 1] - ck_row)
            rows.append(jnp.where(key_ok, sh, NEG))
        s = jnp.concatenate(rows, axis=0)
        mx = jnp.max(s, axis=-1, keepdims=True)
        p = jnp.exp(s - mx)
        den = jnp.sum(p, axis=-1, keepdims=True)
        pb = p.astype(BF16)
        yield None
        o = (_dot_nt(pb[:, :past], vt) + _dot(pb[:, past:], v_new)) / den
        out = jnp.where(own[0], o[0:s_new], 0.0)
        for i in range(1, n_hg):
            out = out + jnp.where(own[i], o[i * s_new:(i + 1) * s_new], 0.0)
        o_ref[tok, cols] = out
        yield None
    yield True


def _fox_sample_call(pt_flat, qb, small, foxn, pool_t, logf_t, psel, triu, eye, n_batch, s_new, pages_per_seq):
    past = pages_per_seq * PAGE_SIZE
    kv_rows = 2 * N_HEADS * HEAD_DIM
    n_seq = 2 if n_batch % 2 == 0 else 1
    row = lambda b, pt: (b, 0)
    fixed = lambda b, pt: (0, 0)

    def page_of(k):
        i, j = divmod(k, pages_per_seq)
        return lambda b, pt: pt[(b * n_seq + i) * pages_per_seq + j]

    def page_spec(k):
        page = page_of(k)
        return pl.BlockSpec((kv_rows, PAGE_SIZE), lambda b, pt: (page(b, pt), 0))

    def logf_spec(k):
        page = page_of(k)
        return pl.BlockSpec((None,) + logf_t.shape[1:], lambda b, pt: (page(b, pt), 0, 0))

    total = n_seq * pages_per_seq
    grid_spec = pltpu.PrefetchScalarGridSpec(
        num_scalar_prefetch=1,
        grid=(n_batch // n_seq,),
        in_specs=[pl.BlockSpec((n_seq * s_new, qb.shape[1]), row), pl.BlockSpec((n_seq * s_new, LANES), row),
                  pl.BlockSpec((n_seq * s_new, foxn.shape[1]), row)]
                 + [page_spec(k) for k in range(total)]
                 + [logf_spec(k) for k in range(total)]
                 + [pl.BlockSpec(psel.shape, fixed), pl.BlockSpec(triu.shape, fixed), pl.BlockSpec(eye.shape, fixed)],
        out_specs=pl.BlockSpec((n_seq * s_new, qb.shape[1]), row),
    )
    return pl.pallas_call(
        functools.partial(_fox_sample_kernel, n_pages=pages_per_seq, n_seq=n_seq, past=past),
        grid_spec=grid_spec,
        out_shape=jax.ShapeDtypeStruct(qb.shape, F32),
        compiler_params=_params(("arbitrary",)),
        name="fox_sample",
    )(pt_flat, qb, small, foxn, *([pool_t] * total), *([logf_t] * total), psel, triu, eye)


def _merge_kernel(x_ref, oa_ref, ob_ref, z_ref, w_ref, y_ref):
    d_a = oa_ref.shape[1]
    z = z_ref[...]
    gate = z * _sigmoid(z)
    u = jnp.concatenate([oa_ref[...] * gate[:, 0:d_a], ob_ref[...] * gate[:, d_a:]], axis=1).astype(BF16)
    y_ref[...] = x_ref[...] + _dot(u, w_ref[...])


def _merge_call(x2d, oa, ob, z, w_out_bf, tm, name):
    n, d = x2d.shape
    row = lambda i: (i, 0)
    return pl.pallas_call(
        _merge_kernel,
        grid=(n // tm,),
        in_specs=[pl.BlockSpec((tm, d), row), pl.BlockSpec((tm, oa.shape[1]), row), pl.BlockSpec((tm, ob.shape[1]), row),
                  pl.BlockSpec((tm, z.shape[1]), row), pl.BlockSpec(w_out_bf.shape, lambda i: (0, 0))],
        out_specs=pl.BlockSpec((tm, d), row),
        out_shape=jax.ShapeDtypeStruct((n, d), F32),
        compiler_params=_params(("arbitrary",)),
        name=name,
    )(x2d, oa, ob, z, w_out_bf)


def _row_tile(n, preferred):
    return preferred if n % preferred == 0 else n


def _rope_tables(pos):
    half = HEAD_DIM // 2
    inv = jnp.power(jnp.float32(ROPE_THETA), -jnp.arange(half, dtype=F32) / half)
    ang = pos.astype(F32)[:, None] * inv[None, :]
    cos = jnp.cos(ang)
    sin = jnp.sin(ang)
    return jnp.tile(cos, (1, 4)), jnp.tile(jnp.concatenate([-sin, sin], axis=1), (1, 2))


def _cmp_to_sel(n_cmp, n_sel, rows, cols):
    cs = np.arange(n_cmp) * CMP_STRIDE
    ss = np.arange(n_sel) * SEL_BLOCK
    ov = np.clip(np.minimum(cs[:, None] + CMP_LEN, ss[None, :] + SEL_BLOCK) - np.maximum(cs[:, None], ss[None, :]), 0, None)
    m = np.zeros((rows, cols), np.float32)
    m[:n_cmp, :n_sel] = ov / CMP_LEN
    return jnp.asarray(m, dtype=BF16)


def _block_expand(n_keys):
    e = np.zeros((LANES, n_keys), np.float32)
    k = np.arange(n_keys)
    e[k // SEL_BLOCK, k] = 1.0
    return jnp.asarray(e, dtype=BF16)


def _gate_expand():
    e = np.zeros((LANES, 3 * N_HEADS * HEAD_DIM), np.float32)
    for i in range(3):
        for h in range(N_HEADS):
            e[i * N_HEADS + h, i * N_HEADS * HEAD_DIM + h * HEAD_DIM:i * N_HEADS * HEAD_DIM + (h + 1) * HEAD_DIM] = 1.0
    return jnp.asarray(e, dtype=BF16)


def _head_expand():
    e = np.zeros((LANES, N_HEADS * HEAD_DIM), np.float32)
    for h in range(N_HEADS):
        e[LOGF_LANE + h, h * HEAD_DIM:(h + 1) * HEAD_DIM] = 1.0
    return jnp.asarray(e, dtype=BF16)


def _compress_weights(w1, w2, pe):
    half = CMP_LEN // 2
    eye = jnp.eye(G_NSA, dtype=F32)
    w = w1.reshape(2, half, HEAD_DIM, CMP_HID)
    w1x = jnp.einsum('pldh,gk->lgdpkh', w, eye).reshape(half * G_NSA * HEAD_DIM, 2 * G_NSA * CMP_HID).astype(BF16)
    w2x = jnp.einsum('hd,gk->ghkd', w2, eye).reshape(G_NSA * CMP_HID, G_NSA * HEAD_DIM).astype(BF16)
    pe2 = jnp.broadcast_to(pe.reshape(2, half, 1, HEAD_DIM), (2, half, G_NSA, HEAD_DIM)).reshape(2, -1)
    pex = jnp.concatenate([pe2, jnp.zeros((14, pe2.shape[1]), F32)], axis=0).astype(BF16)
    return w1x, w2x, pex


def _stored_tiles(cache):
    return jnp.transpose(cache, (0, 2, 3, 4, 1)).reshape(-1, cache.shape[1])


def kernel(x_prompt, x_sample, cache_nsa_cmp_kv, cache_nsa_slc_kv, cache_nsa_win_kv, cache_fox_kv, cache_fox_logf,
           page_table, g_norm, w_in, b_f, gq_a, gk_cmp, gk_slc, gk_win, pe_cmp_k, pe_cmp_v,
           w_cmp1_k, w_cmp2_k, w_cmp1_v, w_cmp2_v, gq_b, gk_b, w_out):
    depth = g_norm.shape[0]
    assert depth == 1, "single-layer stack"
    n_b, seq, d_model = x_prompt.shape
    n_db, s_new, _ = x_sample.shape
    pages_per_seq = page_table.shape[1]
    past = pages_per_seq * PAGE_SIZE
    wb = cache_nsa_win_kv.shape[2]
    n_pool = cache_nsa_cmp_kv.shape[1]
    d_a = N_HEADS * HEAD_DIM
    kv_w = 2 * G_NSA * HEAD_DIM

    w = w_in[0]
    o_kc, o_ga, o_za, o_qb, o_kb, o_fb, o_zb = 512, 1280, 1304, 1816, 2328, 3352, 3360
    gate_cols = o_ga + (np.arange(N_HEADS)[None, :] * 3 + np.arange(3)[:, None]).reshape(-1)
    small_w = jnp.concatenate([w[:, gate_cols], w[:, o_fb:o_fb + N_HEADS],
                               jnp.zeros((d_model, LANES - GATE_LANES - N_HEADS), F32)], axis=1)
    w_bf = jnp.concatenate([w[:, 0:o_ga], w[:, o_za:o_qb], w[:, o_zb:o_zb + d_a], w[:, o_qb:o_fb], small_w],
                           axis=1).astype(BF16)
    assert w_bf.shape[1] == C_END
    gains = jnp.concatenate([jnp.tile(v[0], 2)[None, :] for v in (gq_a, gk_slc, gk_win, gq_b, gk_b)]
                            + [jnp.zeros((3, LANES), F32)], axis=0)
    bf_row = jnp.zeros((1, LANES), F32).at[0, LOGF_LANE:LOGF_LANE + N_HEADS].set(b_f[0])
    lane_head = np.arange(2 * LANES) // HEAD_DIM
    ones_bd2 = jnp.asarray(lane_head[:, None] == lane_head[None, :], dtype=BF16)
    ones_bd = ones_bd2[0:LANES, 0:LANES]
    hexp = _head_expand()
    gexp = _gate_expand()
    consts = lambda tm: (g_norm, w_bf, gains, bf_row, ones_bd2,
                         jnp.asarray(np.tril(np.ones((tm, tm), np.float32)), dtype=BF16), hexp)

    tm_p = 256
    cos_p, sin_p = _rope_tables(jnp.arange(seq, dtype=jnp.int32))
    tiles_per_seq = seq // tm_p
    (cmp_p, slc_p, win_p, fox_p, small_p, z_p, qa_p, qb_p, cexp_p, csum_p, cmpd_p, slcb_p, winb_p, foxb_p) = _proj_call(
        x_prompt.reshape(n_b * seq, d_model), (cos_p, sin_p, lambda i: (i % tiles_per_seq, 0)),
        consts(tm_p), tiles_per_seq, tm_p, "proj_prompt", True, LOG2_E)
    tm_s = _row_tile(n_db * s_new, 256)
    cos_s, sin_s = _rope_tables(past + jnp.arange(s_new, dtype=jnp.int32))
    cos_s, sin_s = jnp.tile(cos_s, (tm_s // s_new, 1)), jnp.tile(sin_s, (tm_s // s_new, 1))
    (cmp_s, slc_s, win_s, fox_s, small_s, z_s, qa_s, qb_s, _, _, _, _, _, _) = _proj_call(
        x_sample.reshape(n_db * s_new, d_model), (cos_s, sin_s, lambda i: (0, 0)),
        consts(tm_s), 1, tm_s, "proj_sample", False, 1.0)

    w1k, w2k, pek = _compress_weights(w_cmp1_k[0], w_cmp2_k[0], pe_cmp_k[0])
    w1v, w2v, pev = _compress_weights(w_cmp1_v[0], w_cmp2_v[0], pe_cmp_v[0])
    cw = (w1k, w1v, pek, pev, w2k, w2v, jnp.tile(gk_cmp[0], 2)[None, :], ones_bd)
    chunk_w = CMP_STRIDE * kv_w
    ckv_p = _compress_prompt_call(cmpd_p.reshape(n_b * seq // CMP_STRIDE, chunk_w), cw, seq // CMP_STRIDE)
    pt_flat = page_table.reshape(-1).astype(jnp.int32)
    tok = np.arange(PAGE_SIZE)
    perm = np.zeros((PAGE_SIZE, PAGE_SIZE), np.float32)
    perm[(tok % CMP_STRIDE) * (PAGE_SIZE // CMP_STRIDE) + tok // CMP_STRIDE, tok] = 1.0
    ckv_s = _compress_sample_call(pt_flat, _stored_tiles(cache_nsa_cmp_kv[0]), jnp.asarray(perm, dtype=BF16), cw,
                                  n_db, pages_per_seq, 2)

    n_cmp_p = (seq - CMP_LEN) // CMP_STRIDE + 1
    msel_p = _cmp_to_sel(n_cmp_p, seq // SEL_BLOCK, seq // CMP_STRIDE, LANES)
    tq_a = 512
    eblk_p = jnp.transpose(_block_expand(seq).reshape(LANES, seq // tq_a, tq_a), (1, 0, 2))
    oa_p = _nsa_prompt_call(qa_p, small_p, ckv_p, slcb_p, winb_p, msel_p, eblk_p, gexp, n_b, seq, tq_a)
    n_cmp_s = (past + s_new - CMP_LEN) // CMP_STRIDE + 1
    n_sel_s = -(-(past + s_new) // SEL_BLOCK)
    msel_s = _cmp_to_sel(n_cmp_s, n_sel_s, past // CMP_STRIDE, LANES)
    eye_kv = jnp.asarray(np.eye(kv_w, dtype=np.float32), dtype=BF16)
    oa_s, new_win_t = _nsa_sample_call(
        pt_flat, qa_s, small_s, ckv_s, _stored_tiles(cache_nsa_slc_kv[0]), slc_s,
        _stored_tiles(cache_nsa_win_kv[0]), win_s, msel_s, _block_expand(past + PAGE_SIZE), gexp, eye_kv,
        n_db, s_new, pages_per_seq)
    new_win = jnp.transpose(new_win_t.reshape(n_db, 2, G_NSA, HEAD_DIM, wb), (0, 4, 1, 2, 3))[None]

    tq_f = 512
    c_p = csum_p[:, LOGF_LANE:LOGF_LANE + N_HEADS].reshape(n_b, seq, N_HEADS)
    ckt = jnp.transpose(c_p, (0, 2, 1)).reshape(n_b, N_HEADS, seq // tq_f, tq_f)
    ob_p = _fox_prompt_call(qb_p, foxb_p, cexp_p, ckt, n_b, seq, tq_f)
    psel = jnp.asarray(np.arange(LANES)[None, :] == (LOGF_LANE + np.arange(2 * N_HEADS))[:, None], dtype=BF16)
    psel = psel.at[N_HEADS:].set(0)
    triu = jnp.asarray(np.triu(np.ones((LANES, LANES), np.float32)), dtype=BF16)
    eye = jnp.asarray(np.eye(LANES, dtype=np.float32), dtype=BF16)
    ob_s = _fox_sample_call(
        pt_flat, qb_s, small_s, fox_s, _stored_tiles(cache_fox_kv[0]),
        jnp.transpose(cache_fox_logf[0], (0, 2, 1)), psel, triu, eye, n_db, s_new, pages_per_seq)

    w_out_bf = w_out[0].astype(BF16)
    y_p = _merge_call(x_prompt.reshape(n_b * seq, d_model), oa_p, ob_p, z_p, w_out_bf, 512, "merge_prompt")
    y_s = _merge_call(x_sample.reshape(n_db * s_new, d_model), oa_s, ob_s, z_s, w_out_bf,
                      _row_tile(n_db * s_new, 512), "merge_sample")

    kv5 = lambda a, nb, t, heads: a.reshape(1, nb, t, 2, heads, HEAD_DIM)
    kv5_t = lambda a, heads: jnp.transpose(a.reshape(n_b, 2, heads, HEAD_DIM, -1), (0, 4, 1, 2, 3))[None]
    wbp = min(WINDOW, seq)
    return (y_p.reshape(n_b, seq, d_model), y_s.reshape(n_db, s_new, d_model),
            kv5_t(cmp_p, G_NSA), kv5(cmp_s, n_db, s_new, G_NSA),
            kv5_t(slc_p, G_NSA), kv5(slc_s, n_db, s_new, G_NSA),
            kv5_t(win_p.reshape(n_b, kv_w, seq)[:, :, seq - wbp:], G_NSA), new_win,
            kv5_t(fox_p, N_HEADS), kv5(fox_s, n_db, s_new, N_HEADS),
            small_p[:, LOGF_LANE:LOGF_LANE + N_HEADS].reshape(1, n_b, seq, N_HEADS),
            small_s[:, LOGF_LANE:LOGF_LANE + N_HEADS].reshape(1, n_db, s_new, N_HEADS))
```

```python
import functools

import numpy as np
import jax
import jax.numpy as jnp
from jax import lax
from jax.experimental import pallas as pl
from jax.experimental.pallas import tpu as pltpu

F32 = jnp.float32
BF16 = jnp.bfloat16

HEAD_DIM = 64
LANES = 128
N_HEADS = 8
G_NSA = 2
R_NSA = N_HEADS // G_NSA
CMP_LEN = 32
CMP_STRIDE = 16
CMP_HID = 2 * HEAD_DIM
SEL_BLOCK = 64
SEL_TOP = 16
WINDOW = 512
PAGE_SIZE = 128
ROPE_THETA = 10000.0
EPS = 1e-6
NEG = -1e30
FORCED = 1e4
QK_SCALE = HEAD_DIM ** -0.5
LOG2_E = 1.4426950408889634

VMEM_LIMIT = 56 * 1024 * 1024

C_QA, C_CMP, C_SLC, C_WIN, C_Z, C_QB, C_FOX, C_SMALL, C_END = 0, 512, 768, 1024, 1280, 2304, 2816, 3840, 3968
GATE_LANES = 3 * N_HEADS
LOGF_LANE = GATE_LANES


def _dot(a, b):
    return jnp.dot(a, b, preferred_element_type=F32)


def _dot_nt(a, b):
    return lax.dot_general(a, b, (((1,), (1,)), ((), ())), preferred_element_type=F32)


def _split2(x):
    hi = x.astype(BF16)
    lo = (x - hi.astype(F32)).astype(BF16)
    return hi, lo


def _split3(x):
    hi = x.astype(BF16)
    r = x - hi.astype(F32)
    mid = r.astype(BF16)
    lo = (r - mid.astype(F32)).astype(BF16)
    return hi, mid, lo


def _dot_exact_lhs(x, m):
    a, b, c = _split3(x)
    return _dot(a, m) + _dot(b, m) + _dot(c, m)


def _dot_exact_rhs(m, x):
    a, b, c = _split3(x)
    return _dot(m, a) + _dot(m, b) + _dot(m, c)


def _sigmoid(x):
    return 1.0 / (1.0 + jnp.exp(-x))


def _params(sem):
    return pltpu.CompilerParams(dimension_semantics=sem, vmem_limit_bytes=VMEM_LIMIT)


def _proj_kernel(x_ref, g_ref, w_ref, cos_ref, sin_ref, gains_ref, bf_ref, ones_ref, tri_ref, hexp_ref,
                 cmp_ref, slc_ref, win_ref, fox_ref, small_ref, z_ref, qa_ref, qb_ref, cexp_ref, csum_ref, cmpd_ref,
                 slcb_ref, winb_ref, foxb_ref, carry_ref, *, tm, tiles_per_seq, kv_t, logit_scale):
    i = pl.program_id(0)
    qb_scale = QK_SCALE * logit_scale
    x = x_ref[...]
    ms = jnp.mean(x * x, axis=-1, keepdims=True)
    h = (x * lax.rsqrt(ms + EPS) * g_ref[...]).astype(BF16)
    cos = cos_ref[...]
    sin = sin_ref[...]
    lane = lax.broadcasted_iota(jnp.int32, (tm, LANES), 1)
    first_half = (lane & (HEAD_DIM // 2)) == 0
    ones = ones_ref[...]

    def proj(c0, c1):
        return _dot(h, w_ref[:, c0:c1])

    def sumsq_by_head(y):
        hi, lo = _split2(y * y)
        m = ones if y.shape[1] == 2 * LANES else ones[0:LANES, 0:LANES]
        return _dot(hi, m) + _dot(lo, m)

    def head_norm(y, row, ss=None):
        if ss is None:
            ss = sumsq_by_head(y)
        return y * lax.rsqrt(ss * (1.0 / HEAD_DIM) + EPS) * gains_ref[row:row + 1, :]

    def head_norm_chunks(y, row):
        out = []
        for s in range(2):
            ss2 = sumsq_by_head(y[:, 2 * s * LANES:2 * (s + 1) * LANES])
            for c in (2 * s, 2 * s + 1):
                out.append(head_norm(chunk(y, c), row, ss2[:, (c - 2 * s) * LANES:(c - 2 * s + 1) * LANES]))
        return out

    def rope(y):
        partner = jnp.where(first_half, pltpu.roll(y, LANES - HEAD_DIM // 2, 1), pltpu.roll(y, HEAD_DIM // 2, 1))
        return y * cos + partner * sin

    def chunk(y, c):
        return y[:, c * LANES:(c + 1) * LANES]

    def with_ones(v, e):
        return jnp.where((lane < HEAD_DIM) if e == 0 else (lane >= HEAD_DIM), v, 1.0)

    y = proj(C_QA, C_CMP)
    for c, t in enumerate(head_norm_chunks(y, 0)):
        qa_ref[:, c * LANES:(c + 1) * LANES] = rope(t) * QK_SCALE

    def store_kv(o_ref, t, c):
        if kv_t:
            o_ref[c * LANES:(c + 1) * LANES, :] = t.T
        else:
            o_ref[:, c * LANES:(c + 1) * LANES] = t

    y = proj(C_CMP, C_SLC)
    k = rope(chunk(y, 0))
    v = chunk(y, 1)
    cmpd_ref[:, 0:LANES] = k
    cmpd_ref[:, LANES:2 * LANES] = v
    store_kv(cmp_ref, k, 0)
    store_kv(cmp_ref, v, 1)

    for (c0, c1, row, o_ref, ob_ref) in ((C_SLC, C_WIN, 1, slc_ref, slcb_ref), (C_WIN, C_Z, 2, win_ref, winb_ref)):
        y = proj(c0, c1)
        k = rope(head_norm(chunk(y, 0), row))
        v = chunk(y, 1)
        store_kv(o_ref, k, 0)
        store_kv(o_ref, v, 1)
        ob_ref[:, 0:LANES] = k.astype(BF16)
        for e in range(2):
            ob_ref[:, (1 + e) * LANES:(2 + e) * LANES] = with_ones(v, e).astype(BF16)

    z_ref[...] = proj(C_Z, C_QB)

    y = proj(C_QB, C_FOX)
    for c, t in enumerate(head_norm_chunks(y, 3)):
        qb_ref[:, c * LANES:(c + 1) * LANES] = t * qb_scale

    y = proj(C_FOX, C_SMALL)
    kb = head_norm_chunks(y[:, 0:N_HEADS * HEAD_DIM], 4)
    for c in range(8):
        t = kb[c] if c < 4 else chunk(y, c)
        store_kv(fox_ref, t, c)
        if c < 4:
            foxb_ref[:, c * LANES:(c + 1) * LANES] = t.astype(BF16)
        else:
            for e in range(2):
                foxb_ref[:, (c + 4 * e) * LANES:(c + 4 * e + 1) * LANES] = with_ones(t, e).astype(BF16)

    raw = proj(C_SMALL, C_END)
    zf = raw + bf_ref[...]
    logf = jnp.minimum(zf, 0.0) - jnp.log1p(jnp.exp(-jnp.abs(zf)))
    small = jnp.where(lane < GATE_LANES, _sigmoid(raw), jnp.where(lane < GATE_LANES + N_HEADS, logf, 0.0))
    small_ref[...] = small

    @pl.when(i % tiles_per_seq == 0)
    def _():
        carry_ref[...] = jnp.zeros_like(carry_ref)

    c = carry_ref[0:1, :] + _dot_exact_rhs(tri_ref[...], small)
    carry_ref[0:1, :] = c[tm - 1:tm, :]
    c = c * logit_scale
    csum_ref[...] = c
    cexp_ref[...] = _dot_exact_lhs(c, hexp_ref[...])


def _proj_call(x2d, tables, consts, tiles_per_seq, tm, name, kv_t, logit_scale):
    n = x2d.shape[0]
    cos_t, sin_t, table_map = tables
    g_norm, w_bf, gains, bf_row, ones_bd, tri, hexp = consts
    row = lambda i: (i, 0)
    fixed = lambda i: (0, 0)
    full = lambda a: pl.BlockSpec(a.shape, fixed)
    kv_widths = (256, 256, 256, 1024)
    out_widths = (128, 1024, 512, 512, 512, 128, 256)
    bf_widths = (3 * LANES, 3 * LANES, 3 * N_HEADS * HEAD_DIM)
    if kv_t:
        seq = tiles_per_seq * tm
        out_shape = [jax.ShapeDtypeStruct((n // seq * w, seq), F32) for w in kv_widths]
        out_specs = [pl.BlockSpec((w, tm), lambda i: (i // tiles_per_seq, i % tiles_per_seq)) for w in kv_widths]
    else:
        out_shape = [jax.ShapeDtypeStruct((n, w), F32) for w in kv_widths]
        out_specs = [pl.BlockSpec((tm, w), row) for w in kv_widths]
    out_shape += [jax.ShapeDtypeStruct((n, w), F32) for w in out_widths]
    out_shape += [jax.ShapeDtypeStruct((n, w), BF16) for w in bf_widths]
    out_specs += [pl.BlockSpec((tm, w), row) for w in out_widths + bf_widths]
    return pl.pallas_call(
        functools.partial(_proj_kernel, tm=tm, tiles_per_seq=tiles_per_seq, kv_t=kv_t, logit_scale=logit_scale),
        grid=(n // tm,),
        in_specs=[pl.BlockSpec((tm, x2d.shape[1]), row), full(g_norm), full(w_bf),
                  pl.BlockSpec((tm, LANES), table_map), pl.BlockSpec((tm, LANES), table_map),
                  full(gains), full(bf_row), full(ones_bd), full(tri), full(hexp)],
        out_specs=out_specs,
        out_shape=out_shape,
        scratch_shapes=[pltpu.VMEM((8, LANES), F32)],
        compiler_params=_params(("arbitrary",)),
        name=name,
    )(x2d, g_norm, w_bf, cos_t, sin_t, gains, bf_row, ones_bd, tri, hexp)


def _compress_body(chunk_rows, m, wk_ref, wv_ref, pek_ref, pev_ref, w2k_ref, w2v_ref, gain_ref, ones_ref, out_ref,
                   rows_per_seq):
    hid = G_NSA * CMP_HID

    def branch(kind, w1_ref, pe_ref, w2_ref):
        r = _dot(jnp.concatenate([chunk_rows(kind), pe_ref[...]], axis=0), w1_ref[...])
        hh = r[:m, :hid] + pltpu.roll(r[:m, hid:], m - 1, 0) + r[m:m + 1, :hid] + r[m + 1:m + 2, hid:]
        act = hh * _sigmoid(hh)
        return _dot(act.astype(BF16), w2_ref[...])

    kc = branch(0, wk_ref, pek_ref, w2k_ref)
    hi, lo = _split2(kc * kc)
    ss = _dot(hi, ones_ref[...]) + _dot(lo, ones_ref[...])
    kc = kc * lax.rsqrt(ss * (1.0 / HEAD_DIM) + EPS) * gain_ref[...]
    vc = branch(1, wv_ref, pev_ref, w2v_ref)
    rows = lax.broadcasted_iota(jnp.int32, (m, LANES), 0)
    valid = (rows % rows_per_seq) != rows_per_seq - 1
    out_ref[:, 0:LANES] = jnp.where(valid, kc, 0.0)
    out_ref[:, LANES:2 * LANES] = jnp.where(valid, vc, 0.0)


def _compress_prompt_kernel(y_ref, *rest, rows_per_seq):
    y = y_ref[...].astype(BF16)
    width = 2 * G_NSA * HEAD_DIM

    def chunk_rows(kind):
        return jnp.concatenate([y[:, l * width + kind * LANES:l * width + (kind + 1) * LANES]
                                for l in range(CMP_STRIDE)], axis=1)

    _compress_body(chunk_rows, y.shape[0], *rest, rows_per_seq=rows_per_seq)


def _compress_sample_kernel(pt_ref, *refs, n_pages, rows_per_seq):
    del pt_ref
    page_refs = refs[:n_pages]
    perm_ref = refs[n_pages]
    chunks = PAGE_SIZE // CMP_STRIDE

    def chunk_rows(kind):
        tiles = [_dot_nt(perm_ref[...], r[kind * LANES:(kind + 1) * LANES, :].astype(BF16)) for r in page_refs]
        return jnp.concatenate([jnp.concatenate([t[l * chunks:(l + 1) * chunks] for t in tiles], axis=0)
                                for l in range(CMP_STRIDE)], axis=1).astype(BF16)

    _compress_body(chunk_rows, n_pages * chunks, *refs[n_pages + 1:], rows_per_seq=rows_per_seq)


def _compress_prompt_call(y2d, cw, rows_per_seq):
    n = y2d.shape[0]
    fixed = lambda i: (0, 0)
    return pl.pallas_call(
        functools.partial(_compress_prompt_kernel, rows_per_seq=rows_per_seq),
        grid=(n // rows_per_seq,),
        in_specs=[pl.BlockSpec((rows_per_seq, y2d.shape[1]), lambda i: (i, 0))] + [pl.BlockSpec(a.shape, fixed) for a in cw],
        out_specs=pl.BlockSpec((rows_per_seq, 2 * LANES), lambda i: (i, 0)),
        out_shape=jax.ShapeDtypeStruct((n, 2 * LANES), F32),
        compiler_params=_params(("arbitrary",)),
        name="compress_prompt",
    )(y2d, *cw)


def _compress_sample_call(pt_flat, pool_t, perm, cw, n_batch, pages_per_seq, batch_per_step):
    kv_rows = 2 * G_NSA * HEAD_DIM
    rows_per_seq = pages_per_seq * (PAGE_SIZE // CMP_STRIDE)
    n_pages = batch_per_step * pages_per_seq
    fixed = lambda i, pt: (0, 0)

    def page_spec(k):
        bb, j = divmod(k, pages_per_seq)
        return pl.BlockSpec((kv_rows, PAGE_SIZE), lambda i, pt: (pt[(i * batch_per_step + bb) * pages_per_seq + j], 0))

    m = batch_per_step * rows_per_seq
    grid_spec = pltpu.PrefetchScalarGridSpec(
        num_scalar_prefetch=1,
        grid=(n_batch // batch_per_step,),
        in_specs=[page_spec(k) for k in range(n_pages)] + [pl.BlockSpec(a.shape, fixed) for a in (perm,) + tuple(cw)],
        out_specs=pl.BlockSpec((m, 2 * LANES), lambda i, pt: (i, 0)),
    )
    return pl.pallas_call(
        functools.partial(_compress_sample_kernel, n_pages=n_pages, rows_per_seq=rows_per_seq),
        grid_spec=grid_spec,
        out_shape=jax.ShapeDtypeStruct((n_batch * rows_per_seq, 2 * LANES), F32),
        compiler_params=_params(("arbitrary",)),
        name="compress_sample",
    )(pt_flat, *([pool_t] * n_pages), perm, *cw)


def _group_queries(qa, g, lane):
    in_group = (lane < HEAD_DIM) if g == 0 else (lane >= HEAD_DIM)
    rows = []
    for r in range(R_NSA):
        head = R_NSA * g + r
        y = qa[:, (head // 2) * LANES:(head // 2 + 1) * LANES]
        if head % 2 != g:
            y = pltpu.roll(y, HEAD_DIM, 1)
        rows.append(jnp.where(in_group, y, 0.0))
    return jnp.concatenate(rows, axis=0)


def _assemble_heads(res, tq, lane):
    chunks = []
    for c in range(N_HEADS // 2):
        g = c // 2
        a = res[g][(2 * (c % 2)) * tq:(2 * (c % 2) + 1) * tq]
        b = res[g][(2 * (c % 2) + 1) * tq:(2 * (c % 2) + 2) * tq]
        if g == 0:
            b = pltpu.roll(b, HEAD_DIM, 1)
        else:
            a = pltpu.roll(a, HEAD_DIM, 1)
        chunks.append(jnp.where(lane < HEAD_DIM, a, b))
    return jnp.concatenate(chunks, axis=1)


def _flash_step(s, n_rb, rb, fix, values, m_ref, acc_ref, base2=False):
    exp = jnp.exp2 if base2 else jnp.exp
    n_c = s.shape[1] // LANES
    cols = [jnp.concatenate([fix(r, c, s[r * rb:(r + 1) * rb, c * LANES:(c + 1) * LANES]) for r in range(n_rb)], axis=0)
            for c in range(n_c)]
    mx = cols[0]
    for c in range(1, n_c):
        mx = jnp.maximum(mx, cols[c])
    m_prev = m_ref[...]
    m_new = jnp.maximum(m_prev, jnp.max(mx, axis=-1, keepdims=True))
    alpha = exp(m_prev - m_new)
    p = jnp.concatenate([exp(x - m_new) for x in cols], axis=1).astype(BF16)
    if len(values) == 1:
        pv = _dot(p, values[0])
    else:
        rows_per_v = n_rb * rb // len(values)
        pv = jnp.concatenate([_dot(p[i * rows_per_v:(i + 1) * rows_per_v], v) for i, v in enumerate(values)], axis=0)
    acc_ref[...] = alpha * acc_ref[...] + pv
    m_ref[...] = m_new


def _flash_init(m_ref, acc_ref):
    m_ref[...] = jnp.full(m_ref.shape, NEG, F32)
    acc_ref[...] = jnp.zeros(acc_ref.shape, F32)


def _flash_result(acc_ref):
    acc = acc_ref[...]
    return acc / pltpu.roll(acc, HEAD_DIM, 1)


def _block_ranks(impm, n_blocks, lane):
    cnt = jnp.zeros(impm.shape, F32)
    for j in range(n_blocks):
        col = impm[:, j:j + 1]
        ge = jnp.where(col >= impm, 1.0, 0.0)
        gt = jnp.where(col > impm, 1.0, 0.0)
        cnt = cnt + jnp.where(lane > j, ge, gt)
    return cnt


def _compressed_branch(qz, ck, cv, cmask, msel, n_heads, tq):
    q_hi, q_lo = _split2(qz)
    k_hi, k_lo = _split2(ck)
    s = _dot_nt(q_hi, k_hi) + _dot_nt(q_hi, k_lo) + _dot_nt(q_lo, k_hi)
    ps = []
    psum = None
    for r in range(n_heads):
        sr = jnp.where(cmask, s[r * tq:(r + 1) * tq], NEG)
        mx = jnp.max(sr, axis=-1, keepdims=True)
        e = jnp.where(cmask, jnp.exp(sr - mx), 0.0)
        den = jnp.sum(e, axis=-1, keepdims=True)
        p = e / jnp.where(den > 0.0, den, 1.0)
        ps.append(p)
        psum = p if psum is None else psum + p
    o = _dot(jnp.concatenate(ps, axis=0).astype(BF16), cv)
    return o, _dot_exact_lhs(psum, msel)


def _select_blocks(imp, tpos, lane, n_blocks):
    forced = (lane == 0) | (lane == tpos // SEL_BLOCK)
    causal = lane * SEL_BLOCK <= tpos
    impm = jnp.where(forced, FORCED, jnp.where(causal, imp, -1.0))
    impm = jnp.where(lane < n_blocks, impm, -3e38)
    cnt = _block_ranks(impm, n_blocks, lane)
    n_top = min(SEL_TOP, n_blocks)
    return jnp.where(cnt < n_top, jnp.where(lane < n_blocks, 1.0, 0.0), 0.0)


def _select_blocks_t(imp, s0, n_blocks, cnt_ref):
    tq = imp.shape[0]
    grp = 8
    imp_t = jnp.transpose(imp)[0:n_blocks]
    blk = lax.broadcasted_iota(jnp.int32, (n_blocks, tq), 0)
    tpos = s0 + lax.broadcasted_iota(jnp.int32, (n_blocks, tq), 1)
    forced = (blk == 0) | (blk == tpos // SEL_BLOCK)
    impm = jnp.where(forced, FORCED, jnp.where(blk * SEL_BLOCK <= tpos, imp_t, -1.0))
    cnt_ref[...] = jnp.zeros(cnt_ref.shape, F32)
    sub = lax.broadcasted_iota(jnp.int32, (grp, tq), 0)
    for jb in range(n_blocks // grp):
        @pl.when(jb * grp * SEL_BLOCK <= s0 + tq - 1)
        def _(jb=jb):
            for rg in range(n_blocks // grp):
                x = impm[rg * grp:(rg + 1) * grp]
                cnt = cnt_ref[rg * grp:(rg + 1) * grp, :]
                for j in range(jb * grp, (jb + 1) * grp):
                    row = impm[j:j + 1, :]
                    if rg > jb:
                        beat = row >= x
                    elif rg < jb:
                        beat = row > x
                    else:
                        cnt = cnt + jnp.where(sub > j - jb * grp, jnp.where(row >= x, 1.0, 0.0),
                                              jnp.where(row > x, 1.0, 0.0))
                        continue
                    cnt = cnt + jnp.where(beat, 1.0, 0.0)
                cnt_ref[rg * grp:(rg + 1) * grp, :] = cnt
    sel_t = jnp.where(cnt_ref[...] < min(SEL_TOP, n_blocks), 1.0, 0.0)
    return jnp.transpose(jnp.concatenate([sel_t, jnp.zeros((LANES - n_blocks, tq), F32)], axis=0))


def _nsa_prompt_kernel(qa_ref, small_ref, cmp_ref, slc_ref, win_ref, msel_ref, eblk_ref, gexp_ref,
                       oa_ref, m_ref, acc_ref, cnt_ref, *, tq, n_sel):
    tk = tq
    qi = pl.program_id(1)
    s0 = qi * tq
    lane = lax.broadcasted_iota(jnp.int32, (tq, LANES), 1)
    trow = lax.broadcasted_iota(jnp.int32, (tq, LANES), 0)
    tpos = s0 + trow
    row_t = lax.broadcasted_iota(jnp.int32, (tq, tk), 0)
    col_t = lax.broadcasted_iota(jnp.int32, (tq, tk), 1)
    diag_mask = col_t <= row_t
    n_back = WINDOW // tk
    assert n_back * tk == WINDOW
    far_mask = (n_back * tk + row_t - col_t) <= WINDOW
    n_cmp = cmp_ref.shape[0]
    cmask = (lax.broadcasted_iota(jnp.int32, (tq, n_cmp), 1) * CMP_STRIDE + CMP_LEN - 1
             <= s0 + lax.broadcasted_iota(jnp.int32, (tq, n_cmp), 0))
    qa = qa_ref[...]
    ck = cmp_ref[:, 0:LANES]
    cv = cmp_ref[:, LANES:2 * LANES].astype(BF16)

    def chunk_of(mask, c):
        return mask[:, c * LANES:(c + 1) * LANES]

    o_cmp, sel = [], []
    qz = [_group_queries(qa, g, lane) for g in range(G_NSA)]
    for g in range(G_NSA):
        oc, imp = _compressed_branch(qz[g], ck, cv, cmask, msel_ref[...], R_NSA, tq)
        o_cmp.append(oc)
        sel.append(_select_blocks_t(imp, s0, n_sel, cnt_ref).astype(BF16))
    q_all = jnp.concatenate(qz, axis=0).astype(BF16)

    def flash_step(kv_ref, kt, biases, mask):
        off = pl.multiple_of(kt * tk, tk)
        k2 = kv_ref[pl.ds(off, tk), 0:LANES]
        values = [kv_ref[pl.ds(off, tk), (1 + g) * LANES:(2 + g) * LANES] for g in range(G_NSA)]

        def fix(r, c, x):
            if biases is not None:
                x = x + chunk_of(biases[r // R_NSA], c)
            if mask is not None:
                x = jnp.where(chunk_of(mask, c), x, NEG)
            return x

        _flash_step(_dot_nt(q_all, k2), G_NSA * R_NSA, tq, fix, values, m_ref, acc_ref)

    def sel_biases(kt):
        return [(_dot(sel[g], eblk_ref[kt]) - 1.0) * (-NEG) for g in range(G_NSA)]

    def per_group(o):
        return [o[g * R_NSA * tq:(g + 1) * R_NSA * tq] for g in range(G_NSA)]

    _flash_init(m_ref, acc_ref)

    def sel_body(kt, carry):
        flash_step(slc_ref, kt, sel_biases(kt), None)
        return carry

    lax.fori_loop(0, qi, sel_body, 0)
    flash_step(slc_ref, qi, sel_biases(qi), diag_mask)
    o_sel = per_group(_flash_result(acc_ref))

    _flash_init(m_ref, acc_ref)

    for back in range(n_back, 0, -1):
        @pl.when(qi >= back)
        def _(back=back):
            flash_step(win_ref, qi - back, None, far_mask if back == n_back else None)

    flash_step(win_ref, qi, None, diag_mask)
    o_win = per_group(_flash_result(acc_ref))

    gates = _dot_exact_lhs(small_ref[...], gexp_ref[...])
    d_a = N_HEADS * HEAD_DIM
    oa_ref[...] = (gates[:, 0:d_a] * _assemble_heads(o_cmp, tq, lane)
                   + gates[:, d_a:2 * d_a] * _assemble_heads(o_sel, tq, lane)
                   + gates[:, 2 * d_a:3 * d_a] * _assemble_heads(o_win, tq, lane))


def _nsa_prompt_call(qa, small, cmpkv, slcb, winb, msel, eblk, gexp, n_batch, seq, tq):
    nq = seq // tq
    n_cmp_rows = cmpkv.shape[0] // n_batch
    n_sel = seq // SEL_BLOCK
    qrow = lambda b, q: (b * nq + q, 0)
    per_b = lambda b, q: (b, 0)
    fixed = lambda b, q: (0, 0)
    return pl.pallas_call(
        functools.partial(_nsa_prompt_kernel, tq=tq, n_sel=n_sel),
        grid=(n_batch, nq),
        in_specs=[pl.BlockSpec((tq, qa.shape[1]), qrow), pl.BlockSpec((tq, LANES), qrow),
                  pl.BlockSpec((n_cmp_rows, 2 * LANES), per_b),
                  pl.BlockSpec((seq, 3 * LANES), per_b), pl.BlockSpec((seq, 3 * LANES), per_b),
                  pl.BlockSpec(msel.shape, fixed), pl.BlockSpec(eblk.shape, lambda b, q: (0, 0, 0)),
                  pl.BlockSpec(gexp.shape, fixed)],
        out_specs=pl.BlockSpec((tq, qa.shape[1]), qrow),
        out_shape=jax.ShapeDtypeStruct(qa.shape, F32),
        scratch_shapes=[pltpu.VMEM((G_NSA * R_NSA * tq, LANES), F32), pltpu.VMEM((G_NSA * R_NSA * tq, LANES), F32),
                        pltpu.VMEM((n_sel, tq), F32)],
        compiler_params=_params(("arbitrary", "arbitrary")),
        name="nsa_prompt",
    )(qa, small, cmpkv, slcb, winb, msel, eblk, gexp)


def _fox_prompt_kernel(q_ref, k_ref, v0_ref, v1_ref, cq_ref, ck_ref, o_ref, m_ref, acc_ref, *, tq):
    tk = tq
    qi = pl.program_id(2)
    lane = lax.broadcasted_iota(jnp.int32, (tq, LANES), 1)
    q = q_ref[...]
    qz = jnp.concatenate([jnp.where(lane < HEAD_DIM, q, 0.0), jnp.where(lane >= HEAD_DIM, q, 0.0)], axis=0).astype(BF16)
    cq = cq_ref[...]
    cq_sw = pltpu.roll(cq, HEAD_DIM, 1)
    cq_rep = [jnp.where(lane < HEAD_DIM, cq, cq_sw), jnp.where(lane >= HEAD_DIM, cq, cq_sw)]
    diag_mask = lax.broadcasted_iota(jnp.int32, (tq, tk), 1) <= lax.broadcasted_iota(jnp.int32, (tq, tk), 0)
    v_refs = (v0_ref, v1_ref)
    _flash_init(m_ref, acc_ref)

    def step(kt, mask):
        off = pl.multiple_of(kt * tk, tk)
        k2 = k_ref[pl.ds(off, tk), :]
        ck = [ck_ref[0, e, pl.ds(kt, 1), :] for e in range(2)]

        def fix(e, c, x):
            x = x + (cq_rep[e] - ck[e][:, c * LANES:(c + 1) * LANES])
            if mask is not None:
                x = jnp.where(mask[:, c * LANES:(c + 1) * LANES], x, NEG)
            return x

        _flash_step(_dot_nt(qz, k2), 2, tq, fix, [r[pl.ds(off, tk), :] for r in v_refs], m_ref, acc_ref, base2=True)

    def body(kt, carry):
        step(kt, None)
        return carry

    lax.fori_loop(0, qi, body, 0)
    step(qi, diag_mask)
    o = _flash_result(acc_ref)
    o_ref[...] = jnp.where(lane < HEAD_DIM, o[0:tq], o[tq:2 * tq])


def _fox_prompt_call(qb, foxb, cexp, ckt, n_batch, seq, tq):
    nq = seq // tq
    n_pairs = N_HEADS // 2
    qmap = lambda b, hp, q: (b * nq + q, hp)
    return pl.pallas_call(
        functools.partial(_fox_prompt_kernel, tq=tq),
        grid=(n_batch, n_pairs, nq),
        in_specs=[pl.BlockSpec((tq, LANES), qmap),
                  pl.BlockSpec((seq, LANES), lambda b, hp, q: (b, hp)),
                  pl.BlockSpec((seq, LANES), lambda b, hp, q: (b, n_pairs + hp)),
                  pl.BlockSpec((seq, LANES), lambda b, hp, q: (b, 2 * n_pairs + hp)),
                  pl.BlockSpec((tq, LANES), qmap),
                  pl.BlockSpec((1, 2, nq, tq), lambda b, hp, q: (b, hp, 0, 0))],
        out_specs=pl.BlockSpec((tq, LANES), qmap),
        out_shape=jax.ShapeDtypeStruct(qb.shape, F32),
        scratch_shapes=[pltpu.VMEM((2 * tq, LANES), F32), pltpu.VMEM((2 * tq, LANES), F32)],
        compiler_params=_params(("arbitrary", "arbitrary", "arbitrary")),
        name="fox_prompt",
    )(qb, foxb, foxb, foxb, cexp, ckt)


def _round_robin(stages):
    results = [None] * len(stages)
    pending = list(range(len(stages)))
    while pending:
        for i in list(pending):
            out = next(stages[i])
            if out is not None:
                results[i] = out
                pending.remove(i)
    return results


def _pad_rows(x, rows):
    return jnp.concatenate([x, jnp.zeros((rows - x.shape[0], x.shape[1]), x.dtype)], axis=0)


def _nsa_sample_kernel(pt_ref, qa_ref, small_ref, cmp_ref, *refs, n_pages, n_seq, past, n_sel):
    del pt_ref
    all_pages = refs[:n_pages * n_seq]
    (slcn_ref, winbuf_ref, winn_ref, msel_ref, eblk_ref, gexp_ref, eye_ref, oa_ref, newwin_ref) = refs[n_pages * n_seq:]
    s_new = qa_ref.shape[0] // n_seq
    wb = winbuf_ref.shape[1]
    kd = G_NSA * HEAD_DIM
    n_keys = past + PAGE_SIZE
    lane = lax.broadcasted_iota(jnp.int32, (s_new, LANES), 1)
    tpos = past + lax.broadcasted_iota(jnp.int32, (s_new, LANES), 0)
    n_cmp = cmp_ref.shape[0] // n_seq
    cmask = (lax.broadcasted_iota(jnp.int32, (s_new, n_cmp), 1) * CMP_STRIDE + CMP_LEN - 1
             <= past + lax.broadcasted_iota(jnp.int32, (s_new, n_cmp), 0))
    kpos = lax.broadcasted_iota(jnp.int32, (s_new, n_keys), 1)
    key_ok = kpos <= past + lax.broadcasted_iota(jnp.int32, (s_new, n_keys), 0)
    w_keys = wb + PAGE_SIZE
    wi = lax.broadcasted_iota(jnp.int32, (s_new, w_keys), 1)
    wt = lax.broadcasted_iota(jnp.int32, (s_new, w_keys), 0)
    dist = wb + wt - wi
    win_ok = (dist >= 0) & (dist <= WINDOW) & (past - wb + wi >= 0) & (wi < wb + s_new)
    lane_w = lax.broadcasted_iota(jnp.int32, (2 * kd, LANES), 1)
    stages = [_nsa_sample_one(seq_i, all_pages[seq_i * n_pages:(seq_i + 1) * n_pages], qa_ref, small_ref, cmp_ref,
                              slcn_ref, winbuf_ref, winn_ref, msel_ref, eblk_ref, gexp_ref, eye_ref,
                              s_new, n_cmp, n_sel, lane, tpos, cmask, key_ok, win_ok, lane_w)
              for seq_i in range(n_seq)]
    for seq_i, (oa, new_win) in enumerate(_round_robin(stages)):
        oa_ref[seq_i * s_new:(seq_i + 1) * s_new, :] = oa
        for c, chunk in enumerate(new_win):
            newwin_ref[seq_i * 2 * kd:(seq_i + 1) * 2 * kd, c * LANES:(c + 1) * LANES] = chunk


def _nsa_sample_one(seq_i, page_refs, qa_ref, small_ref, cmp_ref, slcn_ref, winbuf_ref, winn_ref, msel_ref, eblk_ref,
                    gexp_ref, eye_ref, s_new, n_cmp, n_sel, lane, tpos, cmask, key_ok, win_ok, lane_w):
    kd = G_NSA * HEAD_DIM
    wb = winbuf_ref.shape[1]
    tok = slice(seq_i * s_new, (seq_i + 1) * s_new)
    w0 = seq_i * 2 * kd
    qa = qa_ref[tok, :]
    ck = cmp_ref[seq_i * n_cmp:(seq_i + 1) * n_cmp, 0:LANES]
    cv = cmp_ref[seq_i * n_cmp:(seq_i + 1) * n_cmp, LANES:2 * LANES].astype(BF16)
    slcn = slcn_ref[tok, :]
    winn = winn_ref[tok, :]
    qz = [_group_queries(qa, g, lane) for g in range(G_NSA)]
    q_all = jnp.concatenate(qz, axis=0).astype(BF16)

    def attend(kt, vt, k_new, v_new, masks):
        n_past = kt.shape[1]
        s = jnp.concatenate([_dot(q_all, kt), _dot_nt(q_all, k_new)], axis=1)
        yield None
        s = jnp.concatenate([jnp.where(masks[g], s[(g * R_NSA + r) * s_new:(g * R_NSA + r + 1) * s_new], NEG)
                             for g in range(G_NSA) for r in range(R_NSA)], axis=0)
        mx = jnp.max(s, axis=-1, keepdims=True)
        p = jnp.exp(s - mx)
        den = jnp.sum(p, axis=-1, keepdims=True)
        pb = p.astype(BF16)
        yield None
        return (_dot_nt(pb[:, :n_past], vt) + _dot(pb[:, n_past:], v_new)) / den

    def new_rows(x):
        return _pad_rows(x, PAGE_SIZE).astype(BF16)

    o_cmp, sel_ok = [], []
    for g in range(G_NSA):
        oc, imp = _compressed_branch(qz[g], ck, cv, cmask, msel_ref[...], R_NSA, s_new)
        o_cmp.append(oc)
        yield None
        sel = _select_blocks(imp, tpos, lane, n_sel).astype(BF16)
        sel_ok.append((_dot(sel, eblk_ref[...]) > 0.5) & key_ok)
        yield None

    o_w = yield from attend(winbuf_ref[w0:w0 + kd, :].astype(BF16), winbuf_ref[w0 + kd:w0 + 2 * kd, :].astype(BF16),
                            new_rows(winn[:, 0:LANES]), new_rows(winn[:, LANES:2 * LANES]), [win_ok] * G_NSA)
    yield None
    kt = jnp.concatenate([r[0:kd, :] for r in page_refs], axis=1).astype(BF16)
    vt = jnp.concatenate([r[kd:2 * kd, :] for r in page_refs], axis=1).astype(BF16)
    o_s = yield from attend(kt, vt, new_rows(slcn[:, 0:LANES]), new_rows(slcn[:, LANES:2 * LANES]), sel_ok)
    yield None
    half = R_NSA * s_new
    o_sel = [o_s[0:half], o_s[half:2 * half]]
    o_win = [o_w[0:half], o_w[half:2 * half]]

    gates = _dot_exact_lhs(small_ref[tok, :], gexp_ref[...])
    d_a = N_HEADS * HEAD_DIM
    oa = (gates[:, 0:d_a] * _assemble_heads(o_cmp, s_new, lane)
          + gates[:, d_a:2 * d_a] * _assemble_heads(o_sel, s_new, lane)
          + gates[:, 2 * d_a:3 * d_a] * _assemble_heads(o_win, s_new, lane))

    xn = jnp.concatenate([jnp.zeros((LANES - s_new, 2 * LANES), F32), winn], axis=0)
    a, b, c = _split3(xn)
    eye = eye_ref[...]
    new_t = _dot_nt(eye, a) + _dot_nt(eye, b) + _dot_nt(eye, c)
    n_chunks = wb // LANES
    shifted = [pltpu.roll(winbuf_ref[w0:w0 + 2 * kd, c * LANES:(c + 1) * LANES], LANES - s_new, 1)
               for c in range(n_chunks)]
    shifted.append(new_t)
    new_win = [jnp.where(lane_w < LANES - s_new, shifted[c], shifted[c + 1]) for c in range(n_chunks)]
    yield oa, new_win


def _nsa_sample_call(pt_flat, qa, small, cmpkv, pool_t, slcn, winbuf_t, winn, msel, eblk, gexp, eye,
                     n_batch, s_new, pages_per_seq):
    past = pages_per_seq * PAGE_SIZE
    n_sel = -(-(past + s_new) // SEL_BLOCK)
    n_cmp_rows = cmpkv.shape[0] // n_batch
    kv_rows = 2 * G_NSA * HEAD_DIM
    wb = winbuf_t.shape[1]
    n_seq = next(n for n in (4, 2, 1) if n_batch % n == 0)
    row = lambda b, pt: (b, 0)
    fixed = lambda b, pt: (0, 0)

    def page_spec(k):
        i, j = divmod(k, pages_per_seq)
        return pl.BlockSpec((kv_rows, PAGE_SIZE), lambda b, pt: (pt[(b * n_seq + i) * pages_per_seq + j], 0))

    grid_spec = pltpu.PrefetchScalarGridSpec(
        num_scalar_prefetch=1,
        grid=(n_batch // n_seq,),
        in_specs=[pl.BlockSpec((n_seq * s_new, qa.shape[1]), row), pl.BlockSpec((n_seq * s_new, LANES), row),
                  pl.BlockSpec((n_seq * n_cmp_rows, 2 * LANES), row)]
                 + [page_spec(k) for k in range(n_seq * pages_per_seq)]
                 + [pl.BlockSpec((n_seq * s_new, 2 * LANES), row), pl.BlockSpec((n_seq * kv_rows, wb), row),
                    pl.BlockSpec((n_seq * s_new, 2 * LANES), row),
                    pl.BlockSpec(msel.shape, fixed), pl.BlockSpec(eblk.shape, fixed), pl.BlockSpec(gexp.shape, fixed),
                    pl.BlockSpec(eye.shape, fixed)],
        out_specs=[pl.BlockSpec((n_seq * s_new, qa.shape[1]), row), pl.BlockSpec((n_seq * kv_rows, wb), row)],
    )
    return pl.pallas_call(
        functools.partial(_nsa_sample_kernel, n_pages=pages_per_seq, n_seq=n_seq, past=past, n_sel=n_sel),
        grid_spec=grid_spec,
        out_shape=[jax.ShapeDtypeStruct(qa.shape, F32), jax.ShapeDtypeStruct(winbuf_t.shape, F32)],
        compiler_params=_params(("arbitrary",)),
        name="nsa_sample",
    )(pt_flat, qa, small, cmpkv, *([pool_t] * (n_seq * pages_per_seq)), slcn, winbuf_t, winn, msel, eblk, gexp, eye)


def _fox_sample_kernel(pt_ref, q_ref, small_ref, foxn_ref, *refs, n_pages, n_seq, past):
    del pt_ref
    total = n_pages * n_seq
    psel_ref, triu_ref, eye_ref, o_ref = refs[2 * total:]
    s_new = q_ref.shape[0] // n_seq
    n_keys = past + PAGE_SIZE
    key_ok = (lax.broadcasted_iota(jnp.int32, (s_new, n_keys), 1)
              <= past + lax.broadcasted_iota(jnp.int32, (s_new, n_keys), 0))
    _round_robin([_fox_sample_one(seq_i, refs[seq_i * n_pages:(seq_i + 1) * n_pages],
                                  refs[total + seq_i * n_pages:total + (seq_i + 1) * n_pages],
                                  q_ref, small_ref, foxn_ref, psel_ref, triu_ref, eye_ref, o_ref, s_new, past, key_ok)
                  for seq_i in range(n_seq)])


def _fox_sample_one(seq_i, page_refs, logf_refs, q_ref, small_ref, foxn_ref, psel_ref, triu_ref, eye_ref, o_ref,
                    s_new, past, key_ok):
    n_pages = len(page_refs)
    tok = slice(seq_i * s_new, (seq_i + 1) * s_new)
    small_pad = _pad_rows(small_ref[tok, :], LANES)
    a, b, c = _split3(small_pad)
    psel = psel_ref[...]
    new_t = _dot_nt(psel, a) + _dot_nt(psel, b) + _dot_nt(psel, c)
    w0 = jnp.concatenate([r[...] for r in logf_refs] + [new_t], axis=0)
    wc = _dot_exact_lhs(w0, triu_ref[...])
    yield None
    run = jnp.zeros((N_HEADS, 1), F32)
    c_rows = []
    for j in range(n_pages + 1):
        blk = wc[j * N_HEADS:(j + 1) * N_HEADS]
        c_rows.append(blk + run)
        run = run + blk[:, LANES - 1:LANES]
    a, b, c = _split3(_pad_rows(c_rows[n_pages], LANES))
    eye = eye_ref[...]
    cq_t = _dot_nt(eye, a) + _dot_nt(eye, b) + _dot_nt(eye, c)
    yield None

    q = q_ref[tok, :]
    foxn = foxn_ref[tok, :]
    d_k = N_HEADS * HEAD_DIM
    n_hg = 4
    w_hg = n_hg * HEAD_DIM
    lane4 = lax.broadcasted_iota(jnp.int32, (s_new, w_hg), 1)
    own = [(lane4 >= i * HEAD_DIM) & (lane4 < (i + 1) * HEAD_DIM) for i in range(n_hg)]
    for hg in range(N_HEADS // n_hg):
        cols = slice(hg * w_hg, (hg + 1) * w_hg)
        q4 = q[:, cols]
        qbd = jnp.concatenate([jnp.where(own[i], q4, 0.0) for i in range(n_hg)], axis=0).astype(BF16)
        kt = jnp.concatenate([r[hg * w_hg:(hg + 1) * w_hg, :] for r in page_refs], axis=1).astype(BF16)
        vt = jnp.concatenate([r[d_k + hg * w_hg:d_k + (hg + 1) * w_hg, :] for r in page_refs], axis=1).astype(BF16)
        k_new = _pad_rows(foxn[:, cols], PAGE_SIZE).astype(BF16)
        v_new = _pad_rows(foxn[:, d_k + hg * w_hg:d_k + (hg + 1) * w_hg], PAGE_SIZE).astype(BF16)
        s = jnp.concatenate([_dot(qbd, kt), _dot_nt(qbd, k_new)], axis=1)
        yield None
        rows = []
        for i in range(n_hg):
            head = hg * n_hg + i
            ck_row = jnp.concatenate([cr[head:head + 1, :] for cr in c_rows], axis=1)
            sh = s[i * s_new:(i + 1) * s_new] + (cq_t[0:s_new, head:head + 1] - ck_row)
            rows.append(jnp.where(key_ok, sh, NEG))
        s = jnp.concatenate(rows, axis=0)
        mx = jnp.max(s, axis=-1, keepdims=True)
        p = jnp.exp(s - mx)
        den = jnp.sum(p, axis=-1, keepdims=True)
        pb = p.astype(BF16)
        yield None
        o = (_dot_nt(pb[:, :past], vt) + _dot(pb[:, past:], v_new)) / den
        out = jnp.where(own[0], o[0:s_new], 0.0)
        for i in range(1, n_hg):
            out = out + jnp.where(own[i], o[i * s_new:(i + 1) * s_new], 0.0)
        o_ref[tok, cols] = out
        yield None
    yield True


def _fox_sample_call(pt_flat, qb, small, foxn, pool_t, logf_t, psel, triu, eye, n_batch, s_new, pages_per_seq):
    past = pages_per_seq * PAGE_SIZE
    kv_rows = 2 * N_HEADS * HEAD_DIM
    n_seq = 2 if n_batch % 2 == 0 else 1
    row = lambda b, pt: (b, 0)
    fixed = lambda b, pt: (0, 0)

    def page_of(k):
        i, j = divmod(k, pages_per_seq)
        return lambda b, pt: pt[(b * n_seq + i) * pages_per_seq + j]

    def page_spec(k):
        page = page_of(k)
        return pl.BlockSpec((kv_rows, PAGE_SIZE), lambda b, pt: (page(b, pt), 0))

    def logf_spec(k):
        page = page_of(k)
        return pl.BlockSpec((None,) + logf_t.shape[1:], lambda b, pt: (page(b, pt), 0, 0))

    total = n_seq * pages_per_seq
    grid_spec = pltpu.PrefetchScalarGridSpec(
        num_scalar_prefetch=1,
        grid=(n_batch // n_seq,),
        in_specs=[pl.BlockSpec((n_seq * s_new, qb.shape[1]), row), pl.BlockSpec((n_seq * s_new, LANES), row),
                  pl.BlockSpec((n_seq * s_new, foxn.shape[1]), row)]
                 + [page_spec(k) for k in range(total)]
                 + [logf_spec(k) for k in range(total)]
                 + [pl.BlockSpec(psel.shape, fixed), pl.BlockSpec(triu.shape, fixed), pl.BlockSpec(eye.shape, fixed)],
        out_specs=pl.BlockSpec((n_seq * s_new, qb.shape[1]), row),
    )
    return pl.pallas_call(
        functools.partial(_fox_sample_kernel, n_pages=pages_per_seq, n_seq=n_seq, past=past),
        grid_spec=grid_spec,
        out_shape=jax.ShapeDtypeStruct(qb.shape, F32),
        compiler_params=_params(("arbitrary",)),
        name="fox_sample",
    )(pt_flat, qb, small, foxn, *([pool_t] * total), *([logf_t] * total), psel, triu, eye)


def _merge_kernel(x_ref, oa_ref, ob_ref, z_ref, w_ref, y_ref):
    d_a = oa_ref.shape[1]
    z = z_ref[...]
    gate = z * _sigmoid(z)
    u = jnp.concatenate([oa_ref[...] * gate[:, 0:d_a], ob_ref[...] * gate[:, d_a:]], axis=1).astype(BF16)
    y_ref[...] = x_ref[...] + _dot(u, w_ref[...])


def _merge_call(x2d, oa, ob, z, w_out_bf, tm, name):
    n, d = x2d.shape
    row = lambda i: (i, 0)
    return pl.pallas_call(
        _merge_kernel,
        grid=(n // tm,),
        in_specs=[pl.BlockSpec((tm, d), row), pl.BlockSpec((tm, oa.shape[1]), row), pl.BlockSpec((tm, ob.shape[1]), row),
                  pl.BlockSpec((tm, z.shape[1]), row), pl.BlockSpec(w_out_bf.shape, lambda i: (0, 0))],
        out_specs=pl.BlockSpec((tm, d), row),
        out_shape=jax.ShapeDtypeStruct((n, d), F32),
        compiler_params=_params(("arbitrary",)),
        name=name,
    )(x2d, oa, ob, z, w_out_bf)


def _row_tile(n, preferred):
    return preferred if n % preferred == 0 else n


def _rope_tables(pos):
    half = HEAD_DIM // 2
    inv = jnp.power(jnp.float32(ROPE_THETA), -jnp.arange(half, dtype=F32) / half)
    ang = pos.astype(F32)[:, None] * inv[None, :]
    cos = jnp.cos(ang)
    sin = jnp.sin(ang)
    return jnp.tile(cos, (1, 4)), jnp.tile(jnp.concatenate([-sin, sin], axis=1), (1, 2))


def _cmp_to_sel(n_cmp, n_sel, rows, cols):
    cs = np.arange(n_cmp) * CMP_STRIDE
    ss = np.arange(n_sel) * SEL_BLOCK
    ov = np.clip(np.minimum(cs[:, None] + CMP_LEN, ss[None, :] + SEL_BLOCK) - np.maximum(cs[:, None], ss[None, :]), 0, None)
    m = np.zeros((rows, cols), np.float32)
    m[:n_cmp, :n_sel] = ov / CMP_LEN
    return jnp.asarray(m, dtype=BF16)


def _block_expand(n_keys):
    e = np.zeros((LANES, n_keys), np.float32)
    k = np.arange(n_keys)
    e[k // SEL_BLOCK, k] = 1.0
    return jnp.asarray(e, dtype=BF16)


def _gate_expand():
    e = np.zeros((LANES, 3 * N_HEADS * HEAD_DIM), np.float32)
    for i in range(3):
        for h in range(N_HEADS):
            e[i * N_HEADS + h, i * N_HEADS * HEAD_DIM + h * HEAD_DIM:i * N_HEADS * HEAD_DIM + (h + 1) * HEAD_DIM] = 1.0
    return jnp.asarray(e, dtype=BF16)


def _head_expand():
    e = np.zeros((LANES, N_HEADS * HEAD_DIM), np.float32)
    for h in range(N_HEADS):
        e[LOGF_LANE + h, h * HEAD_DIM:(h + 1) * HEAD_DIM] = 1.0
    return jnp.asarray(e, dtype=BF16)


def _compress_weights(w1, w2, pe):
    half = CMP_LEN // 2
    eye = jnp.eye(G_NSA, dtype=F32)
    w = w1.reshape(2, half, HEAD_DIM, CMP_HID)
    w1x = jnp.einsum('pldh,gk->lgdpkh', w, eye).reshape(half * G_NSA * HEAD_DIM, 2 * G_NSA * CMP_HID).astype(BF16)
    w2x = jnp.einsum('hd,gk->ghkd', w2, eye).reshape(G_NSA * CMP_HID, G_NSA * HEAD_DIM).astype(BF16)
    pe2 = jnp.broadcast_to(pe.reshape(2, half, 1, HEAD_DIM), (2, half, G_NSA, HEAD_DIM)).reshape(2, -1)
    pex = jnp.concatenate([pe2, jnp.zeros((14, pe2.shape[1]), F32)], axis=0).astype(BF16)
    return w1x, w2x, pex


def _stored_tiles(cache):
    return jnp.transpose(cache, (0, 2, 3, 4, 1)).reshape(-1, cache.shape[1])


def kernel(x_prompt, x_sample, cache_nsa_cmp_kv, cache_nsa_slc_kv, cache_nsa_win_kv, cache_fox_kv, cache_fox_logf,
           page_table, g_norm, w_in, b_f, gq_a, gk_cmp, gk_slc, gk_win, pe_cmp_k, pe_cmp_v,
           w_cmp1_k, w_cmp2_k, w_cmp1_v, w_cmp2_v, gq_b, gk_b, w_out):
    depth = g_norm.shape[0]
    assert depth == 1, "single-layer stack"
    n_b, seq, d_model = x_prompt.shape
    n_db, s_new, _ = x_sample.shape
    pages_per_seq = page_table.shape[1]
    past = pages_per_seq * PAGE_SIZE
    wb = cache_nsa_win_kv.shape[2]
    n_pool = cache_nsa_cmp_kv.shape[1]
    d_a = N_HEADS * HEAD_DIM
    kv_w = 2 * G_NSA * HEAD_DIM

    w = w_in[0]
    o_kc, o_ga, o_za, o_qb, o_kb, o_fb, o_zb = 512, 1280, 1304, 1816, 2328, 3352, 3360
    gate_cols = o_ga + (np.arange(N_HEADS)[None, :] * 3 + np.arange(3)[:, None]).reshape(-1)
    small_w = jnp.concatenate([w[:, gate_cols], w[:, o_fb:o_fb + N_HEADS],
                               jnp.zeros((d_model, LANES - GATE_LANES - N_HEADS), F32)], axis=1)
    w_bf = jnp.concatenate([w[:, 0:o_ga], w[:, o_za:o_qb], w[:, o_zb:o_zb + d_a], w[:, o_qb:o_fb], small_w],
                           axis=1).astype(BF16)
    assert w_bf.shape[1] == C_END
    gains = jnp.concatenate([jnp.tile(v[0], 2)[None, :] for v in (gq_a, gk_slc, gk_win, gq_b, gk_b)]
                            + [jnp.zeros((3, LANES), F32)], axis=0)
    bf_row = jnp.zeros((1, LANES), F32).at[0, LOGF_LANE:LOGF_LANE + N_HEADS].set(b_f[0])
    lane_head = np.arange(2 * LANES) // HEAD_DIM
    ones_bd2 = jnp.asarray(lane_head[:, None] == lane_head[None, :], dtype=BF16)
    ones_bd = ones_bd2[0:LANES, 0:LANES]
    hexp = _head_expand()
    gexp = _gate_expand()
    consts = lambda tm: (g_norm, w_bf, gains, bf_row, ones_bd2,
                         jnp.asarray(np.tril(np.ones((tm, tm), np.float32)), dtype=BF16), hexp)

    tm_p = 256
    cos_p, sin_p = _rope_tables(jnp.arange(seq, dtype=jnp.int32))
    tiles_per_seq = seq // tm_p
    (cmp_p, slc_p, win_p, fox_p, small_p, z_p, qa_p, qb_p, cexp_p, csum_p, cmpd_p, slcb_p, winb_p, foxb_p) = _proj_call(
        x_prompt.reshape(n_b * seq, d_model), (cos_p, sin_p, lambda i: (i % tiles_per_seq, 0)),
        consts(tm_p), tiles_per_seq, tm_p, "proj_prompt", True, LOG2_E)
    tm_s = _row_tile(n_db * s_new, 256)
    cos_s, sin_s = _rope_tables(past + jnp.arange(s_new, dtype=jnp.int32))
    cos_s, sin_s = jnp.tile(cos_s, (tm_s // s_new, 1)), jnp.tile(sin_s, (tm_s // s_new, 1))
    (cmp_s, slc_s, win_s, fox_s, small_s, z_s, qa_s, qb_s, _, _, _, _, _, _) = _proj_call(
        x_sample.reshape(n_db * s_new, d_model), (cos_s, sin_s, lambda i: (0, 0)),
        consts(tm_s), 1, tm_s, "proj_sample", False, 1.0)

    w1k, w2k, pek = _compress_weights(w_cmp1_k[0], w_cmp2_k[0], pe_cmp_k[0])
    w1v, w2v, pev = _compress_weights(w_cmp1_v[0], w_cmp2_v[0], pe_cmp_v[0])
    cw = (w1k, w1v, pek, pev, w2k, w2v, jnp.tile(gk_cmp[0], 2)[None, :], ones_bd)
    chunk_w = CMP_STRIDE * kv_w
    ckv_p = _compress_prompt_call(cmpd_p.reshape(n_b * seq // CMP_STRIDE, chunk_w), cw, seq // CMP_STRIDE)
    pt_flat = page_table.reshape(-1).astype(jnp.int32)
    tok = np.arange(PAGE_SIZE)
    perm = np.zeros((PAGE_SIZE, PAGE_SIZE), np.float32)
    perm[(tok % CMP_STRIDE) * (PAGE_SIZE // CMP_STRIDE) + tok // CMP_STRIDE, tok] = 1.0
    ckv_s = _compress_sample_call(pt_flat, _stored_tiles(cache_nsa_cmp_kv[0]), jnp.asarray(perm, dtype=BF16), cw,
                                  n_db, pages_per_seq, next(n for n in (4, 2, 1) if n_db % n == 0))

    n_cmp_p = (seq - CMP_LEN) // CMP_STRIDE + 1
    msel_p = _cmp_to_sel(n_cmp_p, seq // SEL_BLOCK, seq // CMP_STRIDE, LANES)
    tq_a = 512
    eblk_p = jnp.transpose(_block_expand(seq).reshape(LANES, seq // tq_a, tq_a), (1, 0, 2))
    oa_p = _nsa_prompt_call(qa_p, small_p, ckv_p, slcb_p, winb_p, msel_p, eblk_p, gexp, n_b, seq, tq_a)
    n_cmp_s = (past + s_new - CMP_LEN) // CMP_STRIDE + 1
    n_sel_s = -(-(past + s_new) // SEL_BLOCK)
    msel_s = _cmp_to_sel(n_cmp_s, n_sel_s, past // CMP_STRIDE, LANES)
    eye_kv = jnp.asarray(np.eye(kv_w, dtype=np.float32), dtype=BF16)
    oa_s, new_win_t = _nsa_sample_call(
        pt_flat, qa_s, small_s, ckv_s, _stored_tiles(cache_nsa_slc_kv[0]), slc_s,
        _stored_tiles(cache_nsa_win_kv[0]), win_s, msel_s, _block_expand(past + PAGE_SIZE), gexp, eye_kv,
        n_db, s_new, pages_per_seq)
    new_win = jnp.transpose(new_win_t.reshape(n_db, 2, G_NSA, HEAD_DIM, wb), (0, 4, 1, 2, 3))[None]

    tq_f = 512
    c_p = csum_p[:, LOGF_LANE:LOGF_LANE + N_HEADS].reshape(n_b, seq, N_HEADS)
    ckt = jnp.transpose(c_p, (0, 2, 1)).reshape(n_b, N_HEADS, seq // tq_f, tq_f)
    ob_p = _fox_prompt_call(qb_p, foxb_p, cexp_p, ckt, n_b, seq, tq_f)
    psel = jnp.asarray(np.arange(LANES)[None, :] == (LOGF_LANE + np.arange(2 * N_HEADS))[:, None], dtype=BF16)
    psel = psel.at[N_HEADS:].set(0)
    triu = jnp.asarray(np.triu(np.ones((LANES, LANES), np.float32)), dtype=BF16)
    eye = jnp.asarray(np.eye(LANES, dtype=np.float32), dtype=BF16)
    ob_s = _fox_sample_call(
        pt_flat, qb_s, small_s, fox_s, _stored_tiles(cache_fox_kv[0]),
        jnp.transpose(cache_fox_logf[0], (0, 2, 1)), psel, triu, eye, n_db, s_new, pages_per_seq)

    w_out_bf = w_out[0].astype(BF16)
    y_p = _merge_call(x_prompt.reshape(n_b * seq, d_model), oa_p, ob_p, z_p, w_out_bf, 512, "merge_prompt")
    y_s = _merge_call(x_sample.reshape(n_db * s_new, d_model), oa_s, ob_s, z_s, w_out_bf,
                      _row_tile(n_db * s_new, 512), "merge_sample")

    kv5 = lambda a, nb, t, heads: a.reshape(1, nb, t, 2, heads, HEAD_DIM)
    kv5_t = lambda a, heads: jnp.transpose(a.reshape(n_b, 2, heads, HEAD_DIM, -1), (0, 4, 1, 2, 3))[None]
    wbp = min(WINDOW, seq)
    return (y_p.reshape(n_b, seq, d_model), y_s.reshape(n_db, s_new, d_model),
            kv5_t(cmp_p, G_NSA), kv5(cmp_s, n_db, s_new, G_NSA),
            kv5_t(slc_p, G_NSA), kv5(slc_s, n_db, s_new, G_NSA),
            kv5_t(win_p.reshape(n_b, kv_w, seq)[:, :, seq - wbp:], G_NSA), new_win,
            kv5_t(fox_p, N_HEADS), kv5(fox_s, n_db, s_new, N_HEADS),
            small_p[:, LOGF_LANE:LOGF_LANE + N_HEADS].reshape(1, n_b, seq, N_HEADS),
            small_s[:, LOGF_LANE:LOGF_LANE + N_HEADS].reshape(1, n_db, s_new, N_HEADS))
```
